```python
import math
import jax
import jax.numpy as jnp
from jax import lax
import numpy as np

D_MODEL = 1024
BATCH = 2
SEQ = 16384
DEPTH = 4
DEC_BATCH = 32
DEC_SEQ = 32
PAST_LEN = 4096

CHUNK = 64
N_MIXERS = 4
N_CYCLE = DEPTH // N_MIXERS
N_DENSE = (DEPTH + 1) // 2
N_MOE = DEPTH // 2
P_DIM = 256
EPS = 1e-6
CONV_W = 4

A_HEADS = 16
A_HEAD_DIM = D_MODEL // A_HEADS
A_PREV_CHUNKS = 8
A_REACH = A_PREV_CHUNKS * CHUNK
A_BAND = A_REACH + CHUNK
A_REL_CLIP = 128
A_SCALE = A_HEAD_DIM ** -0.5

B_HEADS = 8
B_HEAD_DIM = D_MODEL // B_HEADS
B_QKV = 3 * D_MODEL
B_PROJ = B_QKV + D_MODEL + 2 * B_HEADS
B_SCALE = B_HEAD_DIM ** -0.5

C_WIDTH = D_MODEL
C_BLOCKS = 8
C_BLOCK = C_WIDTH // C_BLOCKS
LRU_C = 8.0

D_HALF = 2 * D_MODEL
D_GROUPS = 8
D_GROUP = D_HALF // D_GROUPS
D_CHUNK = 128

D_FF = 2816
N_EXPERTS = 8
TOP_K = 2
E_FF = D_MODEL

STATE_NAMES = ('k_a', 'v_a', 's_b', 'conv_b', 'h_c', 'conv_c', 'v_d')

kernel_name = 'hybrid_streaming_encoder_step'


def _rmsnorm(x, g):
    xf = x.astype(jnp.float32)
    y = xf * lax.rsqrt(jnp.mean(xf * xf, axis=-1, keepdims=True) + EPS)
    return (y * g.astype(jnp.float32)).astype(x.dtype)


def _layernorm(x, g, b):
    xf = x.astype(jnp.float32)
    xc = xf - jnp.mean(xf, axis=-1, keepdims=True)
    y = xc * lax.rsqrt(jnp.mean(xc * xc, axis=-1, keepdims=True) + EPS)
    return (y * g.astype(jnp.float32) + b.astype(jnp.float32)).astype(x.dtype)


def _l2norm(x):
    xf = x.astype(jnp.float32)
    return xf * lax.rsqrt(jnp.sum(xf * xf, axis=-1, keepdims=True) + EPS)


def _causal_dwconv(x, hist, w):
    t = x.shape[1]
    xp = jnp.concatenate([hist.astype(x.dtype), x], axis=1)
    y = xp[:, 0:t] * w[0]
    for k in range(1, CONV_W):
        y = y + xp[:, k:k + t] * w[k]
    return y, xp[:, t:]


def _rel_bias(table, q_pos, k_pos):
    d = jnp.clip(q_pos[:, None] - k_pos[None, :], -A_REL_CLIP, A_REL_CLIP) + A_REL_CLIP
    return table[:, d].astype(jnp.float32)


def _qkv_a(h, w_in, g_q, g_k):
    bn, t, _ = h.shape
    qkv = (h @ w_in).reshape(bn, t, 3, A_HEADS, A_HEAD_DIM)
    return _rmsnorm(qkv[:, :, 0], g_q), _rmsnorm(qkv[:, :, 1], g_k), qkv[:, :, 2]


def _band_attn_prompt(q, k, v, table):
    bn, t, nh, dh = q.shape
    nc = t // CHUNK
    pad = jnp.zeros((bn, A_REACH, nh, dh), k.dtype)
    kp = jnp.concatenate([pad, k], axis=1)
    vp = jnp.concatenate([pad, v], axis=1)
    band = jnp.arange(A_BAND)
    bias = _rel_bias(table, A_REACH + jnp.arange(CHUNK), band)
    qc = jnp.moveaxis(q.reshape(bn, nc, CHUNK, nh, dh), 1, 0)

    def one_chunk(args):
        c, qb = args
        kb = lax.dynamic_slice_in_dim(kp, c * CHUNK, A_BAND, axis=1)
        vb = lax.dynamic_slice_in_dim(vp, c * CHUNK, A_BAND, axis=1)
        s = jnp.einsum('bqhd,bkhd->bhqk', qb, kb, preferred_element_type=jnp.float32) * A_SCALE + bias
        valid = c * CHUNK + band >= A_REACH
        s = jnp.where(valid, s, -jnp.inf)
        pr = jax.nn.softmax(s, axis=-1).astype(vb.dtype)
        return jnp.einsum('bhqk,bkhd->bqhd', pr, vb)

    o = lax.map(one_chunk, (jnp.arange(nc), qc))
    return jnp.moveaxis(o, 0, 1).reshape(bn, t, nh * dh)


def _band_attn_sample(q, k, v, ck, cv, table):
    bn, t, nh, dh = q.shape
    w = ck.shape[1]
    kk = jnp.concatenate([ck.astype(k.dtype), k], axis=1)
    vv = jnp.concatenate([cv.astype(v.dtype), v], axis=1)
    bias = _rel_bias(table, w + jnp.arange(t), jnp.arange(w + t))
    s = jnp.einsum('bqhd,bkhd->bhqk', q, kk, preferred_element_type=jnp.float32) * A_SCALE + bias
    pr = jax.nn.softmax(s, axis=-1).astype(vv.dtype)
    return jnp.einsum('bhqk,bkhd->bqhd', pr, vv).reshape(bn, t, nh * dh)


def _delta_rule_chunked(q, k, v, g, beta, s0):
    bn, t, nh, dk = q.shape
    dv = v.shape[-1]
    n = t // CHUNK

    def blk(a):
        a = a.astype(jnp.float32).reshape(bn, n, CHUNK, *a.shape[2:])
        return jnp.moveaxis(a, 3, 2)

    q, k, v, g, beta = blk(q), blk(k), blk(v), blk(g), blk(beta)
    gc = jnp.cumsum(g, axis=-1)
    tri = jnp.tril(jnp.ones((CHUNK, CHUNK), bool))
    strict = jnp.tril(jnp.ones((CHUNK, CHUNK), bool), -1)
    decay = jnp.exp(jnp.where(tri, gc[..., :, None] - gc[..., None, :], -jnp.inf))
    kb = k * beta[..., None]
    lmat = jnp.where(strict, jnp.einsum('bnhcd,bnhsd->bnhcs', kb, k) * decay, 0.0)
    amat = lmat + jnp.eye(CHUNK, dtype=jnp.float32)
    rhs = jnp.concatenate([v * beta[..., None], kb * jnp.exp(gc)[..., None]], axis=-1)
    sol = lax.linalg.triangular_solve(amat, rhs, left_side=True, lower=True, unit_diagonal=True)
    u, wk = sol[..., :dv], sol[..., dv:]
    qk = jnp.einsum('bnhcd,bnhsd->bnhcs', q, k) * decay
    q_dec = q * jnp.exp(gc)[..., None]
    k_dec = k * jnp.exp(gc[..., -1:] - gc)[..., None]
    g_last = jnp.exp(gc[..., -1])

    def step(s, xs):
        u_n, w_n, qk_n, qd_n, kd_n, gl_n = xs
        v_new = u_n - jnp.einsum('bhck,bhkv->bhcv', w_n, s)
        o_n = jnp.einsum('bhck,bhkv->bhcv', qd_n, s) + jnp.einsum('bhcs,bhsv->bhcv', qk_n, v_new)
        s = s * gl_n[..., None, None] + jnp.einsum('bhck,bhcv->bhkv', kd_n, v_new)
        return s, o_n

    xs = tuple(jnp.moveaxis(a, 1, 0) for a in (u, wk, qk, q_dec, k_dec, g_last))
    s, o = lax.scan(step, s0.astype(jnp.float32), xs)
    o = jnp.moveaxis(jnp.moveaxis(o, 0, 1), 2, 3).reshape(bn, t, nh, dv)
    return o, s


def _pad_time(a, tp):
    return jnp.pad(a, [(0, 0), (0, tp - a.shape[1])] + [(0, 0)] * (a.ndim - 2))


def _gated_deltanet(h, hist, s0, w_in, conv_w, a_log, dt_bias, g_o, w_out):
    bn, t, _ = h.shape
    proj = h @ w_in
    qkv, z, b_raw, a_raw = jnp.split(proj, [B_QKV, B_QKV + D_MODEL, B_QKV + D_MODEL + B_HEADS], axis=-1)
    qkv, new_hist = _causal_dwconv(qkv, hist, conv_w)
    qkv = jax.nn.silu(qkv).reshape(bn, t, 3, B_HEADS, B_HEAD_DIM)
    q = _l2norm(qkv[:, :, 0]) * B_SCALE
    k = _l2norm(qkv[:, :, 1])
    v = qkv[:, :, 2].astype(jnp.float32)
    beta = jax.nn.sigmoid(b_raw.astype(jnp.float32))
    g = -jnp.exp(a_log.astype(jnp.float32)) * jax.nn.softplus(
        a_raw.astype(jnp.float32) + dt_bias.astype(jnp.float32))
    tp = -(-t // CHUNK) * CHUNK
    o, s = _delta_rule_chunked(_pad_time(q, tp), _pad_time(k, tp), _pad_time(v, tp),
                               _pad_time(g, tp), _pad_time(beta, tp), s0)
    o = _rmsnorm(o[:, :t], g_o) * jax.nn.silu(z.astype(jnp.float32)).reshape(bn, t, B_HEADS, B_HEAD_DIM)
    return o.reshape(bn, t, D_MODEL).astype(h.dtype) @ w_out, s, new_hist


def _lin_combine(left, right):
    a_l, b_l = left
    a_r, b_r = right
    return a_l * a_r, a_r * b_l + b_r


def _rglru_block(h, hist, h0, w_in, conv_w, conv_bias, w_ra, b_ra, w_ri, b_ri, lam, w_out):
    bn, t, _ = h.shape
    y_br, x_br = jnp.split(h @ w_in, 2, axis=-1)
    xc, new_hist = _causal_dwconv(x_br, hist, conv_w)
    xc = (xc + conv_bias).astype(jnp.float32)
    xb = xc.reshape(bn, t, C_BLOCKS, C_BLOCK)
    r = jax.nn.sigmoid(jnp.einsum('btnc,ncd->btnd', xb, w_ra.astype(jnp.float32)).reshape(bn, t, C_WIDTH) + b_ra)
    i = jax.nn.sigmoid(jnp.einsum('btnc,ncd->btnd', xb, w_ri.astype(jnp.float32)).reshape(bn, t, C_WIDTH) + b_ri)
    log_a = -LRU_C * r * jax.nn.softplus(-lam.astype(jnp.float32))
    a = jnp.exp(log_a)
    b = jnp.sqrt(-jnp.expm1(2.0 * log_a)) * (i * xc)
    b = b.at[:, 0].add(a[:, 0] * h0.astype(jnp.float32))
    _, hs = lax.associative_scan(_lin_combine, (a, b), axis=1)
    out = (hs.astype(h.dtype) * jax.nn.gelu(y_br)) @ w_out
    return out, hs[:, -1], new_hist


def _spatial_gating_mlp(h, w_in, b_in, ln_g, ln_b, w_s, b_s, w_out):
    bn, t, _ = h.shape
    u, v = jnp.split(jax.nn.gelu(h @ w_in + b_in), 2, axis=-1)
    vn = _layernorm(v, ln_g, ln_b)
    lc = min(t, D_CHUNK)
    n = t // lc
    ws = jnp.where(jnp.tril(jnp.ones((lc, lc), bool)), w_s[:, :lc, :lc], 0.0)
    s = jnp.einsum('gij,bnjgc->bnigc', ws, vn.reshape(bn, n, lc, D_GROUPS, D_GROUP))
    s = s + jnp.transpose(b_s[:, :lc])[None, None, :, :, None]
    out = (u * s.reshape(bn, t, D_HALF).astype(u.dtype)) @ w_out
    return out, vn


def _swiglu(x, w_g, w_u, w_d):
    return (jax.nn.silu(x @ w_g) * (x @ w_u)) @ w_d


def _moe_swiglu(x, w_router, w_g, w_u, w_d):
    bn, t, d = x.shape
    xf = x.reshape(bn * t, d)
    logits = (xf @ w_router).astype(jnp.float32)
    top_v, top_i = lax.top_k(logits, TOP_K)
    top_w = jax.nn.softmax(top_v, axis=-1)
    gates = jnp.sum(jax.nn.one_hot(top_i, N_EXPERTS, dtype=jnp.float32) * top_w[..., None], axis=1)
    hid = jax.nn.silu(jnp.einsum('nd,edf->nef', xf, w_g)) * jnp.einsum('nd,edf->nef', xf, w_u)
    hid = hid * gates[..., None].astype(hid.dtype)
    return jnp.einsum('nef,efd->nd', hid, w_d).reshape(bn, t, d)


def _trunk(x, p, st, prm):
    bn, t, _ = x.shape
    first = st is None
    dt = x.dtype
    out = {name: [] for name in STATE_NAMES}
    h = x
    for i in range(DEPTH):
        kind, c = i % N_MIXERS, i // N_MIXERS
        hn = _rmsnorm(h, prm['g_mix'][i])
        if kind == 0:
            q, k, v = _qkv_a(hn, prm['w_in_a'][c], prm['g_q_a'][c], prm['g_k_a'][c])
            if first:
                o = _band_attn_prompt(q, k, v, prm['rel_a'][c])
                keep = min(A_REACH, t)
                k_rows, v_rows = k[:, t - keep:], v[:, t - keep:]
            else:
                o = _band_attn_sample(q, k, v, st['k_a'][c], st['v_a'][c], prm['rel_a'][c])
                k_rows, v_rows = k, v
            out['k_a'].append(k_rows)
            out['v_a'].append(v_rows)
            mix = o @ prm['w_out_a'][c]
        elif kind == 1:
            hist = jnp.zeros((bn, CONV_W - 1, B_QKV), dt) if first else st['conv_b'][c]
            s0 = jnp.zeros((bn, B_HEADS, B_HEAD_DIM, B_HEAD_DIM), jnp.float32) if first else st['s_b'][c]
            mix, s, hist = _gated_deltanet(hn, hist, s0, prm['w_in_b'][c], prm['conv_b'][c], prm['a_log_b'][c],
                                           prm['dt_bias_b'][c], prm['g_o_b'][c], prm['w_out_b'][c])
            out['s_b'].append(s.astype(dt))
            out['conv_b'].append(hist)
        elif kind == 2:
            hist = jnp.zeros((bn, CONV_W - 1, C_WIDTH), dt) if first else st['conv_c'][c]
            h0 = jnp.zeros((bn, C_WIDTH), jnp.float32) if first else st['h_c'][c]
            mix, hl, hist = _rglru_block(hn, hist, h0, prm['w_in_c'][c], prm['conv_c'][c], prm['conv_bias_c'][c],
                                         prm['w_ra_c'][c], prm['b_ra_c'][c], prm['w_ri_c'][c], prm['b_ri_c'][c],
                                         prm['lam_c'][c], prm['w_out_c'][c])
            out['h_c'].append(hl.astype(dt))
            out['conv_c'].append(hist)
        else:
            mix, vn = _spatial_gating_mlp(hn, prm['w_in_d'][c], prm['b_in_d'][c], prm['ln_g_d'][c], prm['ln_b_d'][c],
                                          prm['w_s_d'][c], prm['b_s_d'][c], prm['w_out_d'][c])
            if not first:
                out['v_d'].append(vn)
        h = h + mix
        hn = _rmsnorm(h, prm['g_ffn'][i])
        j = i // 2
        if i % 2 == 0:
            h = h + _swiglu(hn, prm['w_gate_ff'][j], prm['w_up_ff'][j], prm['w_down_ff'][j])
        else:
            h = h + _moe_swiglu(hn, prm['w_router'][j], prm['w_gate_e'][j], prm['w_up_e'][j], prm['w_down_e'][j])
        gate = jax.nn.sigmoid(_rmsnorm(h, prm['g_ple'][i]) @ prm['w_ple_gate'][i])
        h = h + (p[i] @ prm['w_ple'][i]) * gate
    new = {name: jnp.stack(rows) for name, rows in out.items() if rows}
    return _rmsnorm(h, prm['g_final']), new


def setup_inputs(seed: int = 0) -> dict:
    key = jax.random.key(seed)
    keys = iter(jax.random.split(key, 80))

    def nrm(shape, scale):
        return jax.random.normal(next(keys), shape, jnp.float32) * scale

    def gain(shape):
        return 1.0 + nrm(shape, 0.02)

    def unif(shape, lo, hi):
        return jax.random.uniform(next(keys), shape, jnp.float32, lo, hi)

    n = N_CYCLE
    kv_keep = min(A_REACH, PAST_LEN)
    a_log_b = jnp.log(unif((n, B_HEADS), 1.0, 16.0))
    dt0 = jnp.exp(unif((n, B_HEADS), math.log(1e-3), math.log(1e-1)))
    dt_bias_b = dt0 + jnp.log(-jnp.expm1(-dt0))
    a_c = unif((n, C_WIDTH), 0.9, 0.999) ** (1.0 / LRU_C)
    lam_c = jnp.log(a_c) - jnp.log1p(-a_c)
    return {
        'x_prompt': nrm((BATCH, SEQ, D_MODEL), 1.0),
        'x_sample': nrm((DEC_BATCH, DEC_SEQ, D_MODEL), 1.0),
        'p_prompt': nrm((DEPTH, BATCH, SEQ, P_DIM), 1.0),
        'p_sample': nrm((DEPTH, DEC_BATCH, DEC_SEQ, P_DIM), 1.0),
        'cache_k_a': nrm((n, DEC_BATCH, kv_keep, A_HEADS, A_HEAD_DIM), 1.0),
        'cache_v_a': nrm((n, DEC_BATCH, kv_keep, A_HEADS, A_HEAD_DIM), 1.0),
        'state_s_b': nrm((n, DEC_BATCH, B_HEADS, B_HEAD_DIM, B_HEAD_DIM), 0.1),
        'state_conv_b': nrm((n, DEC_BATCH, CONV_W - 1, B_QKV), 1.0),
        'state_h_c': nrm((n, DEC_BATCH, C_WIDTH), 0.5),
        'state_conv_c': nrm((n, DEC_BATCH, CONV_W - 1, C_WIDTH), 1.0),
        'g_mix': gain((DEPTH, D_MODEL)),
        'g_ffn': gain((DEPTH, D_MODEL)),
        'g_ple': gain((DEPTH, D_MODEL)),
        'g_final': gain((D_MODEL,)),
        'w_in_a': nrm((n, D_MODEL, 3 * D_MODEL), D_MODEL ** -0.5),
        'g_q_a': gain((n, A_HEAD_DIM)),
        'g_k_a': gain((n, A_HEAD_DIM)),
        'rel_a': nrm((n, A_HEADS, 2 * A_REL_CLIP + 1), 0.5),
        'w_out_a': nrm((n, D_MODEL, D_MODEL), D_MODEL ** -0.5),
        'w_in_b': nrm((n, D_MODEL, B_PROJ), D_MODEL ** -0.5),
        'conv_b': nrm((n, CONV_W, B_QKV), CONV_W ** -0.5),
        'a_log_b': a_log_b,
        'dt_bias_b': dt_bias_b,
        'g_o_b': gain((n, B_HEAD_DIM)),
        'w_out_b': nrm((n, D_MODEL, D_MODEL), D_MODEL ** -0.5),
        'w_in_c': nrm((n, D_MODEL, 2 * C_WIDTH), D_MODEL ** -0.5),
        'conv_c': nrm((n, CONV_W, C_WIDTH), CONV_W ** -0.5),
        'conv_bias_c': nrm((n, C_WIDTH), 0.02),
        'w_ra_c': nrm((n, C_BLOCKS, C_BLOCK, C_BLOCK), C_BLOCK ** -0.5),
        'b_ra_c': nrm((n, C_WIDTH), 0.02),
        'w_ri_c': nrm((n, C_BLOCKS, C_BLOCK, C_BLOCK), C_BLOCK ** -0.5),
        'b_ri_c': nrm((n, C_WIDTH), 0.02),
        'lam_c': lam_c,
        'w_out_c': nrm((n, C_WIDTH, D_MODEL), C_WIDTH ** -0.5),
        'w_in_d': nrm((n, D_MODEL, 2 * D_HALF), D_MODEL ** -0.5),
        'b_in_d': nrm((n, 2 * D_HALF), 0.02),
        'ln_g_d': gain((n, D_HALF)),
        'ln_b_d': nrm((n, D_HALF), 0.02),
        'w_s_d': nrm((n, D_GROUPS, D_CHUNK, D_CHUNK), D_CHUNK ** -0.5),
        'b_s_d': 1.0 + nrm((n, D_GROUPS, D_CHUNK), 0.02),
        'w_out_d': nrm((n, D_HALF, D_MODEL), D_HALF ** -0.5),
        'w_gate_ff': nrm((N_DENSE, D_MODEL, D_FF), D_MODEL ** -0.5),
        'w_up_ff': nrm((N_DENSE, D_MODEL, D_FF), D_MODEL ** -0.5),
        'w_down_ff': nrm((N_DENSE, D_FF, D_MODEL), D_FF ** -0.5),
        'w_router': nrm((N_MOE, D_MODEL, N_EXPERTS), D_MODEL ** -0.5),
        'w_gate_e': nrm((N_MOE, N_EXPERTS, D_MODEL, E_FF), D_MODEL ** -0.5),
        'w_up_e': nrm((N_MOE, N_EXPERTS, D_MODEL, E_FF), D_MODEL ** -0.5),
        'w_down_e': nrm((N_MOE, N_EXPERTS, E_FF, D_MODEL), E_FF ** -0.5),
        'w_ple': nrm((DEPTH, P_DIM, D_MODEL), P_DIM ** -0.5),
        'w_ple_gate': nrm((DEPTH, D_MODEL, D_MODEL), D_MODEL ** -0.5),
    }


def reference(x_prompt, x_sample, p_prompt, p_sample, cache_k_a, cache_v_a, state_s_b, state_conv_b,
              state_h_c, state_conv_c, g_mix, g_ffn, g_ple, g_final, w_in_a, g_q_a, g_k_a, rel_a, w_out_a,
              w_in_b, conv_b, a_log_b, dt_bias_b, g_o_b, w_out_b, w_in_c, conv_c, conv_bias_c, w_ra_c, b_ra_c,
              w_ri_c, b_ri_c, lam_c, w_out_c, w_in_d, b_in_d, ln_g_d, ln_b_d, w_s_d, b_s_d, w_out_d,
              w_gate_ff, w_up_ff, w_down_ff, w_router, w_gate_e, w_up_e, w_down_e, w_ple, w_ple_gate):
    prm = {
        'g_mix': g_mix, 'g_ffn': g_ffn, 'g_ple': g_ple, 'g_final': g_final,
        'w_in_a': w_in_a, 'g_q_a': g_q_a, 'g_k_a': g_k_a, 'rel_a': rel_a, 'w_out_a': w_out_a,
        'w_in_b': w_in_b, 'conv_b': conv_b, 'a_log_b': a_log_b, 'dt_bias_b': dt_bias_b, 'g_o_b': g_o_b,
        'w_out_b': w_out_b,
        'w_in_c': w_in_c, 'conv_c': conv_c, 'conv_bias_c': conv_bias_c, 'w_ra_c': w_ra_c, 'b_ra_c': b_ra_c,
        'w_ri_c': w_ri_c, 'b_ri_c': b_ri_c, 'lam_c': lam_c, 'w_out_c': w_out_c,
        'w_in_d': w_in_d, 'b_in_d': b_in_d, 'ln_g_d': ln_g_d, 'ln_b_d': ln_b_d, 'w_s_d': w_s_d, 'b_s_d': b_s_d,
        'w_out_d': w_out_d,
        'w_gate_ff': w_gate_ff, 'w_up_ff': w_up_ff, 'w_down_ff': w_down_ff,
        'w_router': w_router, 'w_gate_e': w_gate_e, 'w_up_e': w_up_e, 'w_down_e': w_down_e,
        'w_ple': w_ple, 'w_ple_gate': w_ple_gate,
    }
    st = {'k_a': cache_k_a, 'v_a': cache_v_a, 's_b': state_s_b, 'conv_b': state_conv_b,
          'h_c': state_h_c, 'conv_c': state_conv_c}
    y_prompt, new_p = _trunk(x_prompt, p_prompt, None, prm)
    y_sample, new_s = _trunk(x_sample, p_sample, st, prm)
    return (y_prompt, y_sample,
            new_p['k_a'], new_p['v_a'], new_s['k_a'], new_s['v_a'],
            new_p['s_b'], new_p['conv_b'], new_s['s_b'], new_s['conv_b'],
            new_p['h_c'], new_p['conv_c'], new_s['h_c'], new_s['conv_c'],
            new_s['v_d'])
```

```python
import functools

import jax
import jax.numpy as jnp
from jax import lax
from jax.experimental import pallas as pl
from jax.experimental.pallas import tpu as pltpu

F32 = jnp.float32
BF16 = jnp.bfloat16
EPS = 1e-6
CHUNK = 64
CONV_W = 4
A_REL_CLIP = 128
A_PREV_CHUNKS = 8
LRU_C = 8.0
D_CHUNK = 128
TOP_K = 2
LANES = 128
SUBLANES = 8
VMEM_LIMIT = 48 * 1024 * 1024


def _cp(*sem, vmem=VMEM_LIMIT):
    return pltpu.CompilerParams(dimension_semantics=sem, vmem_limit_bytes=vmem)


def _tile_m(m):
    for t in (1024, 512, 256):
        if m % t == 0:
            return t
    raise ValueError(f"token count {m} must be a multiple of 256")


def _mm(a, b):
    return jnp.dot(a.astype(BF16), b.astype(BF16), preferred_element_type=F32)


def _mm_nt(a, b):
    return lax.dot_general(a.astype(BF16), b.astype(BF16), (((1,), (1,)), ((), ())),
                           preferred_element_type=F32)


def _mm_tn(a, b):
    return lax.dot_general(a.astype(BF16), b.astype(BF16), (((0,), (0,)), ((), ())),
                           preferred_element_type=F32)


def _split2(x):
    hi = x.astype(BF16)
    lo = (x - hi.astype(F32)).astype(BF16)
    return hi, lo


def _split3(x):
    hi = x.astype(BF16)
    r = x - hi.astype(F32)
    mid = r.astype(BF16)
    lo = (r - mid.astype(F32)).astype(BF16)
    return hi, mid, lo


def _mm3(a, b):
    ah, al = _split2(a)
    bh, bl = _split2(b)
    d = functools.partial(jnp.dot, preferred_element_type=F32)
    return d(ah, bh) + (d(ah, bl) + d(al, bh))


def _rms(x, g):
    return x * lax.rsqrt(jnp.mean(x * x, axis=-1, keepdims=True) + EPS) * g


def _sigmoid(x):
    return jax.nn.sigmoid(x)


def _silu(x):
    return x * _sigmoid(x)


def _softplus(x):
    return jnp.maximum(x, 0.0) + jnp.log1p(jnp.exp(-jnp.abs(x)))


def _gelu(x):
    c = 0.7978845608028654
    return x * (0.5 * (1.0 + jnp.tanh(c * (x + 0.044715 * (x * x * x)))))


def _shift_rows(x, prev8, s):
    r = pltpu.roll(x, s, axis=0)
    pr = pltpu.roll(prev8, s, axis=0)
    row8 = lax.broadcasted_iota(jnp.int32, pr.shape, 0)
    top = jnp.where(row8 < s, pr, r[0:SUBLANES])
    return jnp.concatenate([top, r[SUBLANES:]], axis=0)


def _dwconv(x, prev8, w):
    y = None
    for k in range(CONV_W):
        s = CONV_W - 1 - k
        xs = x if s == 0 else _shift_rows(x, prev8, s)
        term = xs * w[k:k + 1]
        y = term if y is None else y + term
    return y


def _proj_kernel(*refs, has_aux):
    if has_aux:
        x_ref, g_ref, w_ref, waux_ref, o_ref, oaux_ref, hn_ref = refs
    else:
        x_ref, g_ref, w_ref, o_ref, hn_ref = refs

    @pl.when(pl.program_id(1) == 0)
    def _():
        hn = _rms(x_ref[...], g_ref[...]).astype(BF16)
        hn_ref[...] = hn
        if has_aux:
            oaux_ref[...] = jnp.dot(hn, waux_ref[...], preferred_element_type=F32)

    o_ref[...] = jnp.dot(hn_ref[...], w_ref[...], preferred_element_type=F32)


def _proj(h, g, w, tn, waux=None):
    m, k = h.shape
    n = w.shape[1]
    tm = _tile_m(m)
    in_specs = [pl.BlockSpec((tm, k), lambda i, j: (i, 0)),
                pl.BlockSpec((1, k), lambda i, j: (0, 0)),
                pl.BlockSpec((k, tn), lambda i, j: (0, j))]
    out_shape = [jax.ShapeDtypeStruct((m, n), F32)]
    out_specs = [pl.BlockSpec((tm, tn), lambda i, j: (i, j))]
    args = [h, g.reshape(1, k), w]
    if waux is not None:
        na = waux.shape[1]
        in_specs.append(pl.BlockSpec((k, na), lambda i, j: (0, 0)))
        out_shape.append(jax.ShapeDtypeStruct((m, na), F32))
        out_specs.append(pl.BlockSpec((tm, na), lambda i, j: (i, 0)))
        args.append(waux)
    res = pl.pallas_call(
        functools.partial(_proj_kernel, has_aux=waux is not None),
        grid=(m // tm, n // tn),
        in_specs=in_specs, out_specs=out_specs, out_shape=out_shape,
        scratch_shapes=[pltpu.VMEM((tm, k), BF16)],
        compiler_params=_cp("parallel", "arbitrary"),
        name="proj",
    )(*args)
    return res if waux is not None else res[0]


def _outproj_kernel(h_ref, o_ref, w_ref, out_ref):
    out_ref[...] = h_ref[...] + jnp.dot(o_ref[...].astype(BF16), w_ref[...], preferred_element_type=F32)


def _outproj_res(h, o, w):
    m, d = h.shape
    k = o.shape[1]
    tm = _tile_m(m)
    return pl.pallas_call(
        _outproj_kernel,
        grid=(m // tm,),
        in_specs=[pl.BlockSpec((tm, d), lambda i: (i, 0)),
                  pl.BlockSpec((tm, k), lambda i: (i, 0)),
                  pl.BlockSpec((k, d), lambda i: (0, 0))],
        out_specs=pl.BlockSpec((tm, d), lambda i: (i, 0)),
        out_shape=jax.ShapeDtypeStruct((m, d), F32),
        compiler_params=_cp("parallel"),
        name="outproj_res",
    )(h, o, w)


def _ffn_kernel(x_ref, g_ref, wg_ref, wu_ref, wd_ref, o_ref, hn_ref):
    @pl.when(pl.program_id(1) == 0)
    def _():
        x = x_ref[...]
        hn_ref[...] = _rms(x, g_ref[...]).astype(BF16)
        o_ref[...] = x

    hn = hn_ref[...]
    gg = jnp.dot(hn, wg_ref[...], preferred_element_type=F32)
    uu = jnp.dot(hn, wu_ref[...], preferred_element_type=F32)
    hid = (_silu(gg) * uu).astype(BF16)
    o_ref[...] += jnp.dot(hid, wd_ref[...], preferred_element_type=F32)


def _ffn_dense(h, g, wg, wu, wd):
    m, d = h.shape
    f = wg.shape[1]
    tm = _tile_m(m)
    tf = 256
    return pl.pallas_call(
        _ffn_kernel,
        grid=(m // tm, f // tf),
        in_specs=[pl.BlockSpec((tm, d), lambda i, j: (i, 0)),
                  pl.BlockSpec((1, d), lambda i, j: (0, 0)),
                  pl.BlockSpec((d, tf), lambda i, j: (0, j)),
                  pl.BlockSpec((d, tf), lambda i, j: (0, j)),
                  pl.BlockSpec((tf, d), lambda i, j: (j, 0))],
        out_specs=pl.BlockSpec((tm, d), lambda i, j: (i, 0)),
        out_shape=jax.ShapeDtypeStruct((m, d), F32),
        scratch_shapes=[pltpu.VMEM((tm, d), BF16)],
        compiler_params=_cp("parallel", "arbitrary"),
        name="ffn_dense",
    )(h, g.reshape(1, d), wg, wu, wd)


def _moe_kernel(x_ref, g_ref, wr_ref, wg_ref, wu_ref, wd_ref, o_ref, hn_ref, gates_ref, *, n_experts):
    e = pl.program_id(1)
    f = pl.program_id(2)

    @pl.when((e == 0) & (f == 0))
    def _():
        x = x_ref[...]
        hn = _rms(x, g_ref[...])
        hn_ref[...] = hn.astype(BF16)
        o_ref[...] = x
        logits = _mm3(hn, wr_ref[...])
        lane = lax.broadcasted_iota(jnp.int32, logits.shape, 1)
        lg = jnp.where(lane < n_experts, logits, -jnp.inf)
        m1 = jnp.max(lg, axis=-1, keepdims=True)
        i1 = jnp.min(jnp.where(lg == m1, lane, LANES), axis=-1, keepdims=True)
        lg2 = jnp.where(lane == i1, -jnp.inf, lg)
        m2 = jnp.max(lg2, axis=-1, keepdims=True)
        i2 = jnp.min(jnp.where(lg2 == m2, lane, LANES), axis=-1, keepdims=True)
        e2 = jnp.exp(m2 - m1)
        w1 = 1.0 / (1.0 + e2)
        w2 = e2 / (1.0 + e2)
        gates_ref[...] = jnp.where(lane == i1, w1, 0.0) + jnp.where(lane == i2, w2, 0.0)

    hn = hn_ref[...]
    gates = gates_ref[...]
    lane = lax.broadcasted_iota(jnp.int32, gates.shape, 1)
    gate_e = jnp.sum(jnp.where(lane == e, gates, 0.0), axis=-1, keepdims=True)
    gg = jnp.dot(hn, wg_ref[0], preferred_element_type=F32)
    uu = jnp.dot(hn, wu_ref[0], preferred_element_type=F32)
    hid = (_silu(gg) * uu * gate_e).astype(BF16)
    o_ref[...] += jnp.dot(hid, wd_ref[0], preferred_element_type=F32)


def _moe(h, g, wr, wg, wu, wd):
    m, d = h.shape
    ne, _, ef = wg.shape
    tm = _tile_m(m)
    tf = 512
    return pl.pallas_call(
        functools.partial(_moe_kernel, n_experts=ne),
        grid=(m // tm, ne, ef // tf),
        in_specs=[pl.BlockSpec((tm, d), lambda i, e, j: (i, 0)),
                  pl.BlockSpec((1, d), lambda i, e, j: (0, 0)),
                  pl.BlockSpec((d, LANES), lambda i, e, j: (0, 0)),
                  pl.BlockSpec((1, d, tf), lambda i, e, j: (e, 0, j)),
                  pl.BlockSpec((1, d, tf), lambda i, e, j: (e, 0, j)),
                  pl.BlockSpec((1, tf, d), lambda i, e, j: (e, j, 0))],
        out_specs=pl.BlockSpec((tm, d), lambda i, e, j: (i, 0)),
        out_shape=jax.ShapeDtypeStruct((m, d), F32),
        scratch_shapes=[pltpu.VMEM((tm, d), BF16), pltpu.VMEM((tm, LANES), F32)],
        compiler_params=_cp("parallel", "arbitrary", "arbitrary"),
        name="moe",
    )(h, g.reshape(1, d), wr, wg, wu, wd)


def _ple_kernel(*refs, final):
    if final:
        x_ref, p_ref, g_ref, wp_ref, wg_ref, gf_ref, o_ref = refs
    else:
        x_ref, p_ref, g_ref, wp_ref, wg_ref, o_ref = refs
    x = x_ref[...]
    hn = _rms(x, g_ref[...]).astype(BF16)
    gate = _sigmoid(jnp.dot(hn, wg_ref[...], preferred_element_type=F32))
    pp = jnp.dot(p_ref[...].astype(BF16), wp_ref[...], preferred_element_type=F32)
    out = x + pp * gate
    o_ref[...] = _rms(out, gf_ref[...]) if final else out


def _ple(h, p, g, wp, wgate, g_final=None):
    m, d = h.shape
    pd = p.shape[1]
    tm = _tile_m(m)
    final = g_final is not None
    in_specs = [pl.BlockSpec((tm, d), lambda i: (i, 0)),
                pl.BlockSpec((tm, pd), lambda i: (i, 0)),
                pl.BlockSpec((1, d), lambda i: (0, 0)),
                pl.BlockSpec((pd, d), lambda i: (0, 0)),
                pl.BlockSpec((d, d), lambda i: (0, 0))]
    args = [h, p, g.reshape(1, d), wp, wgate]
    if final:
        in_specs.append(pl.BlockSpec((1, d), lambda i: (0, 0)))
        args.append(g_final.reshape(1, d))
    return pl.pallas_call(
        functools.partial(_ple_kernel, final=final),
        grid=(m // tm,),
        in_specs=in_specs,
        out_specs=pl.BlockSpec((tm, d), lambda i: (i, 0)),
        out_shape=jax.ShapeDtypeStruct((m, d), F32),
        compiler_params=_cp("parallel"),
        name="ple",
    )(*args)


def _pair_rms(x, g, hd):
    lane = lax.broadcasted_iota(jnp.int32, x.shape, 1)
    lo = lane < hd
    x2 = x * x
    s0 = jnp.sum(jnp.where(lo, x2, 0.0), axis=-1, keepdims=True)
    s1 = jnp.sum(jnp.where(lo, 0.0, x2), axis=-1, keepdims=True)
    ms = jnp.where(lo, s0, s1) * (1.0 / hd)
    return x * lax.rsqrt(ms + EPS) * g


def _stack_heads(q, hd):
    lane = lax.broadcasted_iota(jnp.int32, q.shape, 1)
    lo = lane < hd
    return jnp.concatenate([jnp.where(lo, q, 0.0), jnp.where(lo, 0.0, q)], axis=0).astype(BF16)


def _unstack_heads(o2, hd):
    n = o2.shape[0] // 2
    lane = lax.broadcasted_iota(jnp.int32, (n, o2.shape[1]), 1)
    return jnp.where(lane < hd, o2[:n], o2[n:])


def _attend(q2, kw, vw, bias, valid):
    s = _mm_nt(q2, kw) + bias
    if valid is not None:
        s = jnp.where(valid, s, -jnp.inf)
    mx = jnp.max(s, axis=-1, keepdims=True)
    e = jnp.exp(s - mx)
    den = jnp.sum(e, axis=-1, keepdims=True)
    return jnp.dot(e.astype(BF16), vw, preferred_element_type=F32) / den


def _attn_prompt_kernel(q_ref, kp_ref, kc_ref, vp_ref, vc_ref, b_ref, gq_ref, gk_ref, o_ref, kn_ref,
                        *, hd, scale, reach):
    i = pl.program_id(2)
    tq = q_ref.shape[0]
    band = reach + CHUNK
    qn = _pair_rms(q_ref[...], gq_ref[...], hd) * scale
    knc = _pair_rms(kc_ref[...], gk_ref[...], hd)
    knp = _pair_rms(kp_ref[...], gk_ref[...], hd)

    @pl.when(i == pl.num_programs(2) - 1)
    def _():
        kn_ref[...] = knc

    kcat = jnp.concatenate([knp, knc], axis=0).astype(BF16)
    vcat = jnp.concatenate([vp_ref[...], vc_ref[...]], axis=0).astype(BF16)
    bias = b_ref[0]
    col = lax.broadcasted_iota(jnp.int32, (1, band), 1)
    first_valid = jnp.where(i > 0, 0, tq)
    for jj in range(tq // CHUNK):
        lo_row = jj * CHUNK
        q2 = _stack_heads(qn[lo_row:lo_row + CHUNK], hd)
        w0 = tq - reach + lo_row
        valid = (col + w0) >= first_valid
        o2 = _attend(q2, kcat[w0:w0 + band], vcat[w0:w0 + band], bias, valid)
        o_ref[lo_row:lo_row + CHUNK, :] = _unstack_heads(o2, hd)


def _attn_sample_kernel(q_ref, k_ref, v_ref, ck_ref, cv_ref, b_ref, gq_ref, gk_ref, oin_ref, o_ref, kn_ref,
                        *, hd, scale):
    del oin_ref
    qn = _pair_rms(q_ref[...], gq_ref[...], hd) * scale
    kn = _pair_rms(k_ref[...], gk_ref[...], hd)
    kn_ref[...] = kn
    kcat = jnp.concatenate([ck_ref[...], kn], axis=0).astype(BF16)
    vcat = jnp.concatenate([cv_ref[...], v_ref[...]], axis=0).astype(BF16)
    o2 = _attend(_stack_heads(qn, hd), kcat, vcat, b_ref[0], None)
    o_ref[...] = _unstack_heads(o2, hd)


def _rel_bias_pairs(table, q_pos, k_pos):
    d = jnp.clip(q_pos[:, None] - k_pos[None, :], -A_REL_CLIP, A_REL_CLIP) + A_REL_CLIP
    b = table[:, d]
    nh, tq, tk = b.shape
    return b.reshape(nh // 2, 2 * tq, tk)


def _attention(qkv, n_b, t, n_s, t_s, cache_k, cache_v, rel, g_q, g_k, n_heads):
    m_tot, d3 = qkv.shape
    d = d3 // 3
    hd = d // n_heads
    pw = 2 * hd
    n_pairs = d // pw
    m_p = n_b * t
    reach = A_PREV_CHUNKS * CHUNK
    tq = reach
    assert pw == LANES and t % tq == 0 and t_s % 16 == 0
    nqb = t // tq
    scale = hd ** -0.5
    gq2 = jnp.tile(g_q, 2).reshape(1, pw)
    gk2 = jnp.tile(g_k, 2).reshape(1, pw)

    bias_p = _rel_bias_pairs(rel, reach + jnp.arange(CHUNK), jnp.arange(reach + CHUNK))
    kcol, vcol = n_pairs, 2 * n_pairs

    def rows(b, p, i):
        return b * nqb + i

    def rows_prev(b, p, i):
        return b * nqb + jnp.maximum(i - 1, 0)

    blk = (tq, pw)
    o_all, kn_p = pl.pallas_call(
        functools.partial(_attn_prompt_kernel, hd=hd, scale=scale, reach=reach),
        grid=(n_b, n_pairs, nqb),
        in_specs=[pl.BlockSpec(blk, lambda b, p, i: (rows(b, p, i), p)),
                  pl.BlockSpec(blk, lambda b, p, i: (rows_prev(b, p, i), kcol + p)),
                  pl.BlockSpec(blk, lambda b, p, i: (rows(b, p, i), kcol + p)),
                  pl.BlockSpec(blk, lambda b, p, i: (rows_prev(b, p, i), vcol + p)),
                  pl.BlockSpec(blk, lambda b, p, i: (rows(b, p, i), vcol + p)),
                  pl.BlockSpec((1, 2 * CHUNK, reach + CHUNK), lambda b, p, i: (p, 0, 0)),
                  pl.BlockSpec((1, pw), lambda b, p, i: (0, 0)),
                  pl.BlockSpec((1, pw), lambda b, p, i: (0, 0))],
        out_specs=[pl.BlockSpec(blk, lambda b, p, i: (rows(b, p, i), p)),
                   pl.BlockSpec(blk, lambda b, p, i: (b, p))],
        out_shape=[jax.ShapeDtypeStruct((m_tot, d), F32),
                   jax.ShapeDtypeStruct((n_b * tq, d), F32)],
        compiler_params=_cp("parallel", "parallel", "arbitrary"),
        name="attn_prompt",
    )(qkv, qkv, qkv, qkv, qkv, bias_p, gq2, gk2)

    w = cache_k.shape[1]
    bias_s = _rel_bias_pairs(rel, w + jnp.arange(t_s), jnp.arange(w + t_s))
    srow = m_p // t_s
    sblk = (t_s, pw)
    o_all, kn_s = pl.pallas_call(
        functools.partial(_attn_sample_kernel, hd=hd, scale=scale),
        grid=(n_s, n_pairs),
        in_specs=[pl.BlockSpec(sblk, lambda b, p: (srow + b, p)),
                  pl.BlockSpec(sblk, lambda b, p: (srow + b, kcol + p)),
                  pl.BlockSpec(sblk, lambda b, p: (srow + b, vcol + p)),
                  pl.BlockSpec((None, w, pw), lambda b, p: (b, 0, p)),
                  pl.BlockSpec((None, w, pw), lambda b, p: (b, 0, p)),
                  pl.BlockSpec((1, 2 * t_s, w + t_s), lambda b, p: (p, 0, 0)),
                  pl.BlockSpec((1, pw), lambda b, p: (0, 0)),
                  pl.BlockSpec((1, pw), lambda b, p: (0, 0)),
                  pl.BlockSpec(memory_space=pl.ANY)],
        out_specs=[pl.BlockSpec(sblk, lambda b, p: (srow + b, p)),
                   pl.BlockSpec(sblk, lambda b, p: (b, p))],
        out_shape=[jax.ShapeDtypeStruct((m_tot, d), F32),
                   jax.ShapeDtypeStruct((n_s * t_s, d), F32)],
        input_output_aliases={8: 0},
        compiler_params=_cp("parallel", "parallel"),
        name="attn_sample",
    )(qkv, qkv, qkv, cache_k, cache_v, bias_s, gq2, gk2, o_all)
    return o_all, kn_p, kn_s


def _gdn_act_kernel(xq_ref, xk_ref, xv_ref, pq_ref, pk_ref, pv_ref, hq_ref, hk_ref, hv_ref,
                    aux_ref, cw_ref, pvec_ref, o_ref, bg_ref, *, blocks_per_seq, n_heads, scale):
    i = pl.program_id(0)
    start = (i % blocks_per_seq) == 0
    d = xq_ref.shape[1]
    hd = d // n_heads
    parts = ((xq_ref, pq_ref, hq_ref), (xk_ref, pk_ref, hk_ref), (xv_ref, pv_ref, hv_ref))
    for c, (x_ref, p_ref, h_ref) in enumerate(parts):
        prev8 = jnp.where(start, h_ref[...], p_ref[...])
        a = _silu(_dwconv(x_ref[...], prev8, cw_ref[:, c * d:(c + 1) * d]))
        for hh in range(n_heads):
            ah = a[:, hh * hd:(hh + 1) * hd]
            if c < 2:
                ah = ah * lax.rsqrt(jnp.sum(ah * ah, axis=-1, keepdims=True) + EPS)
            if c == 0:
                ah = ah * scale
            o_ref[:, c * d + hh * hd:c * d + (hh + 1) * hd] = ah
    ax = aux_ref[...]
    lane = lax.broadcasted_iota(jnp.int32, ax.shape, 1)
    beta = _sigmoid(ax)
    gdec = -jnp.exp(pvec_ref[0:1, :]) * _softplus(ax + pvec_ref[1:2, :])
    bg_ref[...] = jnp.where(lane < n_heads, beta, jnp.where(lane < 2 * n_heads, gdec, 0.0))


def _gdn_act(proj, aux, row0, n_seq, t, tb, hist8, conv_w, pvec, n_heads):
    d = proj.shape[1] // 4
    hd = d // n_heads
    bps = t // tb
    r0 = row0 // tb
    p8 = tb // SUBLANES

    def xspec(c):
        return pl.BlockSpec((tb, d), lambda i, c=c: (r0 + i, c))

    def pspec(c):
        return pl.BlockSpec((SUBLANES, d), lambda i, c=c: (jnp.maximum((r0 + i) * p8 - 1, 0), c))

    def hspec(c):
        return pl.BlockSpec((None, SUBLANES, d), lambda i, c=c: (i // bps, 0, c))

    m = n_seq * t
    return pl.pallas_call(
        functools.partial(_gdn_act_kernel, blocks_per_seq=bps, n_heads=n_heads, scale=hd ** -0.5),
        grid=(m // tb,),
        in_specs=[xspec(0), xspec(1), xspec(2), pspec(0), pspec(1), pspec(2), hspec(0), hspec(1), hspec(2),
                  pl.BlockSpec((tb, LANES), lambda i: (r0 + i, 0)),
                  pl.BlockSpec((CONV_W, 3 * d), lambda i: (0, 0)),
                  pl.BlockSpec((2, LANES), lambda i: (0, 0))],
        out_specs=[pl.BlockSpec((tb, 3 * d), lambda i: (i, 0)),
                   pl.BlockSpec((tb, LANES), lambda i: (i, 0))],
        out_shape=[jax.ShapeDtypeStruct((m, 3 * d), F32), jax.ShapeDtypeStruct((m, LANES), F32)],
        compiler_params=_cp("parallel"),
        name="gdn_act",
    )(proj, proj, proj, proj, proj, proj, hist8, hist8, hist8, aux, conv_w, pvec)


def _seg_cumsum(x, rowmod, reverse):
    n = x.shape[0]
    d = 1
    while d < CHUNK:
        if reverse:
            sh = pltpu.roll(x, n - d, axis=0)
            x = x + jnp.where(rowmod < CHUNK - d, sh, 0.0)
        else:
            sh = pltpu.roll(x, d, axis=0)
            x = x + jnp.where(rowmod >= d, sh, 0.0)
        d *= 2
    return x


def _gdn_delta_kernel(q_ref, k_ref, v_ref, z_ref, bg_ref, s0_ref, go_ref, o_ref, sout_ref, s_ref, *, n_heads):
    h = pl.program_id(1)
    i = pl.program_id(2)
    tb, hd = q_ref.shape
    nch = tb // CHUNK

    @pl.when(i == 0)
    def _():
        s_ref[...] = s0_ref[...]

    q = q_ref[...]
    k = k_ref[...]
    v = v_ref[...]
    bg = bg_ref[...]
    lane = lax.broadcasted_iota(jnp.int32, bg.shape, 1)
    beta = jnp.broadcast_to(jnp.sum(jnp.where(lane == h, bg, 0.0), axis=-1, keepdims=True), (tb, hd))
    g = jnp.broadcast_to(jnp.sum(jnp.where(lane == n_heads + h, bg, 0.0), axis=-1, keepdims=True), (tb, hd))

    rowmod = lax.broadcasted_iota(jnp.int32, (tb, hd), 0) & (CHUNK - 1)
    gc = _seg_cumsum(g, rowmod, reverse=False)
    grev = _seg_cumsum(g, rowmod, reverse=True) - g
    ri = lax.broadcasted_iota(jnp.int32, (tb, tb), 0)
    ci = lax.broadcasted_iota(jnp.int32, (tb, tb), 1)
    same = (ri >> 6) == (ci >> 6)
    low_incl = same & (ri >= ci)
    strict = same & (ri > ci)

    gc_sq = jnp.broadcast_to(gc[:, 0:1], (tb, tb))
    dh, dm, dl = _split3(jnp.where(ri == ci, gc_sq, 0.0))
    ones8 = jnp.ones((SUBLANES, tb), BF16)
    d = functools.partial(jnp.dot, preferred_element_type=F32)
    gc_row = d(ones8, dh) + d(ones8, dm) + d(ones8, dl)
    gc_row = jnp.broadcast_to(gc_row[0:1, :], (tb, tb))
    decay = jnp.exp(jnp.where(low_incl, gc_sq - gc_row, -jnp.inf))

    kb = k * beta
    lmat = jnp.where(strict, _mm_nt(kb, k) * decay, 0.0)
    egc = jnp.exp(gc)
    rhs = jnp.concatenate([v * beta, kb * egc], axis=1)

    def compact(x):
        acc = x[0:CHUNK]
        for c in range(1, nch):
            acc = acc + x[c * CHUNK:(c + 1) * CHUNK]
        return acc

    def expand(xc):
        return jnp.where(same, jnp.concatenate([xc] * nch, axis=0), 0.0)

    xk_ = compact(lmat)
    tinv = compact(jnp.where(ri == ci, 1.0, 0.0)) - xk_
    p = 1
    while 2 * p < CHUNK:
        xk_ = _mm3(xk_, expand(xk_))
        tinv = tinv + _mm3(tinv, expand(xk_))
        p *= 2
    sol = _mm3(expand(tinv), rhs)
    u = sol[:, :hd]
    w = sol[:, hd:]

    qk = jnp.where(low_incl, _mm_nt(q, k) * decay, 0.0)
    q_dec = q * egc
    k_dec = k * jnp.exp(grev)
    g_last = jnp.exp(gc + grev)

    s = s_ref[...]
    v_news = []
    o_inter = []
    for c in range(nch):
        rc = slice(c * CHUNK, (c + 1) * CHUNK)
        v_new = u[rc] - _mm(w[rc], s)
        o_inter.append(_mm(q_dec[rc], s))
        s = s * g_last[c * CHUNK:c * CHUNK + 1, :] + _mm_tn(k_dec[rc], v_new)
        v_news.append(v_new)
    s_ref[...] = s

    @pl.when(i == pl.num_programs(2) - 1)
    def _():
        sout_ref[...] = s

    o = jnp.concatenate(o_inter, axis=0) + _mm(qk, jnp.concatenate(v_news, axis=0))
    o = _rms(o, go_ref[...])
    o_ref[...] = o * _silu(z_ref[...])


def _gdn_delta(act, bg, zsrc, zcol0, zrow0, n_seq, t, tb, s0, g_o, n_heads):
    d = act.shape[1] // 3
    hd = d // n_heads
    nt = t // tb
    zr0 = zrow0 // tb

    def rows(b, h, i):
        return b * nt + i

    blk = (tb, hd)
    return pl.pallas_call(
        functools.partial(_gdn_delta_kernel, n_heads=n_heads),
        grid=(n_seq, n_heads, nt),
        in_specs=[pl.BlockSpec(blk, lambda b, h, i: (rows(b, h, i), h)),
                  pl.BlockSpec(blk, lambda b, h, i: (rows(b, h, i), n_heads + h)),
                  pl.BlockSpec(blk, lambda b, h, i: (rows(b, h, i), 2 * n_heads + h)),
                  pl.BlockSpec(blk, lambda b, h, i: (zr0 + rows(b, h, i), zcol0 + h)),
                  pl.BlockSpec((tb, LANES), lambda b, h, i: (rows(b, h, i), 0)),
                  pl.BlockSpec((None, None, hd, hd), lambda b, h, i: (b, h, 0, 0)),
                  pl.BlockSpec((1, hd), lambda b, h, i: (0, 0))],
        out_specs=[pl.BlockSpec(blk, lambda b, h, i: (rows(b, h, i), h)),
                   pl.BlockSpec((None, None, hd, hd), lambda b, h, i: (b, h, 0, 0))],
        out_shape=[jax.ShapeDtypeStruct((n_seq * t, d), F32),
                   jax.ShapeDtypeStruct((n_seq, n_heads, hd, hd), F32)],
        scratch_shapes=[pltpu.VMEM((hd, hd), F32)],
        compiler_params=_cp("parallel", "parallel", "arbitrary"),
        name="gdn_delta",
    )(act, act, act, zsrc, bg, s0, g_o.reshape(1, hd))


def _rglru_kernel(h_ref, g_ref, win_ref, cw_ref, cb_ref, wgate_ref, bra_ref, bri_ref, lam_ref, wout_ref,
                  hist_ref, h0_ref, oin_ref, o_ref, hlast_ref, xtail_ref, cx_ref, ch_ref, *, n_blocks):
    del oin_ref
    i = pl.program_id(1)
    tb, d = h_ref.shape
    cw = lam_ref.shape[1]
    blk = cw // n_blocks

    @pl.when(i == 0)
    def _():
        cx_ref[...] = hist_ref[...]
        ch_ref[...] = h0_ref[...]

    h = h_ref[...]
    proj = jnp.dot(_rms(h, g_ref[...]).astype(BF16), win_ref[...], preferred_element_type=F32)
    y_br = proj[:, :cw]
    x_br = proj[:, cw:]
    xc = _dwconv(x_br, cx_ref[...], cw_ref[...]) + cb_ref[...]
    r_parts, i_parts = [], []
    for n in range(n_blocks):
        ri = jnp.dot(xc[:, n * blk:(n + 1) * blk].astype(BF16), wgate_ref[n], preferred_element_type=F32)
        r_parts.append(ri[:, :blk])
        i_parts.append(ri[:, blk:])
    r = _sigmoid(jnp.concatenate(r_parts, axis=1) + bra_ref[...])
    ig = _sigmoid(jnp.concatenate(i_parts, axis=1) + bri_ref[...])
    log_a = -LRU_C * r * _softplus(-lam_ref[...])
    a = jnp.exp(log_a)
    th = jnp.tanh(log_a)
    b = jnp.sqrt(-2.0 * th / (1.0 - th)) * (ig * xc)
    row = lax.broadcasted_iota(jnp.int32, (tb, cw), 0)
    h_prev = ch_ref[SUBLANES - 1:SUBLANES, :]
    b = b + jnp.where(row == 0, a * h_prev, 0.0)
    step = 1
    while step < tb:
        a_s = jnp.where(row >= step, pltpu.roll(a, step, axis=0), 1.0)
        b_s = jnp.where(row >= step, pltpu.roll(b, step, axis=0), 0.0)
        b = a * b_s + b
        a = a * a_s
        step *= 2
    hs = b
    cx_ref[...] = x_br[tb - SUBLANES:, :]
    ch_ref[...] = hs[tb - SUBLANES:, :]

    @pl.when(i == pl.num_programs(1) - 1)
    def _():
        hlast_ref[...] = hs[tb - SUBLANES:, :]
        xtail_ref[...] = x_br[tb - SUBLANES:, :]

    o_ref[...] = h + jnp.dot((hs * _gelu(y_br)).astype(BF16), wout_ref[...], preferred_element_type=F32)


def _rglru(h, h_out, row0, n_seq, t, tb, g, w_in, conv_w, conv_b, wgate, b_ra, b_ri, lam, w_out, hist8, h08):
    m, d = h.shape
    cw = lam.shape[0]
    nb = wgate.shape[0]
    nt = t // tb
    r0 = row0 // tb
    vec = lambda x: x.reshape(1, -1)
    cst = lambda shape: pl.BlockSpec(shape, lambda b, i: (0,) * len(shape))
    in_specs = [pl.BlockSpec((tb, d), lambda b, i: (r0 + b * nt + i, 0)),
                cst((1, d)), cst((d, 2 * cw)), cst((CONV_W, cw)), cst((1, cw)),
                cst(wgate.shape), cst((1, cw)), cst((1, cw)), cst((1, cw)), cst((cw, d)),
                pl.BlockSpec((None, SUBLANES, cw), lambda b, i: (b, 0, 0)),
                pl.BlockSpec((None, SUBLANES, cw), lambda b, i: (b, 0, 0))]
    args = [h, vec(g), w_in, conv_w, vec(conv_b), wgate, vec(b_ra), vec(b_ri), vec(lam), w_out, hist8, h08]
    aliases = {}
    if h_out is not None:
        in_specs.append(pl.BlockSpec(memory_space=pl.ANY))
        args.append(h_out)
        aliases = {len(args) - 1: 0}
    kern = functools.partial(_rglru_kernel, n_blocks=nb)
    if h_out is None:
        kern = lambda *refs: _rglru_kernel(*refs[:12], None, *refs[12:], n_blocks=nb)
    return pl.pallas_call(
        kern,
        grid=(n_seq, nt),
        in_specs=in_specs,
        out_specs=[pl.BlockSpec((tb, d), lambda b, i: (r0 + b * nt + i, 0)),
                   pl.BlockSpec((None, SUBLANES, cw), lambda b, i: (b, 0, 0)),
                   pl.BlockSpec((None, SUBLANES, cw), lambda b, i: (b, 0, 0))],
        out_shape=[jax.ShapeDtypeStruct((m, d), F32),
                   jax.ShapeDtypeStruct((n_seq, SUBLANES, cw), F32),
                   jax.ShapeDtypeStruct((n_seq, SUBLANES, cw), F32)],
        scratch_shapes=[pltpu.VMEM((SUBLANES, cw), F32), pltpu.VMEM((SUBLANES, cw), F32)],
        input_output_aliases=aliases,
        compiler_params=_cp("parallel", "arbitrary"),
        name="rglru",
    )(*args)


def _gmlp_kernel(h_ref, g_ref, win_ref, bin_ref, lng_ref, lnb_ref, ws_ref, bs_ref, wout_ref, o_ref, vn_ref,
                 *, n_groups, first_sample_tile, lc_prompt, lc_sample):
    i = pl.program_id(0)
    tm, d = h_ref.shape
    half = lng_ref.shape[1]
    gw = half // n_groups
    h = h_ref[...]
    act = _gelu(jnp.dot(_rms(h, g_ref[...]).astype(BF16), win_ref[...], preferred_element_type=F32) + bin_ref[...])
    u = act[:, :half]
    v = act[:, half:]
    vc = v - jnp.mean(v, axis=-1, keepdims=True)
    vn = vc * lax.rsqrt(jnp.mean(vc * vc, axis=-1, keepdims=True) + EPS) * lng_ref[...] + lnb_ref[...]
    vn_ref[...] = vn
    shift = jnp.where(i >= first_sample_tile, lc_sample.bit_length() - 1, lc_prompt.bit_length() - 1)
    ri = lax.broadcasted_iota(jnp.int32, (tm, tm), 0)
    ci = lax.broadcasted_iota(jnp.int32, (tm, tm), 1)
    keep = (lax.shift_right_logical(ri, shift) == lax.shift_right_logical(ci, shift)) & (ri >= ci)
    vnb = vn.astype(BF16)
    parts = []
    for gi in range(n_groups):
        ws = jnp.where(keep, ws_ref[0, gi], 0.0).astype(BF16)
        sg = jnp.dot(ws, vnb[:, gi * gw:(gi + 1) * gw], preferred_element_type=F32)
        parts.append(u[:, gi * gw:(gi + 1) * gw] * (sg + bs_ref[0, :, gi * gw:(gi + 1) * gw]))
    gated = jnp.concatenate(parts, axis=1).astype(BF16)
    o_ref[...] = h + jnp.dot(gated, wout_ref[...], preferred_element_type=F32)


def _gmlp(h, m_p, t_s, g, w_in, b_in, ln_g, ln_b, w_s, b_s, w_out):
    m, d = h.shape
    half = ln_g.shape[0]
    ng = w_s.shape[0]
    gw = half // ng
    tm = 256
    lc_p, lc_s = D_CHUNK, min(t_s, D_CHUNK)
    assert m_p % tm == 0 and (m - m_p) % tm == 0 and tm % lc_p == 0 and tm % lc_s == 0

    def tiled(lc):
        wt = jnp.tile(w_s[:, :lc, :lc], (1, tm // lc, tm // lc))
        bt = jnp.repeat(jnp.tile(b_s[:, :lc].T, (tm // lc, 1)), gw, axis=1)
        return wt, bt

    wt_p, bt_p = tiled(lc_p)
    wt_s, bt_s = tiled(lc_s)
    ws_all = jnp.stack([wt_p, wt_s])
    bs_all = jnp.stack([bt_p, bt_s])
    fst = m_p // tm
    vec = lambda x: x.reshape(1, -1)
    cst = lambda shape: pl.BlockSpec(shape, lambda i: (0,) * len(shape))
    return pl.pallas_call(
        functools.partial(_gmlp_kernel, n_groups=ng, first_sample_tile=fst, lc_prompt=lc_p, lc_sample=lc_s),
        grid=(m // tm,),
        in_specs=[pl.BlockSpec((tm, d), lambda i: (i, 0)),
                  cst((1, d)), cst((d, 2 * half)), cst((1, 2 * half)), cst((1, half)), cst((1, half)),
                  pl.BlockSpec((1, ng, tm, tm), lambda i: (i // fst, 0, 0, 0)),
                  pl.BlockSpec((1, tm, half), lambda i: (i // fst, 0, 0)),
                  cst((half, d))],
        out_specs=[pl.BlockSpec((tm, d), lambda i: (i, 0)),
                   pl.BlockSpec((tm, half), lambda i: (jnp.maximum(i - fst, 0), 0))],
        out_shape=[jax.ShapeDtypeStruct((m, d), F32),
                   jax.ShapeDtypeStruct((m - m_p, half), F32)],
        compiler_params=_cp("arbitrary"),
        name="gmlp",
    )(h, vec(g), w_in, vec(b_in), vec(ln_g), vec(ln_b), ws_all, bs_all, w_out)


def _pad_rows8(x3):
    n, r, c = x3.shape
    return jnp.concatenate([jnp.zeros((n, SUBLANES - r, c), x3.dtype), x3], axis=1)


def kernel(x_prompt, x_sample, p_prompt, p_sample, cache_k_a, cache_v_a, state_s_b, state_conv_b, state_h_c, state_conv_c, g_mix, g_ffn, g_ple, g_final, w_in_a, g_q_a, g_k_a, rel_a, w_out_a, w_in_b, conv_b, a_log_b, dt_bias_b, g_o_b, w_out_b, w_in_c, conv_c, conv_bias_c, w_ra_c, b_ra_c, w_ri_c, b_ri_c, lam_c, w_out_c, w_in_d, b_in_d, ln_g_d, ln_b_d, w_s_d, b_s_d, w_out_d, w_gate_ff, w_up_ff, w_down_ff, w_router, w_gate_e, w_up_e, w_down_e, w_ple, w_ple_gate):
    n_b, t, d = x_prompt.shape
    n_s, t_s, _ = x_sample.shape
    depth = p_prompt.shape[0]
    pd = p_prompt.shape[-1]
    m_p, m_s = n_b * t, n_s * t_s
    m = m_p + m_s
    a_heads = d // g_q_a.shape[-1]
    b_heads = a_log_b.shape[-1]
    n_exp = w_router.shape[-1]
    bf = lambda x: x.astype(BF16)

    h = jnp.concatenate([x_prompt.reshape(m_p, d), x_sample.reshape(m_s, d)], axis=0)
    p_all = jnp.concatenate([p_prompt.reshape(depth, m_p, pd), p_sample.reshape(depth, m_s, pd)], axis=1)

    outs = {name: [] for name in ("k_a_p", "v_a_p", "k_a_s", "v_a_s", "s_b_p", "conv_b_p", "s_b_s", "conv_b_s",
                                  "h_c_p", "conv_c_p", "h_c_s", "conv_c_s", "v_d_s")}

    def tail_rows(x2, n_seq, tt, row0, cols):
        x3 = x2[row0:row0 + n_seq * tt, :cols].reshape(n_seq, tt, cols)
        return x3[:, tt - (CONV_W - 1):, :]

    y = None
    for li in range(depth):
        kind, c = li % 4, li // 4
        if kind == 0:
            qkv = _proj(h, g_mix[li], bf(w_in_a[c]), 1024)
            w_keep = cache_k_a.shape[2]
            o, kn_p, kn_s = _attention(qkv, n_b, t, n_s, t_s,
                                       cache_k_a[c].reshape(n_s, w_keep, d), cache_v_a[c].reshape(n_s, w_keep, d),
                                       rel_a[c], g_q_a[c], g_k_a[c], a_heads)
            hd = d // a_heads
            keep = min(A_PREV_CHUNKS * CHUNK, t)
            v_p = qkv[:m_p, 2 * d:].reshape(n_b, t, d)[:, t - keep:]
            outs["k_a_p"].append(kn_p.reshape(n_b, keep, a_heads, hd))
            outs["v_a_p"].append(v_p.reshape(n_b, keep, a_heads, hd))
            outs["k_a_s"].append(kn_s.reshape(n_s, t_s, a_heads, hd))
            outs["v_a_s"].append(qkv[m_p:, 2 * d:].reshape(n_s, t_s, a_heads, hd))
            h = _outproj_res(h, o, bf(w_out_a[c]))
        elif kind == 1:
            w_in = w_in_b[c]
            nq = 3 * d
            w_main = bf(w_in[:, :nq + d])
            w_aux = bf(jnp.pad(w_in[:, nq + d:], ((0, 0), (0, LANES - 2 * b_heads))))
            proj, aux = _proj(h, g_mix[li], w_main, 1024, w_aux)
            pvec = jnp.zeros((2, LANES), F32)
            pvec = pvec.at[0, b_heads:2 * b_heads].set(a_log_b[c]).at[1, b_heads:2 * b_heads].set(dt_bias_b[c])
            hd = d // b_heads
            act_p, bg_p = _gdn_act(proj, aux, 0, n_b, t, 256, jnp.zeros((n_b, SUBLANES, nq), F32),
                                   conv_b[c], pvec, b_heads)
            o_p, s_p = _gdn_delta(act_p, bg_p, proj, 3 * b_heads, 0, n_b, t, 256,
                                  jnp.zeros((n_b, b_heads, hd, hd), F32), g_o_b[c], b_heads)
            act_s, bg_s = _gdn_act(proj, aux, m_p, n_s, t_s, t_s, _pad_rows8(state_conv_b[c]),
                                   conv_b[c], pvec, b_heads)
            tp = -(-t_s // CHUNK) * CHUNK
            padt = lambda x2: jnp.pad(x2.reshape(n_s, t_s, -1), ((0, 0), (0, tp - t_s), (0, 0))).reshape(n_s * tp, -1)
            z_s = padt(proj[m_p:, nq:nq + d])
            o_s, s_s = _gdn_delta(padt(act_s), padt(bg_s), z_s, 0, 0, n_s, tp, tp, state_s_b[c], g_o_b[c], b_heads)
            o_s = o_s.reshape(n_s, tp, d)[:, :t_s].reshape(m_s, d)
            outs["s_b_p"].append(s_p)
            outs["s_b_s"].append(s_s)
            outs["conv_b_p"].append(tail_rows(proj, n_b, t, 0, nq))
            outs["conv_b_s"].append(tail_rows(proj, n_s, t_s, m_p, nq))
            h = _outproj_res(h, jnp.concatenate([o_p, o_s], axis=0), bf(w_out_b[c]))
        elif kind == 2:
            cw = lam_c.shape[-1]
            wgate = bf(jnp.concatenate([w_ra_c[c], w_ri_c[c]], axis=-1))
            common = (g_mix[li], bf(w_in_c[c]), conv_c[c], conv_bias_c[c], wgate, b_ra_c[c], b_ri_c[c],
                      lam_c[c], bf(w_out_c[c]))
            zeros8 = jnp.zeros((n_b, SUBLANES, cw), F32)
            h_new, hl_p, xt_p = _rglru(h, None, 0, n_b, t, 256, *common, zeros8, zeros8)
            h_new, hl_s, xt_s = _rglru(h, h_new, m_p, n_s, t_s, t_s, *common,
                                       _pad_rows8(state_conv_c[c]), _pad_rows8(state_h_c[c][:, None, :]))
            h = h_new
            outs["h_c_p"].append(hl_p[:, SUBLANES - 1])
            outs["h_c_s"].append(hl_s[:, SUBLANES - 1])
            outs["conv_c_p"].append(xt_p[:, SUBLANES - (CONV_W - 1):])
            outs["conv_c_s"].append(xt_s[:, SUBLANES - (CONV_W - 1):])
        else:
            h, vn_s = _gmlp(h, m_p, t_s, g_mix[li], bf(w_in_d[c]), b_in_d[c], ln_g_d[c], ln_b_d[c],
                            w_s_d[c], b_s_d[c], bf(w_out_d[c]))
            outs["v_d_s"].append(vn_s.reshape(n_s, t_s, -1))

        j = li // 2
        if li % 2 == 0:
            h = _ffn_dense(h, g_ffn[li], bf(w_gate_ff[j]), bf(w_up_ff[j]), bf(w_down_ff[j]))
        else:
            wr = jnp.pad(w_router[j], ((0, 0), (0, LANES - n_exp)))
            h = _moe(h, g_ffn[li], wr, bf(w_gate_e[j]), bf(w_up_e[j]), bf(w_down_e[j]))
        last = li == depth - 1
        res = _ple(h, p_all[li], g_ple[li], bf(w_ple[li]), bf(w_ple_gate[li]), g_final if last else None)
        if last:
            y = res
        else:
            h = res

    st = {name: jnp.stack(rows) for name, rows in outs.items()}
    return (y[:m_p].reshape(n_b, t, d), y[m_p:].reshape(n_s, t_s, d),
            st["k_a_p"], st["v_a_p"], st["k_a_s"], st["v_a_s"],
            st["s_b_p"], st["conv_b_p"], st["s_b_s"], st["conv_b_s"],
            st["h_c_p"], st["conv_c_p"], st["h_c_s"], st["conv_c_s"],
            st["v_d_s"])
```

```python
import functools

import jax
import jax.numpy as jnp
from jax import lax
from jax.experimental import pallas as pl
from jax.experimental.pallas import tpu as pltpu

F32 = jnp.float32
BF16 = jnp.bfloat16
EPS = 1e-6
CHUNK = 64
CONV_W = 4
A_REL_CLIP = 128
A_PREV_CHUNKS = 8
LRU_C = 8.0
D_CHUNK = 128
TOP_K = 2
LANES = 128
SUBLANES = 8
VMEM_LIMIT = 48 * 1024 * 1024


def _cp(*sem, vmem=VMEM_LIMIT):
    return pltpu.CompilerParams(dimension_semantics=sem, vmem_limit_bytes=vmem)


def _tile_m(m):
    for t in (1024, 512, 256):
        if m % t == 0:
            return t
    raise ValueError(f"token count {m} must be a multiple of 256")


def _mm(a, b):
    return jnp.dot(a.astype(BF16), b.astype(BF16), preferred_element_type=F32)


def _mm_nt(a, b):
    return lax.dot_general(a.astype(BF16), b.astype(BF16), (((1,), (1,)), ((), ())),
                           preferred_element_type=F32)


def _mm_tn(a, b):
    return lax.dot_general(a.astype(BF16), b.astype(BF16), (((0,), (0,)), ((), ())),
                           preferred_element_type=F32)


def _split2(x):
    hi = x.astype(BF16)
    lo = (x - hi.astype(F32)).astype(BF16)
    return hi, lo


def _split3(x):
    hi = x.astype(BF16)
    r = x - hi.astype(F32)
    mid = r.astype(BF16)
    lo = (r - mid.astype(F32)).astype(BF16)
    return hi, mid, lo


def _mm3(a, b):
    ah, al = _split2(a)
    bh, bl = _split2(b)
    d = functools.partial(jnp.dot, preferred_element_type=F32)
    return d(ah, bh) + (d(ah, bl) + d(al, bh))


def _rms(x, g):
    return x * lax.rsqrt(jnp.mean(x * x, axis=-1, keepdims=True) + EPS) * g


def _sigmoid(x):
    return jax.nn.sigmoid(x)


def _silu(x):
    return x * _sigmoid(x)


def _softplus(x):
    return jnp.maximum(x, 0.0) + jnp.log1p(jnp.exp(-jnp.abs(x)))


def _gelu(x):
    c = 0.7978845608028654
    return x * (0.5 * (1.0 + jnp.tanh(c * (x + 0.044715 * (x * x * x)))))


def _shift_rows(x, prev8, s):
    r = pltpu.roll(x, s, axis=0)
    pr = pltpu.roll(prev8, s, axis=0)
    row8 = lax.broadcasted_iota(jnp.int32, pr.shape, 0)
    top = jnp.where(row8 < s, pr, r[0:SUBLANES])
    return jnp.concatenate([top, r[SUBLANES:]], axis=0)


def _dwconv(x, prev8, w):
    y = None
    for k in range(CONV_W):
        s = CONV_W - 1 - k
        xs = x if s == 0 else _shift_rows(x, prev8, s)
        term = xs * w[k:k + 1]
        y = term if y is None else y + term
    return y


def _proj_kernel(*refs, has_aux):
    if has_aux:
        x_ref, g_ref, w_ref, waux_ref, o_ref, oaux_ref, hn_ref = refs
    else:
        x_ref, g_ref, w_ref, o_ref, hn_ref = refs

    @pl.when(pl.program_id(1) == 0)
    def _():
        hn = _rms(x_ref[...], g_ref[...]).astype(BF16)
        hn_ref[...] = hn
        if has_aux:
            oaux_ref[...] = jnp.dot(hn, waux_ref[...], preferred_element_type=F32)

    o_ref[...] = jnp.dot(hn_ref[...], w_ref[...], preferred_element_type=F32)


def _proj(h, g, w, tn, waux=None):
    m, k = h.shape
    n = w.shape[1]
    tm = _tile_m(m)
    in_specs = [pl.BlockSpec((tm, k), lambda i, j: (i, 0)),
                pl.BlockSpec((1, k), lambda i, j: (0, 0)),
                pl.BlockSpec((k, tn), lambda i, j: (0, j))]
    out_shape = [jax.ShapeDtypeStruct((m, n), F32)]
    out_specs = [pl.BlockSpec((tm, tn), lambda i, j: (i, j))]
    args = [h, g.reshape(1, k), w]
    if waux is not None:
        na = waux.shape[1]
        in_specs.append(pl.BlockSpec((k, na), lambda i, j: (0, 0)))
        out_shape.append(jax.ShapeDtypeStruct((m, na), F32))
        out_specs.append(pl.BlockSpec((tm, na), lambda i, j: (i, 0)))
        args.append(waux)
    res = pl.pallas_call(
        functools.partial(_proj_kernel, has_aux=waux is not None),
        grid=(m // tm, n // tn),
        in_specs=in_specs, out_specs=out_specs, out_shape=out_shape,
        scratch_shapes=[pltpu.VMEM((tm, k), BF16)],
        compiler_params=_cp("parallel", "arbitrary"),
        name="proj",
    )(*args)
    return res if waux is not None else res[0]


def _outproj_kernel(h_ref, o_ref, w_ref, out_ref):
    out_ref[...] = h_ref[...] + jnp.dot(o_ref[...].astype(BF16), w_ref[...], preferred_element_type=F32)


def _outproj_res(h, o, w):
    m, d = h.shape
    k = o.shape[1]
    tm = _tile_m(m)
    return pl.pallas_call(
        _outproj_kernel,
        grid=(m // tm,),
        in_specs=[pl.BlockSpec((tm, d), lambda i: (i, 0)),
                  pl.BlockSpec((tm, k), lambda i: (i, 0)),
                  pl.BlockSpec((k, d), lambda i: (0, 0))],
        out_specs=pl.BlockSpec((tm, d), lambda i: (i, 0)),
        out_shape=jax.ShapeDtypeStruct((m, d), F32),
        compiler_params=_cp("parallel"),
        name="outproj_res",
    )(h, o, w)


def _ffn_kernel(x_ref, g_ref, wg_ref, wu_ref, wd_ref, o_ref, hn_ref):
    @pl.when(pl.program_id(1) == 0)
    def _():
        x = x_ref[...]
        hn_ref[...] = _rms(x, g_ref[...]).astype(BF16)
        o_ref[...] = x

    hn = hn_ref[...]
    gg = jnp.dot(hn, wg_ref[...], preferred_element_type=F32)
    uu = jnp.dot(hn, wu_ref[...], preferred_element_type=F32)
    hid = (_silu(gg) * uu).astype(BF16)
    o_ref[...] += jnp.dot(hid, wd_ref[...], preferred_element_type=F32)


def _ffn_dense(h, g, wg, wu, wd):
    m, d = h.shape
    f = wg.shape[1]
    tm = _tile_m(m)
    tf = 256
    return pl.pallas_call(
        _ffn_kernel,
        grid=(m // tm, f // tf),
        in_specs=[pl.BlockSpec((tm, d), lambda i, j: (i, 0)),
                  pl.BlockSpec((1, d), lambda i, j: (0, 0)),
                  pl.BlockSpec((d, tf), lambda i, j: (0, j)),
                  pl.BlockSpec((d, tf), lambda i, j: (0, j)),
                  pl.BlockSpec((tf, d), lambda i, j: (j, 0))],
        out_specs=pl.BlockSpec((tm, d), lambda i, j: (i, 0)),
        out_shape=jax.ShapeDtypeStruct((m, d), F32),
        scratch_shapes=[pltpu.VMEM((tm, d), BF16)],
        compiler_params=_cp("parallel", "arbitrary"),
        name="ffn_dense",
    )(h, g.reshape(1, d), wg, wu, wd)


def _moe_kernel(x_ref, g_ref, wr_ref, wg_ref, wu_ref, wd_ref, o_ref, hn_ref, gates_ref, *, n_experts):
    e = pl.program_id(1)
    f = pl.program_id(2)

    @pl.when((e == 0) & (f == 0))
    def _():
        x = x_ref[...]
        hn = _rms(x, g_ref[...])
        hn_ref[...] = hn.astype(BF16)
        o_ref[...] = x
        logits = _mm3(hn, wr_ref[...])
        lane = lax.broadcasted_iota(jnp.int32, logits.shape, 1)
        lg = jnp.where(lane < n_experts, logits, -jnp.inf)
        m1 = jnp.max(lg, axis=-1, keepdims=True)
        i1 = jnp.min(jnp.where(lg == m1, lane, LANES), axis=-1, keepdims=True)
        lg2 = jnp.where(lane == i1, -jnp.inf, lg)
        m2 = jnp.max(lg2, axis=-1, keepdims=True)
        i2 = jnp.min(jnp.where(lg2 == m2, lane, LANES), axis=-1, keepdims=True)
        e2 = jnp.exp(m2 - m1)
        w1 = 1.0 / (1.0 + e2)
        w2 = e2 / (1.0 + e2)
        gates_ref[...] = jnp.where(lane == i1, w1, 0.0) + jnp.where(lane == i2, w2, 0.0)

    hn = hn_ref[...]
    gates = gates_ref[...]
    lane = lax.broadcasted_iota(jnp.int32, gates.shape, 1)
    gate_e = jnp.sum(jnp.where(lane == e, gates, 0.0), axis=-1, keepdims=True)
    gg = jnp.dot(hn, wg_ref[0], preferred_element_type=F32)
    uu = jnp.dot(hn, wu_ref[0], preferred_element_type=F32)
    hid = (_silu(gg) * uu * gate_e).astype(BF16)
    o_ref[...] += jnp.dot(hid, wd_ref[0], preferred_element_type=F32)


def _moe(h, g, wr, wg, wu, wd):
    m, d = h.shape
    ne, _, ef = wg.shape
    tm = _tile_m(m)
    tf = 512
    return pl.pallas_call(
        functools.partial(_moe_kernel, n_experts=ne),
        grid=(m // tm, ne, ef // tf),
        in_specs=[pl.BlockSpec((tm, d), lambda i, e, j: (i, 0)),
                  pl.BlockSpec((1, d), lambda i, e, j: (0, 0)),
                  pl.BlockSpec((d, LANES), lambda i, e, j: (0, 0)),
                  pl.BlockSpec((1, d, tf), lambda i, e, j: (e, 0, j)),
                  pl.BlockSpec((1, d, tf), lambda i, e, j: (e, 0, j)),
                  pl.BlockSpec((1, tf, d), lambda i, e, j: (e, j, 0))],
        out_specs=pl.BlockSpec((tm, d), lambda i, e, j: (i, 0)),
        out_shape=jax.ShapeDtypeStruct((m, d), F32),
        scratch_shapes=[pltpu.VMEM((tm, d), BF16), pltpu.VMEM((tm, LANES), F32)],
        compiler_params=_cp("parallel", "arbitrary", "arbitrary"),
        name="moe",
    )(h, g.reshape(1, d), wr, wg, wu, wd)


def _ple_kernel(*refs, final):
    if final:
        x_ref, p_ref, g_ref, wp_ref, wg_ref, gf_ref, o_ref = refs
    else:
        x_ref, p_ref, g_ref, wp_ref, wg_ref, o_ref = refs
    x = x_ref[...]
    hn = _rms(x, g_ref[...]).astype(BF16)
    gate = _sigmoid(jnp.dot(hn, wg_ref[...], preferred_element_type=F32))
    pp = jnp.dot(p_ref[...].astype(BF16), wp_ref[...], preferred_element_type=F32)
    out = x + pp * gate
    o_ref[...] = _rms(out, gf_ref[...]) if final else out


def _ple(h, p, g, wp, wgate, g_final=None):
    m, d = h.shape
    pd = p.shape[1]
    tm = _tile_m(m)
    final = g_final is not None
    in_specs = [pl.BlockSpec((tm, d), lambda i: (i, 0)),
                pl.BlockSpec((tm, pd), lambda i: (i, 0)),
                pl.BlockSpec((1, d), lambda i: (0, 0)),
                pl.BlockSpec((pd, d), lambda i: (0, 0)),
                pl.BlockSpec((d, d), lambda i: (0, 0))]
    args = [h, p, g.reshape(1, d), wp, wgate]
    if final:
        in_specs.append(pl.BlockSpec((1, d), lambda i: (0, 0)))
        args.append(g_final.reshape(1, d))
    return pl.pallas_call(
        functools.partial(_ple_kernel, final=final),
        grid=(m // tm,),
        in_specs=in_specs,
        out_specs=pl.BlockSpec((tm, d), lambda i: (i, 0)),
        out_shape=jax.ShapeDtypeStruct((m, d), F32),
        compiler_params=_cp("parallel"),
        name="ple",
    )(*args)


def _pair_rms(x, g, hd):
    lane = lax.broadcasted_iota(jnp.int32, x.shape, 1)
    lo = lane < hd
    x2 = x * x
    s0 = jnp.sum(jnp.where(lo, x2, 0.0), axis=-1, keepdims=True)
    s1 = jnp.sum(jnp.where(lo, 0.0, x2), axis=-1, keepdims=True)
    ms = jnp.where(lo, s0, s1) * (1.0 / hd)
    return x * lax.rsqrt(ms + EPS) * g


def _stack_heads(q, hd):
    lane = lax.broadcasted_iota(jnp.int32, q.shape, 1)
    lo = lane < hd
    return jnp.concatenate([jnp.where(lo, q, 0.0), jnp.where(lo, 0.0, q)], axis=0).astype(BF16)


def _unstack_heads(o2, hd):
    n = o2.shape[0] // 2
    lane = lax.broadcasted_iota(jnp.int32, (n, o2.shape[1]), 1)
    return jnp.where(lane < hd, o2[:n], o2[n:])


def _attend(q2, kw, vw, bias, valid):
    s = _mm_nt(q2, kw) + bias
    if valid is not None:
        s = jnp.where(valid, s, -jnp.inf)
    mx = jnp.max(s, axis=-1, keepdims=True)
    e = jnp.exp(s - mx)
    den = jnp.sum(e, axis=-1, keepdims=True)
    return jnp.dot(e.astype(BF16), vw, preferred_element_type=F32) / den


def _attn_prompt_kernel(q_ref, k_ref, v_ref, b_ref, gq_ref, gk_ref, o_ref, kn_ref, kwin_ref, vwin_ref,
                        *, hd, scale, reach, chunk_group):
    i = pl.program_id(2)
    tq = q_ref.shape[0]
    band = reach + CHUNK
    qn = _pair_rms(q_ref[...], gq_ref[...], hd) * scale
    knc = _pair_rms(k_ref[...], gk_ref[...], hd)

    @pl.when(i == pl.num_programs(2) - 1)
    def _():
        kn_ref[...] = knc

    @pl.when(i == 0)
    def _():
        kwin_ref[0:tq, :] = jnp.zeros((tq, 2 * hd), BF16)
        vwin_ref[0:tq, :] = jnp.zeros((tq, 2 * hd), BF16)

    @pl.when(i > 0)
    def _():
        kwin_ref[0:tq, :] = kwin_ref[tq:2 * tq, :]
        vwin_ref[0:tq, :] = vwin_ref[tq:2 * tq, :]

    kwin_ref[tq:2 * tq, :] = knc.astype(BF16)
    vwin_ref[tq:2 * tq, :] = v_ref[...].astype(BF16)
    bias = b_ref[0]
    col = lax.broadcasted_iota(jnp.int32, (1, band), 1)
    n_chunks = tq // CHUNK
    w0s = [tq - reach + jj * CHUNK for jj in range(n_chunks)]

    def attend(first_block, jjs):
        q2 = [_stack_heads(qn[jj * CHUNK:(jj + 1) * CHUNK], hd) for jj in jjs]
        s = [_mm_nt(x, kwin_ref[w0s[jj]:w0s[jj] + band, :]) + bias for x, jj in zip(q2, jjs)]
        if first_block:
            s = [jnp.where(col + w0s[jj] >= tq, x, -jnp.inf) for x, jj in zip(s, jjs)]
        mx = [jnp.max(x, axis=-1, keepdims=True) for x in s]
        e = [jnp.exp(x - m) for x, m in zip(s, mx)]
        den = [jnp.sum(x, axis=-1, keepdims=True) for x in e]
        o2 = [jnp.dot(x.astype(BF16), vwin_ref[w0s[jj]:w0s[jj] + band, :], preferred_element_type=F32)
              for x, jj in zip(e, jjs)]
        for x, dn, jj in zip(o2, den, jjs):
            o_ref[jj * CHUNK:(jj + 1) * CHUNK, :] = _unstack_heads(x / dn, hd)

    def attend_all(first_block):
        for j0 in range(0, n_chunks, chunk_group):
            attend(first_block, list(range(j0, j0 + chunk_group)))

    @pl.when(i == 0)
    def _():
        attend_all(True)

    @pl.when(i > 0)
    def _():
        attend_all(False)


def _attn_sample_kernel(q_ref, k_ref, v_ref, ck_ref, cv_ref, b_ref, gq_ref, gk_ref, oin_ref, o_ref, kn_ref,
                        *, hd, scale):
    del oin_ref
    qn = _pair_rms(q_ref[...], gq_ref[...], hd) * scale
    kn = _pair_rms(k_ref[...], gk_ref[...], hd)
    kn_ref[...] = kn
    kcat = jnp.concatenate([ck_ref[...], kn], axis=0).astype(BF16)
    vcat = jnp.concatenate([cv_ref[...], v_ref[...]], axis=0).astype(BF16)
    o2 = _attend(_stack_heads(qn, hd), kcat, vcat, b_ref[0], None)
    o_ref[...] = _unstack_heads(o2, hd)


def _rel_bias_pairs(table, q_pos, k_pos):
    d = jnp.clip(q_pos[:, None] - k_pos[None, :], -A_REL_CLIP, A_REL_CLIP) + A_REL_CLIP
    b = table[:, d]
    nh, tq, tk = b.shape
    return b.reshape(nh // 2, 2 * tq, tk)


def _attention(qkv, n_b, t, n_s, t_s, cache_k, cache_v, rel, g_q, g_k, n_heads):
    m_tot, d3 = qkv.shape
    d = d3 // 3
    hd = d // n_heads
    pw = 2 * hd
    n_pairs = d // pw
    m_p = n_b * t
    reach = A_PREV_CHUNKS * CHUNK
    tq = reach
    assert pw == LANES and t % tq == 0 and t_s % 16 == 0
    nqb = t // tq
    scale = hd ** -0.5
    gq2 = jnp.tile(g_q, 2).reshape(1, pw)
    gk2 = jnp.tile(g_k, 2).reshape(1, pw)

    bias_p = _rel_bias_pairs(rel, reach + jnp.arange(CHUNK), jnp.arange(reach + CHUNK))
    kcol, vcol = n_pairs, 2 * n_pairs

    def rows(b, p, i):
        return b * nqb + i

    blk = (tq, pw)
    o_all, kn_p = pl.pallas_call(
        functools.partial(_attn_prompt_kernel, hd=hd, scale=scale, reach=reach, chunk_group=4),
        grid=(n_b, n_pairs, nqb),
        in_specs=[pl.BlockSpec(blk, lambda b, p, i: (rows(b, p, i), p)),
                  pl.BlockSpec(blk, lambda b, p, i: (rows(b, p, i), kcol + p)),
                  pl.BlockSpec(blk, lambda b, p, i: (rows(b, p, i), vcol + p)),
                  pl.BlockSpec((1, 2 * CHUNK, reach + CHUNK), lambda b, p, i: (p, 0, 0)),
                  pl.BlockSpec((1, pw), lambda b, p, i: (0, 0)),
                  pl.BlockSpec((1, pw), lambda b, p, i: (0, 0))],
        out_specs=[pl.BlockSpec(blk, lambda b, p, i: (rows(b, p, i), p)),
                   pl.BlockSpec(blk, lambda b, p, i: (b, p))],
        out_shape=[jax.ShapeDtypeStruct((m_tot, d), F32),
                   jax.ShapeDtypeStruct((n_b * tq, d), F32)],
        scratch_shapes=[pltpu.VMEM((2 * tq, pw), BF16), pltpu.VMEM((2 * tq, pw), BF16)],
        compiler_params=_cp("parallel", "parallel", "arbitrary"),
        name="attn_prompt",
    )(qkv, qkv, qkv, bias_p, gq2, gk2)

    w = cache_k.shape[1]
    bias_s = _rel_bias_pairs(rel, w + jnp.arange(t_s), jnp.arange(w + t_s))
    srow = m_p // t_s
    sblk = (t_s, pw)
    o_all, kn_s = pl.pallas_call(
        functools.partial(_attn_sample_kernel, hd=hd, scale=scale),
        grid=(n_s, n_pairs),
        in_specs=[pl.BlockSpec(sblk, lambda b, p: (srow + b, p)),
                  pl.BlockSpec(sblk, lambda b, p: (srow + b, kcol + p)),
                  pl.BlockSpec(sblk, lambda b, p: (srow + b, vcol + p)),
                  pl.BlockSpec((None, w, pw), lambda b, p: (b, 0, p)),
                  pl.BlockSpec((None, w, pw), lambda b, p: (b, 0, p)),
                  pl.BlockSpec((1, 2 * t_s, w + t_s), lambda b, p: (p, 0, 0)),
                  pl.BlockSpec((1, pw), lambda b, p: (0, 0)),
                  pl.BlockSpec((1, pw), lambda b, p: (0, 0)),
                  pl.BlockSpec(memory_space=pl.ANY)],
        out_specs=[pl.BlockSpec(sblk, lambda b, p: (srow + b, p)),
                   pl.BlockSpec(sblk, lambda b, p: (b, p))],
        out_shape=[jax.ShapeDtypeStruct((m_tot, d), F32),
                   jax.ShapeDtypeStruct((n_s * t_s, d), F32)],
        input_output_aliases={8: 0},
        compiler_params=_cp("parallel", "parallel"),
        name="attn_sample",
    )(qkv, qkv, qkv, cache_k, cache_v, bias_s, gq2, gk2, o_all)
    return o_all, kn_p, kn_s


def _gdn_act_kernel(xq_ref, xk_ref, xv_ref, pq_ref, pk_ref, pv_ref, hq_ref, hk_ref, hv_ref,
                    aux_ref, cw_ref, pvec_ref, o_ref, bg_ref, *, blocks_per_seq, n_heads, scale):
    i = pl.program_id(0)
    start = (i % blocks_per_seq) == 0
    d = xq_ref.shape[1]
    hd = d // n_heads
    parts = ((xq_ref, pq_ref, hq_ref), (xk_ref, pk_ref, hk_ref), (xv_ref, pv_ref, hv_ref))
    for c, (x_ref, p_ref, h_ref) in enumerate(parts):
        prev8 = jnp.where(start, h_ref[...], p_ref[...])
        a = _silu(_dwconv(x_ref[...], prev8, cw_ref[:, c * d:(c + 1) * d]))
        for hh in range(n_heads):
            ah = a[:, hh * hd:(hh + 1) * hd]
            if c < 2:
                ah = ah * lax.rsqrt(jnp.sum(ah * ah, axis=-1, keepdims=True) + EPS)
            if c == 0:
                ah = ah * scale
            o_ref[:, c * d + hh * hd:c * d + (hh + 1) * hd] = ah
    ax = aux_ref[...]
    lane = lax.broadcasted_iota(jnp.int32, ax.shape, 1)
    beta = _sigmoid(ax)
    gdec = -jnp.exp(pvec_ref[0:1, :]) * _softplus(ax + pvec_ref[1:2, :])
    bg_ref[...] = jnp.where(lane < n_heads, beta, jnp.where(lane < 2 * n_heads, gdec, 0.0))


def _gdn_act(proj, aux, row0, n_seq, t, tb, hist8, conv_w, pvec, n_heads):
    d = proj.shape[1] // 4
    hd = d // n_heads
    bps = t // tb
    r0 = row0 // tb
    p8 = tb // SUBLANES

    def xspec(c):
        return pl.BlockSpec((tb, d), lambda i, c=c: (r0 + i, c))

    def pspec(c):
        return pl.BlockSpec((SUBLANES, d), lambda i, c=c: (jnp.maximum((r0 + i) * p8 - 1, 0), c))

    def hspec(c):
        return pl.BlockSpec((None, SUBLANES, d), lambda i, c=c: (i // bps, 0, c))

    m = n_seq * t
    return pl.pallas_call(
        functools.partial(_gdn_act_kernel, blocks_per_seq=bps, n_heads=n_heads, scale=hd ** -0.5),
        grid=(m // tb,),
        in_specs=[xspec(0), xspec(1), xspec(2), pspec(0), pspec(1), pspec(2), hspec(0), hspec(1), hspec(2),
                  pl.BlockSpec((tb, LANES), lambda i: (r0 + i, 0)),
                  pl.BlockSpec((CONV_W, 3 * d), lambda i: (0, 0)),
                  pl.BlockSpec((2, LANES), lambda i: (0, 0))],
        out_specs=[pl.BlockSpec((tb, 3 * d), lambda i: (i, 0)),
                   pl.BlockSpec((tb, LANES), lambda i: (i, 0))],
        out_shape=[jax.ShapeDtypeStruct((m, 3 * d), F32), jax.ShapeDtypeStruct((m, LANES), F32)],
        compiler_params=_cp("parallel"),
        name="gdn_act",
    )(proj, proj, proj, proj, proj, proj, hist8, hist8, hist8, aux, conv_w, pvec)


def _seg_cumsum(x, rowmod, reverse):
    n = x.shape[0]
    d = 1
    while d < CHUNK:
        if reverse:
            sh = pltpu.roll(x, n - d, axis=0)
            x = x + jnp.where(rowmod < CHUNK - d, sh, 0.0)
        else:
            sh = pltpu.roll(x, d, axis=0)
            x = x + jnp.where(rowmod >= d, sh, 0.0)
        d *= 2
    return x


def _gdn_delta_kernel(act_ref, z_ref, bg_ref, s0_ref, go_ref, o_ref, sout_ref, s_ref, *, n_heads, chained, head_group):
    i = pl.program_id(1)
    tb, d = z_ref.shape
    hd = d // n_heads
    nch = tb // CHUNK
    dot = functools.partial(jnp.dot, preferred_element_type=F32)

    if chained:
        @pl.when(i == 0)
        def _():
            s_ref[...] = s0_ref[...]

    bg = bg_ref[...]
    lane = lax.broadcasted_iota(jnp.int32, (tb, hd), 1)
    rowmod = lax.broadcasted_iota(jnp.int32, (tb, hd), 0) & (CHUNK - 1)
    ri = lax.broadcasted_iota(jnp.int32, (tb, tb), 0)
    ci = lax.broadcasted_iota(jnp.int32, (tb, tb), 1)
    same = (ri >> 6) == (ci >> 6)
    low_incl = same & (ri >= ci)
    strict = same & (ri > ci)
    eye_c = (lax.broadcasted_iota(jnp.int32, (CHUNK, tb), 0)
             == (lax.broadcasted_iota(jnp.int32, (CHUNK, tb), 1) & (CHUNK - 1))).astype(F32)
    def compact(x):
        acc = x[0:CHUNK]
        for c in range(1, nch):
            acc = acc + x[c * CHUNK:(c + 1) * CHUNK]
        return acc

    def expand(xc):
        return jnp.where(same, jnp.concatenate([xc] * nch, axis=0), jnp.zeros((), xc.dtype))

    def split_expand(xc):
        hi, lo = _split2(xc)
        return hi, lo, expand(hi), expand(lo)

    def each(fn, *lists):
        return [fn(*a) for a in zip(*lists)]

    def gc_diff(gc):
        gh, gm, gl = (piece.astype(F32) for piece in _split3(gc))
        lhs = jnp.where(lane == 0, gh, jnp.where(lane == 1, gm, jnp.where(lane == 2, gl,
                                                                          jnp.where(lane < 6, 1.0, 0.0))))
        rhs_t = jnp.where(lane < 3, 1.0, jnp.where(lane == 3, -gh, jnp.where(lane == 4, -gm,
                                                                             jnp.where(lane == 5, -gl, 0.0))))
        return _mm_nt(lhs, rhs_t)

    for h0 in range(0, n_heads, head_group):
        hs = list(range(h0, h0 + head_group))
        q = [act_ref[:, h * hd:(h + 1) * hd] for h in hs]
        k = [act_ref[:, d + h * hd:d + (h + 1) * hd] for h in hs]
        v = [act_ref[:, 2 * d + h * hd:2 * d + (h + 1) * hd] for h in hs]
        beta = [jnp.broadcast_to(jnp.sum(jnp.where(lane == h, bg, 0.0), axis=-1, keepdims=True), (tb, hd))
                for h in hs]
        g = [jnp.broadcast_to(jnp.sum(jnp.where(lane == n_heads + h, bg, 0.0), axis=-1, keepdims=True), (tb, hd))
             for h in hs]
        gc = each(lambda x: _seg_cumsum(x, rowmod, reverse=False), g)
        grev = each(lambda x: _seg_cumsum(x, rowmod, reverse=True) - x, g)
        decay = each(lambda x: jnp.exp(jnp.where(low_incl, gc_diff(x), -jnp.inf)), gc)
        kb = each(lambda a, b: a * b, k, beta)
        lmat = each(lambda a, b, dc: jnp.where(strict, _mm_nt(a, b) * dc, 0.0), kb, k, decay)
        egc = each(jnp.exp, gc)
        rhs = each(lambda a, b, c_, e: jnp.concatenate([a * b, c_ * e], axis=1), v, beta, kb, egc)

        xc = each(compact, lmat)
        tinv = each(lambda x: eye_c - x, xc)
        xs = each(split_expand, xc)
        level = 0
        p = 1
        while 2 * p < CHUNK:
            if level < 2:
                xc = each(lambda s4: dot(s4[0], s4[2]) + (dot(s4[0], s4[3]) + dot(s4[1], s4[2])), xs)
                xs = each(split_expand, xc)
                ts = each(_split2, tinv)
                tinv = each(lambda t, t2, s4: t + (dot(t2[0], s4[2]) + (dot(t2[0], s4[3]) + dot(t2[1], s4[2]))),
                            tinv, ts, xs)
            else:
                xh = each(lambda s4: dot(s4[0], s4[2]).astype(BF16), xs)
                xs = each(lambda x: (x, None, expand(x), None), xh)
                tinv = each(lambda t, s4: t + dot(t.astype(BF16), s4[2]), tinv, xs)
            level += 1
            p *= 2
        ts = each(split_expand, tinv)
        rs = each(_split2, rhs)
        sol = each(lambda t4, r2: dot(t4[2], r2[0]) + (dot(t4[2], r2[1]) + dot(t4[3], r2[0])), ts, rs)

        qk = each(lambda a, b, dc: jnp.where(low_incl, _mm_nt(a, b) * dc, 0.0), q, k, decay)
        q_dec = each(lambda a, e: a * e, q, egc)
        k_dec = each(lambda a, r: a * jnp.exp(r), k, grev)
        g_last = each(lambda a, r: jnp.exp(a + r), gc, grev)

        v_news = [[] for _ in hs]
        o_inter = [[] for _ in hs]
        s = [s_ref[h] for h in hs] if chained else None
        for c in range(nch):
            rc = slice(c * CHUNK, (c + 1) * CHUNK)
            if not chained:
                s = [s0_ref[c, h] for h in hs]
            v_new = each(lambda so, st: so[rc, :hd] - _mm(so[rc, hd:], st), sol, s)
            o_c = each(lambda qd, st: _mm(qd[rc], st), q_dec, s)
            s = each(lambda st, gl, kd, vn: st * gl[c * CHUNK:c * CHUNK + 1, :] + _mm_tn(kd[rc], vn),
                     s, g_last, k_dec, v_new)
            for j, h in enumerate(hs):
                v_news[j].append(v_new[j])
                o_inter[j].append(o_c[j])
                if not chained:
                    sout_ref[c, h] = s[j]
        if chained:
            for j, h in enumerate(hs):
                s_ref[h] = s[j]

            @pl.when(i == pl.num_programs(1) - 1)
            def _(s=s, hs=hs):
                for j, h in enumerate(hs):
                    sout_ref[h] = s[j]

        o = each(lambda oi, m, vn: jnp.concatenate(oi, axis=0) + _mm(m, jnp.concatenate(vn, axis=0)),
                 o_inter, qk, v_news)
        for j, h in enumerate(hs):
            o_ref[:, h * hd:(h + 1) * hd] = _rms(o[j], go_ref[...]) * _silu(z_ref[:, h * hd:(h + 1) * hd])


def _gdn_delta(act, bg, zsrc, zcol, n_seq, t, s0, g_o, n_heads):
    d = act.shape[1] // 3
    hd = d // n_heads
    tb = 4 * CHUNK
    chained = t > CHUNK
    if chained:
        assert t % tb == 0
        grid = (n_seq, t // tb)
        rows = lambda b, i: b * (t // tb) + i
        sspec = pl.BlockSpec((None, n_heads, hd, hd), lambda b, i: (b, 0, 0, 0))
    else:
        assert t == CHUNK and n_seq % (tb // CHUNK) == 0
        grid = (n_seq * t // tb, 1)
        rows = lambda b, i: b
        sspec = pl.BlockSpec((tb // CHUNK, n_heads, hd, hd), lambda b, i: (b, 0, 0, 0))
    return pl.pallas_call(
        functools.partial(_gdn_delta_kernel, n_heads=n_heads, chained=chained, head_group=4),
        grid=grid,
        in_specs=[pl.BlockSpec((tb, 3 * d), lambda b, i: (rows(b, i), 0)),
                  pl.BlockSpec((tb, d), lambda b, i: (rows(b, i), zcol)),
                  pl.BlockSpec((tb, LANES), lambda b, i: (rows(b, i), 0)),
                  sspec,
                  pl.BlockSpec((1, hd), lambda b, i: (0, 0))],
        out_specs=[pl.BlockSpec((tb, d), lambda b, i: (rows(b, i), 0)), sspec],
        out_shape=[jax.ShapeDtypeStruct((n_seq * t, d), F32),
                   jax.ShapeDtypeStruct((n_seq, n_heads, hd, hd), F32)],
        scratch_shapes=[pltpu.VMEM((n_heads, hd, hd), F32)],
        compiler_params=_cp("parallel", "arbitrary"),
        name="gdn_delta",
    )(act, zsrc, bg, s0, g_o.reshape(1, hd))


def _rglru_kernel(h_ref, g_ref, win_ref, cw_ref, cb_ref, wgate_ref, bra_ref, bri_ref, lam_ref, wout_ref,
                  hist_ref, h0_ref, oin_ref, o_ref, hlast_ref, xtail_ref, cx_ref, ch_ref, *, n_blocks):
    del oin_ref
    i = pl.program_id(1)
    tb, d = h_ref.shape
    cw = lam_ref.shape[1]
    blk = cw // n_blocks

    @pl.when(i == 0)
    def _():
        cx_ref[...] = hist_ref[...]
        ch_ref[...] = h0_ref[...]

    h = h_ref[...]
    proj = jnp.dot(_rms(h, g_ref[...]).astype(BF16), win_ref[...], preferred_element_type=F32)
    y_br = proj[:, :cw]
    x_br = proj[:, cw:]
    xc = _dwconv(x_br, cx_ref[...], cw_ref[...]) + cb_ref[...]
    r_parts, i_parts = [], []
    for n in range(n_blocks):
        ri = jnp.dot(xc[:, n * blk:(n + 1) * blk].astype(BF16), wgate_ref[n], preferred_element_type=F32)
        r_parts.append(ri[:, :blk])
        i_parts.append(ri[:, blk:])
    r = _sigmoid(jnp.concatenate(r_parts, axis=1) + bra_ref[...])
    ig = _sigmoid(jnp.concatenate(i_parts, axis=1) + bri_ref[...])
    log_a = -LRU_C * r * _softplus(-lam_ref[...])
    a = jnp.exp(log_a)
    th = jnp.tanh(log_a)
    b = jnp.sqrt(-2.0 * th / (1.0 - th)) * (ig * xc)
    row = lax.broadcasted_iota(jnp.int32, (tb, cw), 0)
    h_prev = ch_ref[SUBLANES - 1:SUBLANES, :]
    b = b + jnp.where(row == 0, a * h_prev, 0.0)
    step = 1
    while step < tb:
        a_s = jnp.where(row >= step, pltpu.roll(a, step, axis=0), 1.0)
        b_s = jnp.where(row >= step, pltpu.roll(b, step, axis=0), 0.0)
        b = a * b_s + b
        a = a * a_s
        step *= 2
    hs = b
    cx_ref[...] = x_br[tb - SUBLANES:, :]
    ch_ref[...] = hs[tb - SUBLANES:, :]

    @pl.when(i == pl.num_programs(1) - 1)
    def _():
        hlast_ref[...] = hs[tb - SUBLANES:, :]
        xtail_ref[...] = x_br[tb - SUBLANES:, :]

    o_ref[...] = h + jnp.dot((hs * _gelu(y_br)).astype(BF16), wout_ref[...], preferred_element_type=F32)


def _rglru(h, h_out, row0, n_seq, t, tb, g, w_in, conv_w, conv_b, wgate, b_ra, b_ri, lam, w_out, hist8, h08):
    m, d = h.shape
    cw = lam.shape[0]
    nb = wgate.shape[0]
    nt = t // tb
    r0 = row0 // tb
    vec = lambda x: x.reshape(1, -1)
    cst = lambda shape: pl.BlockSpec(shape, lambda b, i: (0,) * len(shape))
    in_specs = [pl.BlockSpec((tb, d), lambda b, i: (r0 + b * nt + i, 0)),
                cst((1, d)), cst((d, 2 * cw)), cst((CONV_W, cw)), cst((1, cw)),
                cst(wgate.shape), cst((1, cw)), cst((1, cw)), cst((1, cw)), cst((cw, d)),
                pl.BlockSpec((None, SUBLANES, cw), lambda b, i: (b, 0, 0)),
                pl.BlockSpec((None, SUBLANES, cw), lambda b, i: (b, 0, 0))]
    args = [h, vec(g), w_in, conv_w, vec(conv_b), wgate, vec(b_ra), vec(b_ri), vec(lam), w_out, hist8, h08]
    aliases = {}
    if h_out is not None:
        in_specs.append(pl.BlockSpec(memory_space=pl.ANY))
        args.append(h_out)
        aliases = {len(args) - 1: 0}
    kern = functools.partial(_rglru_kernel, n_blocks=nb)
    if h_out is None:
        kern = lambda *refs: _rglru_kernel(*refs[:12], None, *refs[12:], n_blocks=nb)
    return pl.pallas_call(
        kern,
        grid=(n_seq, nt),
        in_specs=in_specs,
        out_specs=[pl.BlockSpec((tb, d), lambda b, i: (r0 + b * nt + i, 0)),
                   pl.BlockSpec((None, SUBLANES, cw), lambda b, i: (b, 0, 0)),
                   pl.BlockSpec((None, SUBLANES, cw), lambda b, i: (b, 0, 0))],
        out_shape=[jax.ShapeDtypeStruct((m, d), F32),
                   jax.ShapeDtypeStruct((n_seq, SUBLANES, cw), F32),
                   jax.ShapeDtypeStruct((n_seq, SUBLANES, cw), F32)],
        scratch_shapes=[pltpu.VMEM((SUBLANES, cw), F32), pltpu.VMEM((SUBLANES, cw), F32)],
        input_output_aliases=aliases,
        compiler_params=_cp("parallel", "arbitrary"),
        name="rglru",
    )(*args)


def _gmlp_kernel(h_ref, g_ref, win_ref, bin_ref, lng_ref, lnb_ref, ws_ref, bs_ref, wout_ref, o_ref, vn_ref,
                 *, n_groups, first_sample_tile, lc_prompt, lc_sample):
    i = pl.program_id(0)
    tm, d = h_ref.shape
    half = lng_ref.shape[1]
    gw = half // n_groups
    h = h_ref[...]
    act = _gelu(jnp.dot(_rms(h, g_ref[...]).astype(BF16), win_ref[...], preferred_element_type=F32) + bin_ref[...])
    u = act[:, :half]
    v = act[:, half:]
    vc = v - jnp.mean(v, axis=-1, keepdims=True)
    vn = vc * lax.rsqrt(jnp.mean(vc * vc, axis=-1, keepdims=True) + EPS) * lng_ref[...] + lnb_ref[...]
    vn_ref[...] = vn
    shift = jnp.where(i >= first_sample_tile, lc_sample.bit_length() - 1, lc_prompt.bit_length() - 1)
    ri = lax.broadcasted_iota(jnp.int32, (tm, tm), 0)
    ci = lax.broadcasted_iota(jnp.int32, (tm, tm), 1)
    keep = (lax.shift_right_logical(ri, shift) == lax.shift_right_logical(ci, shift)) & (ri >= ci)
    vnb = vn.astype(BF16)
    parts = []
    for gi in range(n_groups):
        ws = jnp.where(keep, ws_ref[0, gi], 0.0).astype(BF16)
        sg = jnp.dot(ws, vnb[:, gi * gw:(gi + 1) * gw], preferred_element_type=F32)
        parts.append(u[:, gi * gw:(gi + 1) * gw] * (sg + bs_ref[0, :, gi * gw:(gi + 1) * gw]))
    gated = jnp.concatenate(parts, axis=1).astype(BF16)
    o_ref[...] = h + jnp.dot(gated, wout_ref[...], preferred_element_type=F32)


def _gmlp(h, m_p, t_s, g, w_in, b_in, ln_g, ln_b, w_s, b_s, w_out):
    m, d = h.shape
    half = ln_g.shape[0]
    ng = w_s.shape[0]
    gw = half // ng
    tm = 256
    lc_p, lc_s = D_CHUNK, min(t_s, D_CHUNK)
    assert m_p % tm == 0 and (m - m_p) % tm == 0 and tm % lc_p == 0 and tm % lc_s == 0

    def tiled(lc):
        wt = jnp.tile(w_s[:, :lc, :lc], (1, tm // lc, tm // lc))
        bt = jnp.repeat(jnp.tile(b_s[:, :lc].T, (tm // lc, 1)), gw, axis=1)
        return wt, bt

    wt_p, bt_p = tiled(lc_p)
    wt_s, bt_s = tiled(lc_s)
    ws_all = jnp.stack([wt_p, wt_s])
    bs_all = jnp.stack([bt_p, bt_s])
    fst = m_p // tm
    vec = lambda x: x.reshape(1, -1)
    cst = lambda shape: pl.BlockSpec(shape, lambda i: (0,) * len(shape))
    return pl.pallas_call(
        functools.partial(_gmlp_kernel, n_groups=ng, first_sample_tile=fst, lc_prompt=lc_p, lc_sample=lc_s),
        grid=(m // tm,),
        in_specs=[pl.BlockSpec((tm, d), lambda i: (i, 0)),
                  cst((1, d)), cst((d, 2 * half)), cst((1, 2 * half)), cst((1, half)), cst((1, half)),
                  pl.BlockSpec((1, ng, tm, tm), lambda i: (i // fst, 0, 0, 0)),
                  pl.BlockSpec((1, tm, half), lambda i: (i // fst, 0, 0)),
                  cst((half, d))],
        out_specs=[pl.BlockSpec((tm, d), lambda i: (i, 0)),
                   pl.BlockSpec((tm, half), lambda i: (jnp.maximum(i - fst, 0), 0))],
        out_shape=[jax.ShapeDtypeStruct((m, d), F32),
                   jax.ShapeDtypeStruct((m - m_p, half), F32)],
        compiler_params=_cp("arbitrary"),
        name="gmlp",
    )(h, vec(g), w_in, vec(b_in), vec(ln_g), vec(ln_b), ws_all, bs_all, w_out)


def _pad_rows8(x3):
    n, r, c = x3.shape
    return jnp.concatenate([jnp.zeros((n, SUBLANES - r, c), x3.dtype), x3], axis=1)


def kernel(x_prompt, x_sample, p_prompt, p_sample, cache_k_a, cache_v_a, state_s_b, state_conv_b, state_h_c, state_conv_c, g_mix, g_ffn, g_ple, g_final, w_in_a, g_q_a, g_k_a, rel_a, w_out_a, w_in_b, conv_b, a_log_b, dt_bias_b, g_o_b, w_out_b, w_in_c, conv_c, conv_bias_c, w_ra_c, b_ra_c, w_ri_c, b_ri_c, lam_c, w_out_c, w_in_d, b_in_d, ln_g_d, ln_b_d, w_s_d, b_s_d, w_out_d, w_gate_ff, w_up_ff, w_down_ff, w_router, w_gate_e, w_up_e, w_down_e, w_ple, w_ple_gate):
    n_b, t, d = x_prompt.shape
    n_s, t_s, _ = x_sample.shape
    depth = p_prompt.shape[0]
    pd = p_prompt.shape[-1]
    m_p, m_s = n_b * t, n_s * t_s
    m = m_p + m_s
    a_heads = d // g_q_a.shape[-1]
    b_heads = a_log_b.shape[-1]
    n_exp = w_router.shape[-1]
    bf = lambda x: x.astype(BF16)

    h = jnp.concatenate([x_prompt.reshape(m_p, d), x_sample.reshape(m_s, d)], axis=0)
    p_all = jnp.concatenate([p_prompt.reshape(depth, m_p, pd), p_sample.reshape(depth, m_s, pd)], axis=1)

    outs = {name: [] for name in ("k_a_p", "v_a_p", "k_a_s", "v_a_s", "s_b_p", "conv_b_p", "s_b_s", "conv_b_s",
                                  "h_c_p", "conv_c_p", "h_c_s", "conv_c_s", "v_d_s")}

    def tail_rows(x2, n_seq, tt, row0, cols):
        x3 = x2[row0:row0 + n_seq * tt, :cols].reshape(n_seq, tt, cols)
        return x3[:, tt - (CONV_W - 1):, :]

    y = None
    for li in range(depth):
        kind, c = li % 4, li // 4
        if kind == 0:
            qkv = _proj(h, g_mix[li], bf(w_in_a[c]), 1024)
            w_keep = cache_k_a.shape[2]
            o, kn_p, kn_s = _attention(qkv, n_b, t, n_s, t_s,
                                       cache_k_a[c].reshape(n_s, w_keep, d), cache_v_a[c].reshape(n_s, w_keep, d),
                                       rel_a[c], g_q_a[c], g_k_a[c], a_heads)
            hd = d // a_heads
            keep = min(A_PREV_CHUNKS * CHUNK, t)
            v_p = qkv[:m_p, 2 * d:].reshape(n_b, t, d)[:, t - keep:]
            outs["k_a_p"].append(kn_p.reshape(n_b, keep, a_heads, hd))
            outs["v_a_p"].append(v_p.reshape(n_b, keep, a_heads, hd))
            outs["k_a_s"].append(kn_s.reshape(n_s, t_s, a_heads, hd))
            outs["v_a_s"].append(qkv[m_p:, 2 * d:].reshape(n_s, t_s, a_heads, hd))
            h = _outproj_res(h, o, bf(w_out_a[c]))
        elif kind == 1:
            w_in = w_in_b[c]
            nq = 3 * d
            w_main = bf(w_in[:, :nq + d])
            w_aux = bf(jnp.pad(w_in[:, nq + d:], ((0, 0), (0, LANES - 2 * b_heads))))
            proj, aux = _proj(h, g_mix[li], w_main, 1024, w_aux)
            pvec = jnp.zeros((2, LANES), F32)
            pvec = pvec.at[0, b_heads:2 * b_heads].set(a_log_b[c]).at[1, b_heads:2 * b_heads].set(dt_bias_b[c])
            hd = d // b_heads
            act_p, bg_p = _gdn_act(proj, aux, 0, n_b, t, 256, jnp.zeros((n_b, SUBLANES, nq), F32),
                                   conv_b[c], pvec, b_heads)
            o_p, s_p = _gdn_delta(act_p, bg_p, proj, 3, n_b, t,
                                  jnp.zeros((n_b, b_heads, hd, hd), F32), g_o_b[c], b_heads)
            act_s, bg_s = _gdn_act(proj, aux, m_p, n_s, t_s, t_s, _pad_rows8(state_conv_b[c]),
                                   conv_b[c], pvec, b_heads)
            tp = -(-t_s // CHUNK) * CHUNK
            padt = lambda x2: jnp.pad(x2.reshape(n_s, t_s, -1), ((0, 0), (0, tp - t_s), (0, 0))).reshape(n_s * tp, -1)
            z_s = padt(proj[m_p:, nq:nq + d])
            o_s, s_s = _gdn_delta(padt(act_s), padt(bg_s), z_s, 0, n_s, tp, state_s_b[c], g_o_b[c], b_heads)
            o_s = o_s.reshape(n_s, tp, d)[:, :t_s].reshape(m_s, d)
            outs["s_b_p"].append(s_p)
            outs["s_b_s"].append(s_s)
            outs["conv_b_p"].append(tail_rows(proj, n_b, t, 0, nq))
            outs["conv_b_s"].append(tail_rows(proj, n_s, t_s, m_p, nq))
            h = _outproj_res(h, jnp.concatenate([o_p, o_s], axis=0), bf(w_out_b[c]))
        elif kind == 2:
            cw = lam_c.shape[-1]
            wgate = bf(jnp.concatenate([w_ra_c[c], w_ri_c[c]], axis=-1))
            common = (g_mix[li], bf(w_in_c[c]), conv_c[c], conv_bias_c[c], wgate, b_ra_c[c], b_ri_c[c],
                      lam_c[c], bf(w_out_c[c]))
            zeros8 = jnp.zeros((n_b, SUBLANES, cw), F32)
            h_new, hl_p, xt_p = _rglru(h, None, 0, n_b, t, 256, *common, zeros8, zeros8)
            h_new, hl_s, xt_s = _rglru(h, h_new, m_p, n_s, t_s, t_s, *common,
                                       _pad_rows8(state_conv_c[c]), _pad_rows8(state_h_c[c][:, None, :]))
            h = h_new
            outs["h_c_p"].append(hl_p[:, SUBLANES - 1])
            outs["h_c_s"].append(hl_s[:, SUBLANES - 1])
            outs["conv_c_p"].append(xt_p[:, SUBLANES - (CONV_W - 1):])
            outs["conv_c_s"].append(xt_s[:, SUBLANES - (CONV_W - 1):])
        else:
            h, vn_s = _gmlp(h, m_p, t_s, g_mix[li], bf(w_in_d[c]), b_in_d[c], ln_g_d[c], ln_b_d[c],
                            w_s_d[c], b_s_d[c], bf(w_out_d[c]))
            outs["v_d_s"].append(vn_s.reshape(n_s, t_s, -1))

        j = li // 2
        if li % 2 == 0:
            h = _ffn_dense(h, g_ffn[li], bf(w_gate_ff[j]), bf(w_up_ff[j]), bf(w_down_ff[j]))
        else:
            wr = jnp.pad(w_router[j], ((0, 0), (0, LANES - n_exp)))
            h = _moe(h, g_ffn[li], wr, bf(w_gate_e[j]), bf(w_up_e[j]), bf(w_down_e[j]))
        last = li == depth - 1
        res = _ple(h, p_all[li], g_ple[li], bf(w_ple[li]), bf(w_ple_gate[li]), g_final if last else None)
        if last:
            y = res
        else:
            h = res

    st = {name: jnp.stack(rows) for name, rows in outs.items()}
    return (y[:m_p].reshape(n_b, t, d), y[m_p:].reshape(n_s, t_s, d),
            st["k_a_p"], st["v_a_p"], st["k_a_s"], st["v_a_s"],
            st["s_b_p"], st["conv_b_p"], st["s_b_s"], st["conv_b_s"],
            st["h_c_p"], st["conv_c_p"], st["h_c_s"], st["conv_c_s"],
            st["v_d_s"])
```

```python
import functools

import jax
import jax.numpy as jnp
from jax import lax
from jax.experimental import pallas as pl
from jax.experimental.pallas import tpu as pltpu

F32 = jnp.float32
BF16 = jnp.bfloat16
EPS = 1e-6
CHUNK = 64
CONV_W = 4
A_REL_CLIP = 128
A_PREV_CHUNKS = 8
LRU_C = 8.0
D_CHUNK = 128
TOP_K = 2
LANES = 128
SUBLANES = 8
VMEM_LIMIT = 48 * 1024 * 1024


def _cp(*sem, vmem=VMEM_LIMIT):
    return pltpu.CompilerParams(dimension_semantics=sem, vmem_limit_bytes=vmem)


def _tile_m(m):
    for t in (1024, 512, 256):
        if m % t == 0:
            return t
    raise ValueError(f"token count {m} must be a multiple of 256")


def _mm(a, b):
    return jnp.dot(a.astype(BF16), b.astype(BF16), preferred_element_type=F32)


def _mm_nt(a, b):
    return lax.dot_general(a.astype(BF16), b.astype(BF16), (((1,), (1,)), ((), ())),
                           preferred_element_type=F32)


def _mm_tn(a, b):
    return lax.dot_general(a.astype(BF16), b.astype(BF16), (((0,), (0,)), ((), ())),
                           preferred_element_type=F32)


def _split2(x):
    hi = x.astype(BF16)
    lo = (x - hi.astype(F32)).astype(BF16)
    return hi, lo


def _split3(x):
    hi = x.astype(BF16)
    r = x - hi.astype(F32)
    mid = r.astype(BF16)
    lo = (r - mid.astype(F32)).astype(BF16)
    return hi, mid, lo


def _mm3(a, b):
    ah, al = _split2(a)
    bh, bl = _split2(b)
    d = functools.partial(jnp.dot, preferred_element_type=F32)
    return d(ah, bh) + (d(ah, bl) + d(al, bh))


def _rms(x, g):
    return x * lax.rsqrt(jnp.mean(x * x, axis=-1, keepdims=True) + EPS) * g


def _sigmoid(x):
    return jax.nn.sigmoid(x)


def _silu(x):
    return x * _sigmoid(x)


def _softplus(x):
    return jnp.maximum(x, 0.0) + jnp.log1p(jnp.exp(-jnp.abs(x)))


def _gelu(x):
    c = 0.7978845608028654
    return x * (0.5 * (1.0 + jnp.tanh(c * (x + 0.044715 * (x * x * x)))))


def _shift_rows(x, prev8, s):
    r = pltpu.roll(x, s, axis=0)
    pr = pltpu.roll(prev8, s, axis=0)
    row8 = lax.broadcasted_iota(jnp.int32, pr.shape, 0)
    top = jnp.where(row8 < s, pr, r[0:SUBLANES])
    return jnp.concatenate([top, r[SUBLANES:]], axis=0)


def _dwconv(x, prev8, w):
    y = None
    for k in range(CONV_W):
        s = CONV_W - 1 - k
        xs = x if s == 0 else _shift_rows(x, prev8, s)
        term = xs * w[k:k + 1]
        y = term if y is None else y + term
    return y


def _proj_kernel(*refs, has_aux):
    if has_aux:
        x_ref, g_ref, w_ref, waux_ref, o_ref, oaux_ref, hn_ref = refs
    else:
        x_ref, g_ref, w_ref, o_ref, hn_ref = refs

    @pl.when(pl.program_id(1) == 0)
    def _():
        hn = _rms(x_ref[...], g_ref[...]).astype(BF16)
        hn_ref[...] = hn
        if has_aux:
            oaux_ref[...] = jnp.dot(hn, waux_ref[...], preferred_element_type=F32)

    o_ref[...] = jnp.dot(hn_ref[...], w_ref[...], preferred_element_type=F32)


def _proj(h, g, w, tn, waux=None):
    m, k = h.shape
    n = w.shape[1]
    tm = _tile_m(m)
    in_specs = [pl.BlockSpec((tm, k), lambda i, j: (i, 0)),
                pl.BlockSpec((1, k), lambda i, j: (0, 0)),
                pl.BlockSpec((k, tn), lambda i, j: (0, j))]
    out_shape = [jax.ShapeDtypeStruct((m, n), F32)]
    out_specs = [pl.BlockSpec((tm, tn), lambda i, j: (i, j))]
    args = [h, g.reshape(1, k), w]
    if waux is not None:
        na = waux.shape[1]
        in_specs.append(pl.BlockSpec((k, na), lambda i, j: (0, 0)))
        out_shape.append(jax.ShapeDtypeStruct((m, na), F32))
        out_specs.append(pl.BlockSpec((tm, na), lambda i, j: (i, 0)))
        args.append(waux)
    res = pl.pallas_call(
        functools.partial(_proj_kernel, has_aux=waux is not None),
        grid=(m // tm, n // tn),
        in_specs=in_specs, out_specs=out_specs, out_shape=out_shape,
        scratch_shapes=[pltpu.VMEM((tm, k), BF16)],
        compiler_params=_cp("parallel", "arbitrary"),
        name="proj",
    )(*args)
    return res if waux is not None else res[0]


def _outproj_kernel(h_ref, o_ref, w_ref, out_ref):
    out_ref[...] = h_ref[...] + jnp.dot(o_ref[...].astype(BF16), w_ref[...], preferred_element_type=F32)


def _outproj_res(h, o, w):
    m, d = h.shape
    k = o.shape[1]
    tm = _tile_m(m)
    return pl.pallas_call(
        _outproj_kernel,
        grid=(m // tm,),
        in_specs=[pl.BlockSpec((tm, d), lambda i: (i, 0)),
                  pl.BlockSpec((tm, k), lambda i: (i, 0)),
                  pl.BlockSpec((k, d), lambda i: (0, 0))],
        out_specs=pl.BlockSpec((tm, d), lambda i: (i, 0)),
        out_shape=jax.ShapeDtypeStruct((m, d), F32),
        compiler_params=_cp("parallel"),
        name="outproj_res",
    )(h, o, w)


def _ffn_kernel(x_ref, g_ref, wg_ref, wu_ref, wd_ref, o_ref, hn_ref):
    @pl.when(pl.program_id(1) == 0)
    def _():
        x = x_ref[...]
        hn_ref[...] = _rms(x, g_ref[...]).astype(BF16)
        o_ref[...] = x

    hn = hn_ref[...]
    gg = jnp.dot(hn, wg_ref[...], preferred_element_type=F32)
    uu = jnp.dot(hn, wu_ref[...], preferred_element_type=F32)
    hid = (_silu(gg) * uu).astype(BF16)
    o_ref[...] += jnp.dot(hid, wd_ref[...], preferred_element_type=F32)


def _ffn_dense(h, g, wg, wu, wd):
    m, d = h.shape
    f = wg.shape[1]
    tm = _tile_m(m)
    tf = 256
    return pl.pallas_call(
        _ffn_kernel,
        grid=(m // tm, f // tf),
        in_specs=[pl.BlockSpec((tm, d), lambda i, j: (i, 0)),
                  pl.BlockSpec((1, d), lambda i, j: (0, 0)),
                  pl.BlockSpec((d, tf), lambda i, j: (0, j)),
                  pl.BlockSpec((d, tf), lambda i, j: (0, j)),
                  pl.BlockSpec((tf, d), lambda i, j: (j, 0))],
        out_specs=pl.BlockSpec((tm, d), lambda i, j: (i, 0)),
        out_shape=jax.ShapeDtypeStruct((m, d), F32),
        scratch_shapes=[pltpu.VMEM((tm, d), BF16)],
        compiler_params=_cp("parallel", "arbitrary"),
        name="ffn_dense",
    )(h, g.reshape(1, d), wg, wu, wd)


def _moe_kernel(x_ref, g_ref, wr_ref, wg_ref, wu_ref, wd_ref, o_ref, hn_ref, gates_ref, rank_ref, rankt_ref,
                *, n_experts, cap):
    e = pl.program_id(1)
    tm, d = x_ref.shape

    @pl.when(e == 0)
    def _():
        x = x_ref[...]
        hn = _rms(x, g_ref[...])
        hn_ref[...] = hn.astype(BF16)
        o_ref[...] = x
        logits = _mm3(hn, wr_ref[...])
        lane = lax.broadcasted_iota(jnp.int32, logits.shape, 1)
        lg = jnp.where(lane < n_experts, logits, -jnp.inf)
        m1 = jnp.max(lg, axis=-1, keepdims=True)
        i1 = jnp.min(jnp.where(lg == m1, lane, LANES), axis=-1, keepdims=True)
        lg2 = jnp.where(lane == i1, -jnp.inf, lg)
        m2 = jnp.max(lg2, axis=-1, keepdims=True)
        i2 = jnp.min(jnp.where(lg2 == m2, lane, LANES), axis=-1, keepdims=True)
        e2 = jnp.exp(m2 - m1)
        w1 = 1.0 / (1.0 + e2)
        w2 = e2 / (1.0 + e2)
        gates_ref[...] = jnp.where(lane == i1, w1, 0.0) + jnp.where(lane == i2, w2, 0.0)
        sel = jnp.where(lane == i1, 1.0, jnp.where(lane == i2, 1.0, 0.0))
        ri = lax.broadcasted_iota(jnp.int32, (tm, tm), 0)
        ci = lax.broadcasted_iota(jnp.int32, (tm, tm), 1)
        tril = jnp.where(ri >= ci, 1.0, 0.0).astype(BF16)
        count = jnp.dot(tril, sel.astype(BF16), preferred_element_type=F32)
        rank = jnp.where(sel > 0.0, count - 1.0, -1.0)
        rank_ref[...] = rank
        rankt_ref[...] = rank.T

    rank_row = rankt_ref[pl.ds(e, 1), :]
    n_routed = (jnp.max(rank_row) + 1.0).astype(jnp.int32)
    lane = lax.broadcasted_iota(jnp.int32, (tm, LANES), 1)
    rank_col = jnp.sum(jnp.where(lane == e, rank_ref[...], 0.0), axis=-1, keepdims=True)
    gate_col = jnp.sum(jnp.where(lane == e, gates_ref[...], 0.0), axis=-1, keepdims=True)
    row_id = lax.broadcasted_iota(jnp.int32, (cap, tm), 0).astype(F32)
    col_id = lax.broadcasted_iota(jnp.int32, (tm, cap), 1).astype(F32)

    def chunk(cc, carry):
        base = (cc * cap).astype(F32)
        pick = jnp.where(rank_row == row_id + base, 1.0, 0.0).astype(BF16)
        xe = jnp.dot(pick, hn_ref[...], preferred_element_type=F32).astype(BF16)
        gg = jnp.dot(xe, wg_ref[0], preferred_element_type=F32)
        uu = jnp.dot(xe, wu_ref[0], preferred_element_type=F32)
        hid = (_silu(gg) * uu).astype(BF16)
        ye = jnp.dot(hid, wd_ref[0], preferred_element_type=F32).astype(BF16)
        put = jnp.where(rank_col == col_id + base, gate_col, 0.0).astype(BF16)
        o_ref[...] += jnp.dot(put, ye, preferred_element_type=F32)
        return carry

    lax.fori_loop(0, (n_routed + cap - 1) // cap, chunk, 0)


def _moe(h, g, wr, wg, wu, wd):
    m, d = h.shape
    ne, _, ef = wg.shape
    tm = 512
    cap = 160
    assert m % tm == 0
    return pl.pallas_call(
        functools.partial(_moe_kernel, n_experts=ne, cap=cap),
        grid=(m // tm, ne),
        in_specs=[pl.BlockSpec((tm, d), lambda i, e: (i, 0)),
                  pl.BlockSpec((1, d), lambda i, e: (0, 0)),
                  pl.BlockSpec((d, LANES), lambda i, e: (0, 0)),
                  pl.BlockSpec((1, d, ef), lambda i, e: (e, 0, 0)),
                  pl.BlockSpec((1, d, ef), lambda i, e: (e, 0, 0)),
                  pl.BlockSpec((1, ef, d), lambda i, e: (e, 0, 0))],
        out_specs=pl.BlockSpec((tm, d), lambda i, e: (i, 0)),
        out_shape=jax.ShapeDtypeStruct((m, d), F32),
        scratch_shapes=[pltpu.VMEM((tm, d), BF16), pltpu.VMEM((tm, LANES), F32),
                        pltpu.VMEM((tm, LANES), F32), pltpu.VMEM((LANES, tm), F32)],
        compiler_params=_cp("parallel", "arbitrary"),
        name="moe",
    )(h, g.reshape(1, d), wr, wg, wu, wd)


def _ple_kernel(*refs, final):
    if final:
        x_ref, p_ref, g_ref, wp_ref, wg_ref, gf_ref, o_ref = refs
    else:
        x_ref, p_ref, g_ref, wp_ref, wg_ref, o_ref = refs
    x = x_ref[...]
    hn = _rms(x, g_ref[...]).astype(BF16)
    gate = _sigmoid(jnp.dot(hn, wg_ref[...], preferred_element_type=F32))
    pp = jnp.dot(p_ref[...].astype(BF16), wp_ref[...], preferred_element_type=F32)
    out = x + pp * gate
    o_ref[...] = _rms(out, gf_ref[...]) if final else out


def _ple(h, p, g, wp, wgate, g_final=None):
    m, d = h.shape
    pd = p.shape[1]
    tm = _tile_m(m)
    final = g_final is not None
    in_specs = [pl.BlockSpec((tm, d), lambda i: (i, 0)),
                pl.BlockSpec((tm, pd), lambda i: (i, 0)),
                pl.BlockSpec((1, d), lambda i: (0, 0)),
                pl.BlockSpec((pd, d), lambda i: (0, 0)),
                pl.BlockSpec((d, d), lambda i: (0, 0))]
    args = [h, p, g.reshape(1, d), wp, wgate]
    if final:
        in_specs.append(pl.BlockSpec((1, d), lambda i: (0, 0)))
        args.append(g_final.reshape(1, d))
    return pl.pallas_call(
        functools.partial(_ple_kernel, final=final),
        grid=(m // tm,),
        in_specs=in_specs,
        out_specs=pl.BlockSpec((tm, d), lambda i: (i, 0)),
        out_shape=jax.ShapeDtypeStruct((m, d), F32),
        compiler_params=_cp("parallel"),
        name="ple",
    )(*args)


def _pair_rms(x, g, hd):
    lane = lax.broadcasted_iota(jnp.int32, x.shape, 1)
    lo = lane < hd
    x2 = x * x
    s0 = jnp.sum(jnp.where(lo, x2, 0.0), axis=-1, keepdims=True)
    s1 = jnp.sum(jnp.where(lo, 0.0, x2), axis=-1, keepdims=True)
    ms = jnp.where(lo, s0, s1) * (1.0 / hd)
    return x * lax.rsqrt(ms + EPS) * g


def _stack_heads(q, hd):
    lane = lax.broadcasted_iota(jnp.int32, q.shape, 1)
    lo = lane < hd
    return jnp.concatenate([jnp.where(lo, q, 0.0), jnp.where(lo, 0.0, q)], axis=0).astype(BF16)


def _unstack_heads(o2, hd):
    n = o2.shape[0] // 2
    lane = lax.broadcasted_iota(jnp.int32, (n, o2.shape[1]), 1)
    return jnp.where(lane < hd, o2[:n], o2[n:])


def _attend(q2, kw, vw, bias, valid):
    s = _mm_nt(q2, kw) + bias
    if valid is not None:
        s = jnp.where(valid, s, -jnp.inf)
    mx = jnp.max(s, axis=-1, keepdims=True)
    e = jnp.exp(s - mx)
    den = jnp.sum(e, axis=-1, keepdims=True)
    return jnp.dot(e.astype(BF16), vw, preferred_element_type=F32) / den


def _attn_prompt_kernel(q_ref, k_ref, v_ref, b_ref, gq_ref, gk_ref, o_ref, kn_ref, kwin_ref, vwin_ref,
                        *, hd, scale, reach, chunk_group):
    i = pl.program_id(2)
    tq = q_ref.shape[0]
    band = reach + CHUNK
    qn = _pair_rms(q_ref[...], gq_ref[...], hd) * scale
    knc = _pair_rms(k_ref[...], gk_ref[...], hd)

    @pl.when(i == pl.num_programs(2) - 1)
    def _():
        kn_ref[...] = knc

    @pl.when(i == 0)
    def _():
        kwin_ref[0:tq, :] = jnp.zeros((tq, 2 * hd), BF16)
        vwin_ref[0:tq, :] = jnp.zeros((tq, 2 * hd), BF16)

    @pl.when(i > 0)
    def _():
        kwin_ref[0:tq, :] = kwin_ref[tq:2 * tq, :]
        vwin_ref[0:tq, :] = vwin_ref[tq:2 * tq, :]

    kwin_ref[tq:2 * tq, :] = knc.astype(BF16)
    vwin_ref[tq:2 * tq, :] = v_ref[...].astype(BF16)
    bias = b_ref[0]
    col = lax.broadcasted_iota(jnp.int32, (1, band), 1)
    n_chunks = tq // CHUNK
    w0s = [tq - reach + jj * CHUNK for jj in range(n_chunks)]

    def attend(first_block, jjs):
        q2 = [_stack_heads(qn[jj * CHUNK:(jj + 1) * CHUNK], hd) for jj in jjs]
        s = [_mm_nt(x, kwin_ref[w0s[jj]:w0s[jj] + band, :]) + bias for x, jj in zip(q2, jjs)]
        if first_block:
            s = [jnp.where(col + w0s[jj] >= tq, x, -jnp.inf) for x, jj in zip(s, jjs)]
        mx = [jnp.max(x, axis=-1, keepdims=True) for x in s]
        e = [jnp.exp(x - m) for x, m in zip(s, mx)]
        den = [jnp.sum(x, axis=-1, keepdims=True) for x in e]
        o2 = [jnp.dot(x.astype(BF16), vwin_ref[w0s[jj]:w0s[jj] + band, :], preferred_element_type=F32)
              for x, jj in zip(e, jjs)]
        for x, dn, jj in zip(o2, den, jjs):
            o_ref[jj * CHUNK:(jj + 1) * CHUNK, :] = _unstack_heads(x / dn, hd)

    def attend_all(first_block):
        for j0 in range(0, n_chunks, chunk_group):
            attend(first_block, list(range(j0, j0 + chunk_group)))

    @pl.when(i == 0)
    def _():
        attend_all(True)

    @pl.when(i > 0)
    def _():
        attend_all(False)


def _attn_sample_kernel(q_ref, k_ref, v_ref, ck_ref, cv_ref, b_ref, gq_ref, gk_ref, oin_ref, o_ref, kn_ref,
                        *, hd, scale):
    del oin_ref
    qn = _pair_rms(q_ref[...], gq_ref[...], hd) * scale
    kn = _pair_rms(k_ref[...], gk_ref[...], hd)
    kn_ref[...] = kn
    kcat = jnp.concatenate([ck_ref[...], kn], axis=0).astype(BF16)
    vcat = jnp.concatenate([cv_ref[...], v_ref[...]], axis=0).astype(BF16)
    o2 = _attend(_stack_heads(qn, hd), kcat, vcat, b_ref[0], None)
    o_ref[...] = _unstack_heads(o2, hd)


def _rel_bias_pairs(table, q0, nq, nk):
    span = nq + nk - 1
    dist = jnp.arange(span) + (q0 - (nk - 1))
    line = table[:, jnp.clip(dist, -A_REL_CLIP, A_REL_CLIP) + A_REL_CLIP]
    flipped = line[:, ::-1]
    b = jnp.stack([flipped[:, nq - 1 - q:nq - 1 - q + nk] for q in range(nq)], axis=1)
    nh = b.shape[0]
    return b.reshape(nh // 2, 2 * nq, nk)


def _attention(qkv, n_b, t, n_s, t_s, cache_k, cache_v, rel, g_q, g_k, n_heads):
    m_tot, d3 = qkv.shape
    d = d3 // 3
    hd = d // n_heads
    pw = 2 * hd
    n_pairs = d // pw
    m_p = n_b * t
    reach = A_PREV_CHUNKS * CHUNK
    tq = reach
    assert pw == LANES and t % tq == 0 and t_s % 16 == 0
    nqb = t // tq
    scale = hd ** -0.5
    gq2 = jnp.tile(g_q, 2).reshape(1, pw)
    gk2 = jnp.tile(g_k, 2).reshape(1, pw)

    bias_p = _rel_bias_pairs(rel, reach, CHUNK, reach + CHUNK)
    kcol, vcol = n_pairs, 2 * n_pairs

    def rows(b, p, i):
        return b * nqb + i

    blk = (tq, pw)
    o_all, kn_p = pl.pallas_call(
        functools.partial(_attn_prompt_kernel, hd=hd, scale=scale, reach=reach, chunk_group=4),
        grid=(n_b, n_pairs, nqb),
        in_specs=[pl.BlockSpec(blk, lambda b, p, i: (rows(b, p, i), p)),
                  pl.BlockSpec(blk, lambda b, p, i: (rows(b, p, i), kcol + p)),
                  pl.BlockSpec(blk, lambda b, p, i: (rows(b, p, i), vcol + p)),
                  pl.BlockSpec((1, 2 * CHUNK, reach + CHUNK), lambda b, p, i: (p, 0, 0)),
                  pl.BlockSpec((1, pw), lambda b, p, i: (0, 0)),
                  pl.BlockSpec((1, pw), lambda b, p, i: (0, 0))],
        out_specs=[pl.BlockSpec(blk, lambda b, p, i: (rows(b, p, i), p)),
                   pl.BlockSpec(blk, lambda b, p, i: (b, p))],
        out_shape=[jax.ShapeDtypeStruct((m_tot, d), F32),
                   jax.ShapeDtypeStruct((n_b * tq, d), F32)],
        scratch_shapes=[pltpu.VMEM((2 * tq, pw), BF16), pltpu.VMEM((2 * tq, pw), BF16)],
        compiler_params=_cp("parallel", "parallel", "arbitrary"),
        name="attn_prompt",
    )(qkv, qkv, qkv, bias_p, gq2, gk2)

    w = cache_k.shape[1]
    bias_s = _rel_bias_pairs(rel, w, t_s, w + t_s)
    srow = m_p // t_s
    sblk = (t_s, pw)
    o_all, kn_s = pl.pallas_call(
        functools.partial(_attn_sample_kernel, hd=hd, scale=scale),
        grid=(n_s, n_pairs),
        in_specs=[pl.BlockSpec(sblk, lambda b, p: (srow + b, p)),
                  pl.BlockSpec(sblk, lambda b, p: (srow + b, kcol + p)),
                  pl.BlockSpec(sblk, lambda b, p: (srow + b, vcol + p)),
                  pl.BlockSpec((None, w, pw), lambda b, p: (b, 0, p)),
                  pl.BlockSpec((None, w, pw), lambda b, p: (b, 0, p)),
                  pl.BlockSpec((1, 2 * t_s, w + t_s), lambda b, p: (p, 0, 0)),
                  pl.BlockSpec((1, pw), lambda b, p: (0, 0)),
                  pl.BlockSpec((1, pw), lambda b, p: (0, 0)),
                  pl.BlockSpec(memory_space=pl.ANY)],
        out_specs=[pl.BlockSpec(sblk, lambda b, p: (srow + b, p)),
                   pl.BlockSpec(sblk, lambda b, p: (b, p))],
        out_shape=[jax.ShapeDtypeStruct((m_tot, d), F32),
                   jax.ShapeDtypeStruct((n_s * t_s, d), F32)],
        input_output_aliases={8: 0},
        compiler_params=_cp("parallel", "parallel"),
        name="attn_sample",
    )(qkv, qkv, qkv, cache_k, cache_v, bias_s, gq2, gk2, o_all)
    return o_all, kn_p, kn_s


def _gdn_act_kernel(xq_ref, xk_ref, xv_ref, pq_ref, pk_ref, pv_ref, hq_ref, hk_ref, hv_ref,
                    aux_ref, cw_ref, pvec_ref, o_ref, bg_ref, *, blocks_per_seq, n_heads, scale):
    i = pl.program_id(0)
    start = (i % blocks_per_seq) == 0
    d = xq_ref.shape[1]
    hd = d // n_heads
    parts = ((xq_ref, pq_ref, hq_ref), (xk_ref, pk_ref, hk_ref), (xv_ref, pv_ref, hv_ref))
    for c, (x_ref, p_ref, h_ref) in enumerate(parts):
        prev8 = jnp.where(start, h_ref[...], p_ref[...])
        a = _silu(_dwconv(x_ref[...], prev8, cw_ref[:, c * d:(c + 1) * d]))
        for hh in range(n_heads):
            ah = a[:, hh * hd:(hh + 1) * hd]
            if c < 2:
                ah = ah * lax.rsqrt(jnp.sum(ah * ah, axis=-1, keepdims=True) + EPS)
            if c == 0:
                ah = ah * scale
            o_ref[:, c * d + hh * hd:c * d + (hh + 1) * hd] = ah
    ax = aux_ref[...]
    lane = lax.broadcasted_iota(jnp.int32, ax.shape, 1)
    beta = _sigmoid(ax)
    gdec = -jnp.exp(pvec_ref[0:1, :]) * _softplus(ax + pvec_ref[1:2, :])
    bg_ref[...] = jnp.where(lane < n_heads, beta, jnp.where(lane < 2 * n_heads, gdec, 0.0))


def _gdn_act(proj, aux, row0, n_seq, t, tb, hist8, conv_w, pvec, n_heads):
    d = proj.shape[1] // 4
    hd = d // n_heads
    bps = t // tb
    r0 = row0 // tb
    p8 = tb // SUBLANES

    def xspec(c):
        return pl.BlockSpec((tb, d), lambda i, c=c: (r0 + i, c))

    def pspec(c):
        return pl.BlockSpec((SUBLANES, d), lambda i, c=c: (jnp.maximum((r0 + i) * p8 - 1, 0), c))

    def hspec(c):
        return pl.BlockSpec((None, SUBLANES, d), lambda i, c=c: (i // bps, 0, c))

    m = n_seq * t
    return pl.pallas_call(
        functools.partial(_gdn_act_kernel, blocks_per_seq=bps, n_heads=n_heads, scale=hd ** -0.5),
        grid=(m // tb,),
        in_specs=[xspec(0), xspec(1), xspec(2), pspec(0), pspec(1), pspec(2), hspec(0), hspec(1), hspec(2),
                  pl.BlockSpec((tb, LANES), lambda i: (r0 + i, 0)),
                  pl.BlockSpec((CONV_W, 3 * d), lambda i: (0, 0)),
                  pl.BlockSpec((2, LANES), lambda i: (0, 0))],
        out_specs=[pl.BlockSpec((tb, 3 * d), lambda i: (i, 0)),
                   pl.BlockSpec((tb, LANES), lambda i: (i, 0))],
        out_shape=[jax.ShapeDtypeStruct((m, 3 * d), F32), jax.ShapeDtypeStruct((m, LANES), F32)],
        compiler_params=_cp("parallel"),
        name="gdn_act",
    )(proj, proj, proj, proj, proj, proj, hist8, hist8, hist8, aux, conv_w, pvec)


def _seg_cumsum(x, rowmod, reverse):
    n = x.shape[0]
    d = 1
    while d < CHUNK:
        if reverse:
            sh = pltpu.roll(x, n - d, axis=0)
            x = x + jnp.where(rowmod < CHUNK - d, sh, 0.0)
        else:
            sh = pltpu.roll(x, d, axis=0)
            x = x + jnp.where(rowmod >= d, sh, 0.0)
        d *= 2
    return x


def _gdn_delta_kernel(act_ref, z_ref, bg_ref, s0_ref, go_ref, o_ref, sout_ref, s_ref, *, n_heads, chained, head_group):
    i = pl.program_id(1)
    tb, d = z_ref.shape
    hd = d // n_heads
    nch = tb // CHUNK
    dot = functools.partial(jnp.dot, preferred_element_type=F32)

    if chained:
        @pl.when(i == 0)
        def _():
            s_ref[...] = s0_ref[...]

    bg = bg_ref[...]
    lane = lax.broadcasted_iota(jnp.int32, (tb, hd), 1)
    rowmod = lax.broadcasted_iota(jnp.int32, (tb, hd), 0) & (CHUNK - 1)
    ri = lax.broadcasted_iota(jnp.int32, (tb, tb), 0)
    ci = lax.broadcasted_iota(jnp.int32, (tb, tb), 1)
    same = (ri >> 6) == (ci >> 6)
    low_incl = same & (ri >= ci)
    strict = same & (ri > ci)
    eye_c = (lax.broadcasted_iota(jnp.int32, (CHUNK, tb), 0)
             == (lax.broadcasted_iota(jnp.int32, (CHUNK, tb), 1) & (CHUNK - 1))).astype(F32)
    def compact(x):
        acc = x[0:CHUNK]
        for c in range(1, nch):
            acc = acc + x[c * CHUNK:(c + 1) * CHUNK]
        return acc

    def expand(xc):
        return jnp.where(same, jnp.concatenate([xc] * nch, axis=0), jnp.zeros((), xc.dtype))

    def split_expand(xc):
        hi, lo = _split2(xc)
        return hi, lo, expand(hi), expand(lo)

    def each(fn, *lists):
        return [fn(*a) for a in zip(*lists)]

    def gc_diff(gc):
        gh, gm, gl = (piece.astype(F32) for piece in _split3(gc))
        lhs = jnp.where(lane == 0, gh, jnp.where(lane == 1, gm, jnp.where(lane == 2, gl,
                                                                          jnp.where(lane < 6, 1.0, 0.0))))
        rhs_t = jnp.where(lane < 3, 1.0, jnp.where(lane == 3, -gh, jnp.where(lane == 4, -gm,
                                                                             jnp.where(lane == 5, -gl, 0.0))))
        return _mm_nt(lhs, rhs_t)

    for h0 in range(0, n_heads, head_group):
        hs = list(range(h0, h0 + head_group))
        q = [act_ref[:, h * hd:(h + 1) * hd] for h in hs]
        k = [act_ref[:, d + h * hd:d + (h + 1) * hd] for h in hs]
        v = [act_ref[:, 2 * d + h * hd:2 * d + (h + 1) * hd] for h in hs]
        beta = [jnp.broadcast_to(jnp.sum(jnp.where(lane == h, bg, 0.0), axis=-1, keepdims=True), (tb, hd))
                for h in hs]
        g = [jnp.broadcast_to(jnp.sum(jnp.where(lane == n_heads + h, bg, 0.0), axis=-1, keepdims=True), (tb, hd))
             for h in hs]
        gc = each(lambda x: _seg_cumsum(x, rowmod, reverse=False), g)
        grev = each(lambda x: _seg_cumsum(x, rowmod, reverse=True) - x, g)
        decay = each(lambda x: jnp.exp(jnp.where(low_incl, gc_diff(x), -jnp.inf)), gc)
        kb = each(lambda a, b: a * b, k, beta)
        lmat = each(lambda a, b, dc: jnp.where(strict, _mm_nt(a, b) * dc, 0.0), kb, k, decay)
        egc = each(jnp.exp, gc)
        rhs = each(lambda a, b, c_, e: jnp.concatenate([a * b, c_ * e], axis=1), v, beta, kb, egc)

        xc = each(compact, lmat)
        tinv = each(lambda x: eye_c - x, xc)
        xs = each(split_expand, xc)
        level = 0
        p = 1
        while 2 * p < CHUNK:
            if level < 2:
                xc = each(lambda s4: dot(s4[0], s4[2]) + (dot(s4[0], s4[3]) + dot(s4[1], s4[2])), xs)
                xs = each(split_expand, xc)
                ts = each(_split2, tinv)
                tinv = each(lambda t, t2, s4: t + (dot(t2[0], s4[2]) + (dot(t2[0], s4[3]) + dot(t2[1], s4[2]))),
                            tinv, ts, xs)
            else:
                xh = each(lambda s4: dot(s4[0], s4[2]).astype(BF16), xs)
                xs = each(lambda x: (x, None, expand(x), None), xh)
                tinv = each(lambda t, s4: t + dot(t.astype(BF16), s4[2]), tinv, xs)
            level += 1
            p *= 2
        ts = each(split_expand, tinv)
        rs = each(_split2, rhs)
        sol = each(lambda t4, r2: dot(t4[2], r2[0]) + (dot(t4[2], r2[1]) + dot(t4[3], r2[0])), ts, rs)

        qk = each(lambda a, b, dc: jnp.where(low_incl, _mm_nt(a, b) * dc, 0.0), q, k, decay)
        q_dec = each(lambda a, e: a * e, q, egc)
        k_dec = each(lambda a, r: a * jnp.exp(r), k, grev)
        g_last = each(lambda a, r: jnp.exp(a + r), gc, grev)

        v_news = [[] for _ in hs]
        o_inter = [[] for _ in hs]
        s = [s_ref[h] for h in hs] if chained else None
        for c in range(nch):
            rc = slice(c * CHUNK, (c + 1) * CHUNK)
            if not chained:
                s = [s0_ref[c, h] for h in hs]
            v_new = each(lambda so, st: so[rc, :hd] - _mm(so[rc, hd:], st), sol, s)
            o_c = each(lambda qd, st: _mm(qd[rc], st), q_dec, s)
            s = each(lambda st, gl, kd, vn: st * gl[c * CHUNK:c * CHUNK + 1, :] + _mm_tn(kd[rc], vn),
                     s, g_last, k_dec, v_new)
            for j, h in enumerate(hs):
                v_news[j].append(v_new[j])
                o_inter[j].append(o_c[j])
                if not chained:
                    sout_ref[c, h] = s[j]
        if chained:
            for j, h in enumerate(hs):
                s_ref[h] = s[j]

            @pl.when(i == pl.num_programs(1) - 1)
            def _(s=s, hs=hs):
                for j, h in enumerate(hs):
                    sout_ref[h] = s[j]

        o = each(lambda oi, m, vn: jnp.concatenate(oi, axis=0) + _mm(m, jnp.concatenate(vn, axis=0)),
                 o_inter, qk, v_news)
        for j, h in enumerate(hs):
            o_ref[:, h * hd:(h + 1) * hd] = _rms(o[j], go_ref[...]) * _silu(z_ref[:, h * hd:(h + 1) * hd])


def _gdn_delta(act, bg, zsrc, zcol, n_seq, t, s0, g_o, n_heads, out_rows=None):
    d = act.shape[1] // 3
    hd = d // n_heads
    tb = 4 * CHUNK
    chained = t > CHUNK
    if chained:
        assert t % tb == 0
        grid = (n_seq, t // tb)
        rows = lambda b, i: b * (t // tb) + i
        sspec = pl.BlockSpec((None, n_heads, hd, hd), lambda b, i: (b, 0, 0, 0))
    else:
        assert t == CHUNK and n_seq % (tb // CHUNK) == 0
        grid = (n_seq * t // tb, 1)
        rows = lambda b, i: b
        sspec = pl.BlockSpec((tb // CHUNK, n_heads, hd, hd), lambda b, i: (b, 0, 0, 0))
    return pl.pallas_call(
        functools.partial(_gdn_delta_kernel, n_heads=n_heads, chained=chained, head_group=4),
        grid=grid,
        in_specs=[pl.BlockSpec((tb, 3 * d), lambda b, i: (rows(b, i), 0)),
                  pl.BlockSpec((tb, d), lambda b, i: (rows(b, i), zcol)),
                  pl.BlockSpec((tb, LANES), lambda b, i: (rows(b, i), 0)),
                  sspec,
                  pl.BlockSpec((1, hd), lambda b, i: (0, 0))],
        out_specs=[pl.BlockSpec((tb, d), lambda b, i: (rows(b, i), 0)), sspec],
        out_shape=[jax.ShapeDtypeStruct((out_rows or n_seq * t, d), F32),
                   jax.ShapeDtypeStruct((n_seq, n_heads, hd, hd), F32)],
        scratch_shapes=[pltpu.VMEM((n_heads, hd, hd), F32)],
        compiler_params=_cp("parallel", "arbitrary"),
        name="gdn_delta",
    )(act, zsrc, bg, s0, g_o.reshape(1, hd))


def _rglru_kernel(h_ref, g_ref, win_ref, cw_ref, cb_ref, wgate_ref, bra_ref, bri_ref, lam_ref, wout_ref,
                  hist_ref, h0_ref, oin_ref, o_ref, hlast_ref, xtail_ref, cx_ref, ch_ref, *, n_blocks):
    del oin_ref
    i = pl.program_id(1)
    tb, d = h_ref.shape
    cw = lam_ref.shape[1]
    blk = cw // n_blocks

    @pl.when(i == 0)
    def _():
        cx_ref[...] = hist_ref[...]
        ch_ref[...] = h0_ref[...]

    h = h_ref[...]
    proj = jnp.dot(_rms(h, g_ref[...]).astype(BF16), win_ref[...], preferred_element_type=F32)
    y_br = proj[:, :cw]
    x_br = proj[:, cw:]
    xc = _dwconv(x_br, cx_ref[...], cw_ref[...]) + cb_ref[...]
    r_parts, i_parts = [], []
    for n in range(n_blocks):
        ri = jnp.dot(xc[:, n * blk:(n + 1) * blk].astype(BF16), wgate_ref[n], preferred_element_type=F32)
        r_parts.append(ri[:, :blk])
        i_parts.append(ri[:, blk:])
    r = _sigmoid(jnp.concatenate(r_parts, axis=1) + bra_ref[...])
    ig = _sigmoid(jnp.concatenate(i_parts, axis=1) + bri_ref[...])
    log_a = -LRU_C * r * _softplus(-lam_ref[...])
    a = jnp.exp(log_a)
    th = jnp.tanh(log_a)
    b = jnp.sqrt(-2.0 * th / (1.0 - th)) * (ig * xc)
    row = lax.broadcasted_iota(jnp.int32, (tb, cw), 0)
    h_prev = ch_ref[SUBLANES - 1:SUBLANES, :]
    b = b + jnp.where(row == 0, a * h_prev, 0.0)
    step = 1
    while step < tb:
        a_s = jnp.where(row >= step, pltpu.roll(a, step, axis=0), 1.0)
        b_s = jnp.where(row >= step, pltpu.roll(b, step, axis=0), 0.0)
        b = a * b_s + b
        a = a * a_s
        step *= 2
    hs = b
    cx_ref[...] = x_br[tb - SUBLANES:, :]
    ch_ref[...] = hs[tb - SUBLANES:, :]

    @pl.when(i == pl.num_programs(1) - 1)
    def _():
        hlast_ref[...] = hs[tb - SUBLANES:, :]
        xtail_ref[...] = x_br[tb - SUBLANES:, :]

    o_ref[...] = h + jnp.dot((hs * _gelu(y_br)).astype(BF16), wout_ref[...], preferred_element_type=F32)


def _rglru(h, h_out, row0, n_seq, t, tb, g, w_in, conv_w, conv_b, wgate, b_ra, b_ri, lam, w_out, hist8, h08):
    m, d = h.shape
    cw = lam.shape[0]
    nb = wgate.shape[0]
    nt = t // tb
    r0 = row0 // tb
    vec = lambda x: x.reshape(1, -1)
    cst = lambda shape: pl.BlockSpec(shape, lambda b, i: (0,) * len(shape))
    in_specs = [pl.BlockSpec((tb, d), lambda b, i: (r0 + b * nt + i, 0)),
                cst((1, d)), cst((d, 2 * cw)), cst((CONV_W, cw)), cst((1, cw)),
                cst(wgate.shape), cst((1, cw)), cst((1, cw)), cst((1, cw)), cst((cw, d)),
                pl.BlockSpec((None, SUBLANES, cw), lambda b, i: (b, 0, 0)),
                pl.BlockSpec((None, SUBLANES, cw), lambda b, i: (b, 0, 0))]
    args = [h, vec(g), w_in, conv_w, vec(conv_b), wgate, vec(b_ra), vec(b_ri), vec(lam), w_out, hist8, h08]
    aliases = {}
    if h_out is not None:
        in_specs.append(pl.BlockSpec(memory_space=pl.ANY))
        args.append(h_out)
        aliases = {len(args) - 1: 0}
    kern = functools.partial(_rglru_kernel, n_blocks=nb)
    if h_out is None:
        kern = lambda *refs: _rglru_kernel(*refs[:12], None, *refs[12:], n_blocks=nb)
    return pl.pallas_call(
        kern,
        grid=(n_seq, nt),
        in_specs=in_specs,
        out_specs=[pl.BlockSpec((tb, d), lambda b, i: (r0 + b * nt + i, 0)),
                   pl.BlockSpec((None, SUBLANES, cw), lambda b, i: (b, 0, 0)),
                   pl.BlockSpec((None, SUBLANES, cw), lambda b, i: (b, 0, 0))],
        out_shape=[jax.ShapeDtypeStruct((m, d), F32),
                   jax.ShapeDtypeStruct((n_seq, SUBLANES, cw), F32),
                   jax.ShapeDtypeStruct((n_seq, SUBLANES, cw), F32)],
        scratch_shapes=[pltpu.VMEM((SUBLANES, cw), F32), pltpu.VMEM((SUBLANES, cw), F32)],
        input_output_aliases=aliases,
        compiler_params=_cp("parallel", "arbitrary"),
        name="rglru",
    )(*args)


def _gmlp_kernel(h_ref, g_ref, win_ref, bin_ref, lng_ref, lnb_ref, ws_ref, bs_ref, wout_ref, o_ref, vn_ref,
                 *, n_groups, first_sample_tile, lc_prompt, lc_sample):
    i = pl.program_id(0)
    tm, d = h_ref.shape
    half = lng_ref.shape[1]
    gw = half // n_groups
    h = h_ref[...]
    act = _gelu(jnp.dot(_rms(h, g_ref[...]).astype(BF16), win_ref[...], preferred_element_type=F32) + bin_ref[...])
    u = act[:, :half]
    v = act[:, half:]
    vc = v - jnp.mean(v, axis=-1, keepdims=True)
    vn = vc * lax.rsqrt(jnp.mean(vc * vc, axis=-1, keepdims=True) + EPS) * lng_ref[...] + lnb_ref[...]
    vn_ref[...] = vn
    shift = jnp.where(i >= first_sample_tile, lc_sample.bit_length() - 1, lc_prompt.bit_length() - 1)
    ri = lax.broadcasted_iota(jnp.int32, (tm, tm), 0)
    ci = lax.broadcasted_iota(jnp.int32, (tm, tm), 1)
    keep = (lax.shift_right_logical(ri, shift) == lax.shift_right_logical(ci, shift)) & (ri >= ci)
    vnb = vn.astype(BF16)
    parts = []
    for gi in range(n_groups):
        ws = jnp.where(keep, ws_ref[0, gi], 0.0).astype(BF16)
        sg = jnp.dot(ws, vnb[:, gi * gw:(gi + 1) * gw], preferred_element_type=F32)
        parts.append(u[:, gi * gw:(gi + 1) * gw] * (sg + bs_ref[0, :, gi * gw:(gi + 1) * gw]))
    gated = jnp.concatenate(parts, axis=1).astype(BF16)
    o_ref[...] = h + jnp.dot(gated, wout_ref[...], preferred_element_type=F32)


def _gmlp(h, m_p, t_s, g, w_in, b_in, ln_g, ln_b, w_s, b_s, w_out):
    m, d = h.shape
    half = ln_g.shape[0]
    ng = w_s.shape[0]
    gw = half // ng
    tm = 256
    lc_p, lc_s = D_CHUNK, min(t_s, D_CHUNK)
    assert m_p % tm == 0 and (m - m_p) % tm == 0 and tm % lc_p == 0 and tm % lc_s == 0

    def tiled(lc):
        wt = jnp.tile(w_s[:, :lc, :lc], (1, tm // lc, tm // lc))
        bt = jnp.repeat(jnp.tile(b_s[:, :lc].T, (tm // lc, 1)), gw, axis=1)
        return wt, bt

    wt_p, bt_p = tiled(lc_p)
    wt_s, bt_s = tiled(lc_s)
    ws_all = jnp.stack([wt_p, wt_s])
    bs_all = jnp.stack([bt_p, bt_s])
    fst = m_p // tm
    vec = lambda x: x.reshape(1, -1)
    cst = lambda shape: pl.BlockSpec(shape, lambda i: (0,) * len(shape))
    return pl.pallas_call(
        functools.partial(_gmlp_kernel, n_groups=ng, first_sample_tile=fst, lc_prompt=lc_p, lc_sample=lc_s),
        grid=(m // tm,),
        in_specs=[pl.BlockSpec((tm, d), lambda i: (i, 0)),
                  cst((1, d)), cst((d, 2 * half)), cst((1, 2 * half)), cst((1, half)), cst((1, half)),
                  pl.BlockSpec((1, ng, tm, tm), lambda i: (i // fst, 0, 0, 0)),
                  pl.BlockSpec((1, tm, half), lambda i: (i // fst, 0, 0)),
                  cst((half, d))],
        out_specs=[pl.BlockSpec((tm, d), lambda i: (i, 0)),
                   pl.BlockSpec((tm, half), lambda i: (jnp.maximum(i - fst, 0), 0))],
        out_shape=[jax.ShapeDtypeStruct((m, d), F32),
                   jax.ShapeDtypeStruct((m - m_p, half), F32)],
        compiler_params=_cp("arbitrary"),
        name="gmlp",
    )(h, vec(g), w_in, vec(b_in), vec(ln_g), vec(ln_b), ws_all, bs_all, w_out)


def _pad_rows8(x3):
    n, r, c = x3.shape
    return jnp.concatenate([jnp.zeros((n, SUBLANES - r, c), x3.dtype), x3], axis=1)


def kernel(x_prompt, x_sample, p_prompt, p_sample, cache_k_a, cache_v_a, state_s_b, state_conv_b, state_h_c, state_conv_c, g_mix, g_ffn, g_ple, g_final, w_in_a, g_q_a, g_k_a, rel_a, w_out_a, w_in_b, conv_b, a_log_b, dt_bias_b, g_o_b, w_out_b, w_in_c, conv_c, conv_bias_c, w_ra_c, b_ra_c, w_ri_c, b_ri_c, lam_c, w_out_c, w_in_d, b_in_d, ln_g_d, ln_b_d, w_s_d, b_s_d, w_out_d, w_gate_ff, w_up_ff, w_down_ff, w_router, w_gate_e, w_up_e, w_down_e, w_ple, w_ple_gate):
    n_b, t, d = x_prompt.shape
    n_s, t_s, _ = x_sample.shape
    depth = p_prompt.shape[0]
    pd = p_prompt.shape[-1]
    m_p, m_s = n_b * t, n_s * t_s
    m = m_p + m_s
    a_heads = d // g_q_a.shape[-1]
    b_heads = a_log_b.shape[-1]
    n_exp = w_router.shape[-1]
    bf = lambda x: x.astype(BF16)

    h = jnp.concatenate([x_prompt.reshape(m_p, d), x_sample.reshape(m_s, d)], axis=0)
    p_all = jnp.concatenate([p_prompt.reshape(depth, m_p, pd), p_sample.reshape(depth, m_s, pd)], axis=1)

    outs = {name: [] for name in ("k_a_p", "v_a_p", "k_a_s", "v_a_s", "s_b_p", "conv_b_p", "s_b_s", "conv_b_s",
                                  "h_c_p", "conv_c_p", "h_c_s", "conv_c_s", "v_d_s")}

    def tail_rows(x2, n_seq, tt, row0, cols):
        first = row0 + tt - (CONV_W - 1)
        rows = [x2[first + k:row0 + n_seq * tt:tt, :cols] for k in range(CONV_W - 1)]
        return jnp.stack(rows, axis=1)

    y = None
    for li in range(depth):
        kind, c = li % 4, li // 4
        if kind == 0:
            qkv = _proj(h, g_mix[li], bf(w_in_a[c]), 1024)
            w_keep = cache_k_a.shape[2]
            o, kn_p, kn_s = _attention(qkv, n_b, t, n_s, t_s,
                                       cache_k_a[c].reshape(n_s, w_keep, d), cache_v_a[c].reshape(n_s, w_keep, d),
                                       rel_a[c], g_q_a[c], g_k_a[c], a_heads)
            hd = d // a_heads
            keep = min(A_PREV_CHUNKS * CHUNK, t)
            v_p = jnp.stack([qkv[(b + 1) * t - keep:(b + 1) * t, 2 * d:] for b in range(n_b)])
            outs["k_a_p"].append(kn_p.reshape(n_b, keep, a_heads, hd))
            outs["v_a_p"].append(v_p.reshape(n_b, keep, a_heads, hd))
            outs["k_a_s"].append(kn_s.reshape(n_s, t_s, a_heads, hd))
            outs["v_a_s"].append(qkv[m_p:, 2 * d:].reshape(n_s, t_s, a_heads, hd))
            h = _outproj_res(h, o, bf(w_out_a[c]))
        elif kind == 1:
            w_in = w_in_b[c]
            nq = 3 * d
            w_main = bf(w_in[:, :nq + d])
            w_aux = bf(jnp.pad(w_in[:, nq + d:], ((0, 0), (0, LANES - 2 * b_heads))))
            proj, aux = _proj(h, g_mix[li], w_main, 1024, w_aux)
            pvec = jnp.zeros((2, LANES), F32)
            pvec = pvec.at[0, b_heads:2 * b_heads].set(a_log_b[c]).at[1, b_heads:2 * b_heads].set(dt_bias_b[c])
            hd = d // b_heads
            act_p, bg_p = _gdn_act(proj, aux, 0, n_b, t, 256, jnp.zeros((n_b, SUBLANES, nq), F32),
                                   conv_b[c], pvec, b_heads)
            o_p, s_p = _gdn_delta(act_p, bg_p, proj, 3, n_b, t,
                                  jnp.zeros((n_b, b_heads, hd, hd), F32), g_o_b[c], b_heads, out_rows=m)
            act_s, bg_s = _gdn_act(proj, aux, m_p, n_s, t_s, t_s, _pad_rows8(state_conv_b[c]),
                                   conv_b[c], pvec, b_heads)
            tp = -(-t_s // CHUNK) * CHUNK
            padt = lambda x2: jnp.pad(x2.reshape(n_s, t_s, -1), ((0, 0), (0, tp - t_s), (0, 0))).reshape(n_s * tp, -1)
            z_s = padt(proj[m_p:, nq:nq + d])
            o_s, s_s = _gdn_delta(padt(act_s), padt(bg_s), z_s, 0, n_s, tp, state_s_b[c], g_o_b[c], b_heads)
            o_s = o_s.reshape(n_s, tp, d)[:, :t_s].reshape(m_s, d)
            outs["s_b_p"].append(s_p)
            outs["s_b_s"].append(s_s)
            outs["conv_b_p"].append(tail_rows(proj, n_b, t, 0, nq))
            outs["conv_b_s"].append(tail_rows(proj, n_s, t_s, m_p, nq))
            h = _outproj_res(h, lax.dynamic_update_slice(o_p, o_s, (m_p, 0)), bf(w_out_b[c]))
        elif kind == 2:
            cw = lam_c.shape[-1]
            wgate = bf(jnp.concatenate([w_ra_c[c], w_ri_c[c]], axis=-1))
            common = (g_mix[li], bf(w_in_c[c]), conv_c[c], conv_bias_c[c], wgate, b_ra_c[c], b_ri_c[c],
                      lam_c[c], bf(w_out_c[c]))
            zeros8 = jnp.zeros((n_b, SUBLANES, cw), F32)
            h_new, hl_p, xt_p = _rglru(h, None, 0, n_b, t, 256, *common, zeros8, zeros8)
            h_new, hl_s, xt_s = _rglru(h, h_new, m_p, n_s, t_s, t_s, *common,
                                       _pad_rows8(state_conv_c[c]), _pad_rows8(state_h_c[c][:, None, :]))
            h = h_new
            outs["h_c_p"].append(hl_p[:, SUBLANES - 1])
            outs["h_c_s"].append(hl_s[:, SUBLANES - 1])
            outs["conv_c_p"].append(xt_p[:, SUBLANES - (CONV_W - 1):])
            outs["conv_c_s"].append(xt_s[:, SUBLANES - (CONV_W - 1):])
        else:
            h, vn_s = _gmlp(h, m_p, t_s, g_mix[li], bf(w_in_d[c]), b_in_d[c], ln_g_d[c], ln_b_d[c],
                            w_s_d[c], b_s_d[c], bf(w_out_d[c]))
            outs["v_d_s"].append(vn_s.reshape(n_s, t_s, -1))

        j = li // 2
        if li % 2 == 0:
            h = _ffn_dense(h, g_ffn[li], bf(w_gate_ff[j]), bf(w_up_ff[j]), bf(w_down_ff[j]))
        else:
            wr = jnp.pad(w_router[j], ((0, 0), (0, LANES - n_exp)))
            h = _moe(h, g_ffn[li], wr, bf(w_gate_e[j]), bf(w_up_e[j]), bf(w_down_e[j]))
        last = li == depth - 1
        res = _ple(h, p_all[li], g_ple[li], bf(w_ple[li]), bf(w_ple_gate[li]), g_final if last else None)
        if last:
            y = res
        else:
            h = res

    st = {name: jnp.stack(rows) for name, rows in outs.items()}
    return (y[:m_p].reshape(n_b, t, d), y[m_p:].reshape(n_s, t_s, d),
            st["k_a_p"], st["v_a_p"], st["k_a_s"], st["v_a_s"],
            st["s_b_p"], st["conv_b_p"], st["s_b_s"], st["conv_b_s"],
            st["h_c_p"], st["conv_c_p"], st["h_c_s"], st["conv_c_s"],
            st["v_d_s"])
```

```python
import functools

import jax
import jax.numpy as jnp
from jax import lax
from jax.experimental import pallas as pl
from jax.experimental.pallas import tpu as pltpu

F32 = jnp.float32
BF16 = jnp.bfloat16
EPS = 1e-6
CHUNK = 64
CONV_W = 4
A_REL_CLIP = 128
A_PREV_CHUNKS = 8
LRU_C = 8.0
D_CHUNK = 128
TOP_K = 2
LANES = 128
SUBLANES = 8
VMEM_LIMIT = 48 * 1024 * 1024


def _cp(*sem, vmem=VMEM_LIMIT):
    return pltpu.CompilerParams(dimension_semantics=sem, vmem_limit_bytes=vmem)


def _tile_m(m):
    for t in (1024, 512, 256):
        if m % t == 0:
            return t
    raise ValueError(f"token count {m} must be a multiple of 256")


def _mm(a, b):
    return jnp.dot(a.astype(BF16), b.astype(BF16), preferred_element_type=F32)


def _mm_nt(a, b):
    return lax.dot_general(a.astype(BF16), b.astype(BF16), (((1,), (1,)), ((), ())),
                           preferred_element_type=F32)


def _mm_tn(a, b):
    return lax.dot_general(a.astype(BF16), b.astype(BF16), (((0,), (0,)), ((), ())),
                           preferred_element_type=F32)


def _split2(x):
    hi = x.astype(BF16)
    lo = (x - hi.astype(F32)).astype(BF16)
    return hi, lo


def _split3(x):
    hi = x.astype(BF16)
    r = x - hi.astype(F32)
    mid = r.astype(BF16)
    lo = (r - mid.astype(F32)).astype(BF16)
    return hi, mid, lo


def _mm3(a, b):
    ah, al = _split2(a)
    bh, bl = _split2(b)
    d = functools.partial(jnp.dot, preferred_element_type=F32)
    return d(ah, bh) + (d(ah, bl) + d(al, bh))


def _rms(x, g):
    return x * lax.rsqrt(jnp.mean(x * x, axis=-1, keepdims=True) + EPS) * g


def _sigmoid(x):
    return 0.5 * (1.0 + jnp.tanh(0.5 * x))


def _silu(x):
    return x * _sigmoid(x)


def _softplus(x):
    return jnp.maximum(x, 0.0) + jnp.log1p(jnp.exp(-jnp.abs(x)))


def _gelu(x):
    c = 0.7978845608028654
    return x * (0.5 * (1.0 + jnp.tanh(c * (x + 0.044715 * (x * x * x)))))


def _shift_rows(x, prev8, s):
    r = pltpu.roll(x, s, axis=0)
    pr = pltpu.roll(prev8, s, axis=0)
    row8 = lax.broadcasted_iota(jnp.int32, pr.shape, 0)
    top = jnp.where(row8 < s, pr, r[0:SUBLANES])
    return jnp.concatenate([top, r[SUBLANES:]], axis=0)


def _dwconv(x, prev8, w):
    y = None
    for k in range(CONV_W):
        s = CONV_W - 1 - k
        xs = x if s == 0 else _shift_rows(x, prev8, s)
        term = xs * w[k:k + 1]
        y = term if y is None else y + term
    return y


def _proj_kernel(*refs, has_aux):
    if has_aux:
        x_ref, g_ref, w_ref, waux_ref, o_ref, oaux_ref, hn_ref = refs
    else:
        x_ref, g_ref, w_ref, o_ref, hn_ref = refs

    @pl.when(pl.program_id(1) == 0)
    def _():
        hn = _rms(x_ref[...], g_ref[...]).astype(BF16)
        hn_ref[...] = hn
        if has_aux:
            oaux_ref[...] = jnp.dot(hn, waux_ref[...], preferred_element_type=F32)

    o_ref[...] = jnp.dot(hn_ref[...], w_ref[...], preferred_element_type=F32)


def _proj(h, g, w, tn, waux=None):
    m, k = h.shape
    n = w.shape[1]
    tm = _tile_m(m)
    in_specs = [pl.BlockSpec((tm, k), lambda i, j: (i, 0)),
                pl.BlockSpec((1, k), lambda i, j: (0, 0)),
                pl.BlockSpec((k, tn), lambda i, j: (0, j))]
    out_shape = [jax.ShapeDtypeStruct((m, n), F32)]
    out_specs = [pl.BlockSpec((tm, tn), lambda i, j: (i, j))]
    args = [h, g.reshape(1, k), w]
    if waux is not None:
        na = waux.shape[1]
        in_specs.append(pl.BlockSpec((k, na), lambda i, j: (0, 0)))
        out_shape.append(jax.ShapeDtypeStruct((m, na), F32))
        out_specs.append(pl.BlockSpec((tm, na), lambda i, j: (i, 0)))
        args.append(waux)
    res = pl.pallas_call(
        functools.partial(_proj_kernel, has_aux=waux is not None),
        grid=(m // tm, n // tn),
        in_specs=in_specs, out_specs=out_specs, out_shape=out_shape,
        scratch_shapes=[pltpu.VMEM((tm, k), BF16)],
        compiler_params=_cp("parallel", "arbitrary"),
        name="proj",
    )(*args)
    return res if waux is not None else res[0]


def _outproj_kernel(h_ref, o_ref, w_ref, out_ref):
    out_ref[...] = h_ref[...] + jnp.dot(o_ref[...].astype(BF16), w_ref[...], preferred_element_type=F32)


def _outproj_res(h, o, w):
    m, d = h.shape
    k = o.shape[1]
    tm = _tile_m(m)
    return pl.pallas_call(
        _outproj_kernel,
        grid=(m // tm,),
        in_specs=[pl.BlockSpec((tm, d), lambda i: (i, 0)),
                  pl.BlockSpec((tm, k), lambda i: (i, 0)),
                  pl.BlockSpec((k, d), lambda i: (0, 0))],
        out_specs=pl.BlockSpec((tm, d), lambda i: (i, 0)),
        out_shape=jax.ShapeDtypeStruct((m, d), F32),
        compiler_params=_cp("parallel"),
        name="outproj_res",
    )(h, o, w)


def _ffn_kernel(x_ref, g_ref, wg_ref, wu_ref, wd_ref, o_ref, hn_ref):
    @pl.when(pl.program_id(1) == 0)
    def _():
        x = x_ref[...]
        hn_ref[...] = _rms(x, g_ref[...]).astype(BF16)
        o_ref[...] = x

    hn = hn_ref[...]
    gg = jnp.dot(hn, wg_ref[...], preferred_element_type=F32)
    uu = jnp.dot(hn, wu_ref[...], preferred_element_type=F32)
    hid = (_silu(gg) * uu).astype(BF16)
    o_ref[...] += jnp.dot(hid, wd_ref[...], preferred_element_type=F32)


def _ffn_dense(h, g, wg, wu, wd):
    m, d = h.shape
    f = wg.shape[1]
    tm = _tile_m(m)
    tf = 256
    return pl.pallas_call(
        _ffn_kernel,
        grid=(m // tm, f // tf),
        in_specs=[pl.BlockSpec((tm, d), lambda i, j: (i, 0)),
                  pl.BlockSpec((1, d), lambda i, j: (0, 0)),
                  pl.BlockSpec((d, tf), lambda i, j: (0, j)),
                  pl.BlockSpec((d, tf), lambda i, j: (0, j)),
                  pl.BlockSpec((tf, d), lambda i, j: (j, 0))],
        out_specs=pl.BlockSpec((tm, d), lambda i, j: (i, 0)),
        out_shape=jax.ShapeDtypeStruct((m, d), F32),
        scratch_shapes=[pltpu.VMEM((tm, d), BF16)],
        compiler_params=_cp("parallel", "arbitrary"),
        name="ffn_dense",
    )(h, g.reshape(1, d), wg, wu, wd)


def _moe_kernel(x_ref, g_ref, wr_ref, wg_ref, wu_ref, wd_ref, o_ref, hn_ref, gates_ref, rank_ref, rankt_ref,
                *, n_experts, cap):
    e = pl.program_id(1)
    n_sub, _, sub = rankt_ref.shape

    @pl.when(e == 0)
    def _():
        ri = lax.broadcasted_iota(jnp.int32, (sub, sub), 0)
        ci = lax.broadcasted_iota(jnp.int32, (sub, sub), 1)
        tril = jnp.where(ri >= ci, 1.0, 0.0).astype(BF16)
        lane = lax.broadcasted_iota(jnp.int32, (sub, LANES), 1)
        for s in range(n_sub):
            rs = slice(s * sub, (s + 1) * sub)
            x = x_ref[rs, :]
            hn = _rms(x, g_ref[...])
            hn_ref[rs, :] = hn.astype(BF16)
            o_ref[rs, :] = x
            logits = _mm3(hn, wr_ref[...])
            lg = jnp.where(lane < n_experts, logits, -jnp.inf)
            m1 = jnp.max(lg, axis=-1, keepdims=True)
            i1 = jnp.min(jnp.where(lg == m1, lane, LANES), axis=-1, keepdims=True)
            lg2 = jnp.where(lane == i1, -jnp.inf, lg)
            m2 = jnp.max(lg2, axis=-1, keepdims=True)
            i2 = jnp.min(jnp.where(lg2 == m2, lane, LANES), axis=-1, keepdims=True)
            e2 = jnp.exp(m2 - m1)
            w1 = 1.0 / (1.0 + e2)
            w2 = e2 / (1.0 + e2)
            gates_ref[rs, :] = jnp.where(lane == i1, w1, 0.0) + jnp.where(lane == i2, w2, 0.0)
            sel = jnp.where(lane == i1, 1.0, jnp.where(lane == i2, 1.0, 0.0))
            count = jnp.dot(tril, sel.astype(BF16), preferred_element_type=F32)
            rank = jnp.where(sel > 0.0, count - 1.0, -1.0)
            rank_ref[rs, :] = rank
            rankt_ref[s] = rank.T

    lane = lax.broadcasted_iota(jnp.int32, (sub, LANES), 1)
    row_id = lax.broadcasted_iota(jnp.int32, (cap, sub), 0).astype(F32)
    col_id = lax.broadcasted_iota(jnp.int32, (sub, cap), 1).astype(F32)
    for s in range(n_sub):
        rs = slice(s * sub, (s + 1) * sub)
        rank_row = rankt_ref[s, pl.ds(e, 1), :]
        n_routed = (jnp.max(rank_row) + 1.0).astype(jnp.int32)
        rank_col = jnp.sum(jnp.where(lane == e, rank_ref[rs, :], 0.0), axis=-1, keepdims=True)
        gate_col = jnp.sum(jnp.where(lane == e, gates_ref[rs, :], 0.0), axis=-1, keepdims=True)

        def chunk(cc, carry, rs=rs, rank_row=rank_row, rank_col=rank_col, gate_col=gate_col):
            base = (cc * cap).astype(F32)
            pick = jnp.where(rank_row == row_id + base, 1.0, 0.0).astype(BF16)
            xe = jnp.dot(pick, hn_ref[rs, :], preferred_element_type=F32).astype(BF16)
            gg = jnp.dot(xe, wg_ref[0], preferred_element_type=F32)
            uu = jnp.dot(xe, wu_ref[0], preferred_element_type=F32)
            hid = (_silu(gg) * uu).astype(BF16)
            ye = jnp.dot(hid, wd_ref[0], preferred_element_type=F32).astype(BF16)
            put = jnp.where(rank_col == col_id + base, gate_col, 0.0).astype(BF16)
            o_ref[rs, :] += jnp.dot(put, ye, preferred_element_type=F32)
            return carry

        lax.fori_loop(0, (n_routed + cap - 1) // cap, chunk, 0)


def _moe(h, g, wr, wg, wu, wd):
    m, d = h.shape
    ne, _, ef = wg.shape
    sub = 512
    tm = 2 * sub if m % (2 * sub) == 0 else sub
    cap = 160
    assert m % tm == 0
    return pl.pallas_call(
        functools.partial(_moe_kernel, n_experts=ne, cap=cap),
        grid=(m // tm, ne),
        in_specs=[pl.BlockSpec((tm, d), lambda i, e: (i, 0)),
                  pl.BlockSpec((1, d), lambda i, e: (0, 0)),
                  pl.BlockSpec((d, LANES), lambda i, e: (0, 0)),
                  pl.BlockSpec((1, d, ef), lambda i, e: (e, 0, 0)),
                  pl.BlockSpec((1, d, ef), lambda i, e: (e, 0, 0)),
                  pl.BlockSpec((1, ef, d), lambda i, e: (e, 0, 0))],
        out_specs=pl.BlockSpec((tm, d), lambda i, e: (i, 0)),
        out_shape=jax.ShapeDtypeStruct((m, d), F32),
        scratch_shapes=[pltpu.VMEM((tm, d), BF16), pltpu.VMEM((tm, LANES), F32),
                        pltpu.VMEM((tm, LANES), F32), pltpu.VMEM((tm // sub, LANES, sub), F32)],
        compiler_params=_cp("parallel", "arbitrary"),
        name="moe",
    )(h, g.reshape(1, d), wr, wg, wu, wd)


def _ple_kernel(*refs, final, n_prompt_tiles):
    if final:
        x_ref, pp_ref, ps_ref, g_ref, wp_ref, wg_ref, gf_ref, yp_ref, ys_ref = refs
    else:
        x_ref, pp_ref, ps_ref, g_ref, wp_ref, wg_ref, o_ref = refs
    i = pl.program_id(0)
    is_sample = i >= n_prompt_tiles
    x = x_ref[...]
    hn = _rms(x, g_ref[...]).astype(BF16)
    gate = _sigmoid(jnp.dot(hn, wg_ref[...], preferred_element_type=F32))
    p = jnp.where(is_sample, ps_ref[...], pp_ref[...])
    pp = jnp.dot(p.astype(BF16), wp_ref[...], preferred_element_type=F32)
    out = x + pp * gate
    if final:
        y = _rms(out, gf_ref[...])

        @pl.when(jnp.logical_not(is_sample))
        def _():
            yp_ref[...] = y

        @pl.when(is_sample)
        def _():
            ys_ref[...] = y
    else:
        o_ref[...] = out


def _ple(h, p_prompt, p_sample, g, wp, wgate, g_final=None):
    m, d = h.shape
    m_p, pd = p_prompt.shape
    m_s = p_sample.shape[0]
    tm = _tile_m(m)
    assert m_p % tm == 0 and m_s % tm == 0
    npt, nst = m_p // tm, m_s // tm
    final = g_final is not None
    in_specs = [pl.BlockSpec((tm, d), lambda i: (i, 0)),
                pl.BlockSpec((tm, pd), lambda i: (jnp.minimum(i, npt - 1), 0)),
                pl.BlockSpec((tm, pd), lambda i: (jnp.maximum(i - npt, 0), 0)),
                pl.BlockSpec((1, d), lambda i: (0, 0)),
                pl.BlockSpec((pd, d), lambda i: (0, 0)),
                pl.BlockSpec((d, d), lambda i: (0, 0))]
    args = [h, p_prompt, p_sample, g.reshape(1, d), wp, wgate]
    if final:
        in_specs.append(pl.BlockSpec((1, d), lambda i: (0, 0)))
        args.append(g_final.reshape(1, d))
        out_specs = [pl.BlockSpec((tm, d), lambda i: (jnp.minimum(i, npt - 1), 0)),
                     pl.BlockSpec((tm, d), lambda i: (jnp.maximum(i - npt, 0), 0))]
        out_shape = [jax.ShapeDtypeStruct((m_p, d), F32), jax.ShapeDtypeStruct((m_s, d), F32)]
    else:
        out_specs = pl.BlockSpec((tm, d), lambda i: (i, 0))
        out_shape = jax.ShapeDtypeStruct((m, d), F32)
    return pl.pallas_call(
        functools.partial(_ple_kernel, final=final, n_prompt_tiles=npt),
        grid=(npt + nst,),
        in_specs=in_specs, out_specs=out_specs, out_shape=out_shape,
        compiler_params=_cp("arbitrary"),
        name="ple",
    )(*args)


def _pair_rms(x, g, hd):
    lane = lax.broadcasted_iota(jnp.int32, x.shape, 1)
    lo = lane < hd
    x2 = x * x
    s0 = jnp.sum(jnp.where(lo, x2, 0.0), axis=-1, keepdims=True)
    s1 = jnp.sum(jnp.where(lo, 0.0, x2), axis=-1, keepdims=True)
    ms = jnp.where(lo, s0, s1) * (1.0 / hd)
    return x * lax.rsqrt(ms + EPS) * g


def _stack_heads(q, hd):
    lane = lax.broadcasted_iota(jnp.int32, q.shape, 1)
    lo = lane < hd
    return jnp.concatenate([jnp.where(lo, q, 0.0), jnp.where(lo, 0.0, q)], axis=0).astype(BF16)


def _unstack_heads(o2, hd):
    n = o2.shape[0] // 2
    lane = lax.broadcasted_iota(jnp.int32, (n, o2.shape[1]), 1)
    return jnp.where(lane < hd, o2[:n], o2[n:])


def _attend(q2, kw, vw, bias, valid):
    s = _mm_nt(q2, kw) + bias
    if valid is not None:
        s = jnp.where(valid, s, -jnp.inf)
    mx = jnp.max(s, axis=-1, keepdims=True)
    e = jnp.exp(s - mx)
    den = jnp.sum(e, axis=-1, keepdims=True)
    return jnp.dot(e.astype(BF16), vw, preferred_element_type=F32) / den


def _attn_prompt_kernel(q_ref, k_ref, v_ref, b_ref, gq_ref, gk_ref, o_ref, kn_ref, kwin_ref, vwin_ref,
                        *, hd, scale, reach, chunk_group):
    i = pl.program_id(2)
    tq = q_ref.shape[0]
    band = reach + CHUNK
    qn = _pair_rms(q_ref[...], gq_ref[...], hd) * scale
    knc = _pair_rms(k_ref[...], gk_ref[...], hd)

    @pl.when(i == pl.num_programs(2) - 1)
    def _():
        kn_ref[...] = knc

    @pl.when(i == 0)
    def _():
        kwin_ref[0:tq, :] = jnp.zeros((tq, 2 * hd), BF16)
        vwin_ref[0:tq, :] = jnp.zeros((tq, 2 * hd), BF16)

    @pl.when(i > 0)
    def _():
        kwin_ref[0:tq, :] = kwin_ref[tq:2 * tq, :]
        vwin_ref[0:tq, :] = vwin_ref[tq:2 * tq, :]

    kwin_ref[tq:2 * tq, :] = knc.astype(BF16)
    vwin_ref[tq:2 * tq, :] = v_ref[...].astype(BF16)
    bias = b_ref[0]
    col = lax.broadcasted_iota(jnp.int32, (1, band), 1)
    n_chunks = tq // CHUNK
    w0s = [tq - reach + jj * CHUNK for jj in range(n_chunks)]

    def attend(first_block, jjs):
        q2 = [_stack_heads(qn[jj * CHUNK:(jj + 1) * CHUNK], hd) for jj in jjs]
        s = [_mm_nt(x, kwin_ref[w0s[jj]:w0s[jj] + band, :]) + bias for x, jj in zip(q2, jjs)]
        if first_block:
            s = [jnp.where(col + w0s[jj] >= tq, x, -jnp.inf) for x, jj in zip(s, jjs)]
        mx = [jnp.max(x, axis=-1, keepdims=True) for x in s]
        e = [jnp.exp(x - m) for x, m in zip(s, mx)]
        den = [jnp.sum(x, axis=-1, keepdims=True) for x in e]
        o2 = [jnp.dot(x.astype(BF16), vwin_ref[w0s[jj]:w0s[jj] + band, :], preferred_element_type=F32)
              for x, jj in zip(e, jjs)]
        for x, dn, jj in zip(o2, den, jjs):
            o_ref[jj * CHUNK:(jj + 1) * CHUNK, :] = _unstack_heads(x / dn, hd)

    def attend_all(first_block):
        for j0 in range(0, n_chunks, chunk_group):
            attend(first_block, list(range(j0, j0 + chunk_group)))

    @pl.when(i == 0)
    def _():
        attend_all(True)

    @pl.when(i > 0)
    def _():
        attend_all(False)


def _attn_sample_kernel(q_ref, k_ref, v_ref, ck_ref, cv_ref, b_ref, gq_ref, gk_ref, oin_ref, o_ref, kn_ref,
                        *, hd, scale):
    del oin_ref
    pw = 2 * hd
    pairs = [slice(p * pw, (p + 1) * pw) for p in range(q_ref.shape[1] // pw)]
    qn = [_pair_rms(q_ref[:, c], gq_ref[...], hd) * scale for c in pairs]
    kn = [_pair_rms(k_ref[:, c], gk_ref[...], hd) for c in pairs]
    for c, x in zip(pairs, kn):
        kn_ref[:, c] = x
    kcat = [jnp.concatenate([ck_ref[:, c], x], axis=0).astype(BF16) for c, x in zip(pairs, kn)]
    vcat = [jnp.concatenate([cv_ref[:, c], v_ref[:, c]], axis=0).astype(BF16) for c in pairs]
    s = [_mm_nt(_stack_heads(x, hd), kk) + b_ref[p] for p, (x, kk) in enumerate(zip(qn, kcat))]
    mx = [jnp.max(x, axis=-1, keepdims=True) for x in s]
    e = [jnp.exp(x - m) for x, m in zip(s, mx)]
    den = [jnp.sum(x, axis=-1, keepdims=True) for x in e]
    o2 = [jnp.dot(x.astype(BF16), vv, preferred_element_type=F32) for x, vv in zip(e, vcat)]
    for c, x, dn in zip(pairs, o2, den):
        o_ref[:, c] = _unstack_heads(x / dn, hd)


def _rel_bias_pairs(table, q0, nq, nk):
    span = nq + nk - 1
    dist = jnp.arange(span) + (q0 - (nk - 1))
    line = table[:, jnp.clip(dist, -A_REL_CLIP, A_REL_CLIP) + A_REL_CLIP]
    flipped = line[:, ::-1]
    b = jnp.stack([flipped[:, nq - 1 - q:nq - 1 - q + nk] for q in range(nq)], axis=1)
    nh = b.shape[0]
    return b.reshape(nh // 2, 2 * nq, nk)


def _attention(qkv, n_b, t, n_s, t_s, cache_k, cache_v, rel, g_q, g_k, n_heads):
    m_tot, d3 = qkv.shape
    d = d3 // 3
    hd = d // n_heads
    pw = 2 * hd
    n_pairs = d // pw
    m_p = n_b * t
    reach = A_PREV_CHUNKS * CHUNK
    tq = reach
    assert pw == LANES and t % tq == 0 and t_s % 16 == 0
    nqb = t // tq
    scale = hd ** -0.5
    gq2 = jnp.tile(g_q, 2).reshape(1, pw)
    gk2 = jnp.tile(g_k, 2).reshape(1, pw)

    bias_p = _rel_bias_pairs(rel, reach, CHUNK, reach + CHUNK)
    kcol, vcol = n_pairs, 2 * n_pairs

    def rows(b, p, i):
        return b * nqb + i

    blk = (tq, pw)
    o_all, kn_p = pl.pallas_call(
        functools.partial(_attn_prompt_kernel, hd=hd, scale=scale, reach=reach, chunk_group=4),
        grid=(n_b, n_pairs, nqb),
        in_specs=[pl.BlockSpec(blk, lambda b, p, i: (rows(b, p, i), p)),
                  pl.BlockSpec(blk, lambda b, p, i: (rows(b, p, i), kcol + p)),
                  pl.BlockSpec(blk, lambda b, p, i: (rows(b, p, i), vcol + p)),
                  pl.BlockSpec((1, 2 * CHUNK, reach + CHUNK), lambda b, p, i: (p, 0, 0)),
                  pl.BlockSpec((1, pw), lambda b, p, i: (0, 0)),
                  pl.BlockSpec((1, pw), lambda b, p, i: (0, 0))],
        out_specs=[pl.BlockSpec(blk, lambda b, p, i: (rows(b, p, i), p)),
                   pl.BlockSpec(blk, lambda b, p, i: (b, p))],
        out_shape=[jax.ShapeDtypeStruct((m_tot, d), F32),
                   jax.ShapeDtypeStruct((n_b * tq, d), F32)],
        scratch_shapes=[pltpu.VMEM((2 * tq, pw), BF16), pltpu.VMEM((2 * tq, pw), BF16)],
        compiler_params=_cp("parallel", "parallel", "arbitrary"),
        name="attn_prompt",
    )(qkv, qkv, qkv, bias_p, gq2, gk2)

    w = cache_k.shape[1]
    bias_s = _rel_bias_pairs(rel, w, t_s, w + t_s)
    srow = m_p // t_s
    sblk = (t_s, d)
    o_all, kn_s = pl.pallas_call(
        functools.partial(_attn_sample_kernel, hd=hd, scale=scale),
        grid=(n_s,),
        in_specs=[pl.BlockSpec(sblk, lambda b: (srow + b, 0)),
                  pl.BlockSpec(sblk, lambda b: (srow + b, 1)),
                  pl.BlockSpec(sblk, lambda b: (srow + b, 2)),
                  pl.BlockSpec((None, w, d), lambda b: (b, 0, 0)),
                  pl.BlockSpec((None, w, d), lambda b: (b, 0, 0)),
                  pl.BlockSpec((n_pairs, 2 * t_s, w + t_s), lambda b: (0, 0, 0)),
                  pl.BlockSpec((1, pw), lambda b: (0, 0)),
                  pl.BlockSpec((1, pw), lambda b: (0, 0)),
                  pl.BlockSpec(memory_space=pl.ANY)],
        out_specs=[pl.BlockSpec(sblk, lambda b: (srow + b, 0)),
                   pl.BlockSpec(sblk, lambda b: (b, 0))],
        out_shape=[jax.ShapeDtypeStruct((m_tot, d), F32),
                   jax.ShapeDtypeStruct((n_s * t_s, d), F32)],
        input_output_aliases={8: 0},
        compiler_params=_cp("parallel"),
        name="attn_sample",
    )(qkv, qkv, qkv, cache_k, cache_v, bias_s, gq2, gk2, o_all)
    return o_all, kn_p, kn_s


def _gdn_act_kernel(xq_ref, xk_ref, xv_ref, pq_ref, pk_ref, pv_ref, hq_ref, hk_ref, hv_ref,
                    aux_ref, cw_ref, pvec_ref, o_ref, bg_ref, *, blocks_per_seq, n_heads, scale):
    i = pl.program_id(0)
    start = (i % blocks_per_seq) == 0
    d = xq_ref.shape[1]
    hd = d // n_heads
    parts = ((xq_ref, pq_ref, hq_ref), (xk_ref, pk_ref, hk_ref), (xv_ref, pv_ref, hv_ref))
    for c, (x_ref, p_ref, h_ref) in enumerate(parts):
        prev8 = jnp.where(start, h_ref[...], p_ref[...])
        a = _silu(_dwconv(x_ref[...], prev8, cw_ref[:, c * d:(c + 1) * d]))
        for hh in range(n_heads):
            ah = a[:, hh * hd:(hh + 1) * hd]
            if c < 2:
                ah = ah * lax.rsqrt(jnp.sum(ah * ah, axis=-1, keepdims=True) + EPS)
            if c == 0:
                ah = ah * scale
            o_ref[:, c * d + hh * hd:c * d + (hh + 1) * hd] = ah
    ax = aux_ref[...]
    lane = lax.broadcasted_iota(jnp.int32, ax.shape, 1)
    beta = _sigmoid(ax)
    gdec = -jnp.exp(pvec_ref[0:1, :]) * _softplus(ax + pvec_ref[1:2, :])
    bg_ref[...] = jnp.where(lane < n_heads, beta, jnp.where(lane < 2 * n_heads, gdec, 0.0))


def _gdn_act(proj, aux, row0, n_seq, t, tb, hist8, conv_w, pvec, n_heads):
    d = proj.shape[1] // 4
    hd = d // n_heads
    bps = t // tb
    r0 = row0 // tb
    p8 = tb // SUBLANES

    def xspec(c):
        return pl.BlockSpec((tb, d), lambda i, c=c: (r0 + i, c))

    def pspec(c):
        return pl.BlockSpec((SUBLANES, d), lambda i, c=c: (jnp.maximum((r0 + i) * p8 - 1, 0), c))

    def hspec(c):
        return pl.BlockSpec((None, SUBLANES, d), lambda i, c=c: (i // bps, 0, c))

    m = n_seq * t
    return pl.pallas_call(
        functools.partial(_gdn_act_kernel, blocks_per_seq=bps, n_heads=n_heads, scale=hd ** -0.5),
        grid=(m // tb,),
        in_specs=[xspec(0), xspec(1), xspec(2), pspec(0), pspec(1), pspec(2), hspec(0), hspec(1), hspec(2),
                  pl.BlockSpec((tb, LANES), lambda i: (r0 + i, 0)),
                  pl.BlockSpec((CONV_W, 3 * d), lambda i: (0, 0)),
                  pl.BlockSpec((2, LANES), lambda i: (0, 0))],
        out_specs=[pl.BlockSpec((tb, 3 * d), lambda i: (i, 0)),
                   pl.BlockSpec((tb, LANES), lambda i: (i, 0))],
        out_shape=[jax.ShapeDtypeStruct((m, 3 * d), F32), jax.ShapeDtypeStruct((m, LANES), F32)],
        compiler_params=_cp("parallel"),
        name="gdn_act",
    )(proj, proj, proj, proj, proj, proj, hist8, hist8, hist8, aux, conv_w, pvec)


def _seg_cumsum(x, rowmod, reverse):
    n = x.shape[0]
    d = 1
    while d < CHUNK:
        if reverse:
            sh = pltpu.roll(x, n - d, axis=0)
            x = x + jnp.where(rowmod < CHUNK - d, sh, 0.0)
        else:
            sh = pltpu.roll(x, d, axis=0)
            x = x + jnp.where(rowmod >= d, sh, 0.0)
        d *= 2
    return x


def _gdn_delta_kernel(act_ref, z_ref, bg_ref, s0_ref, go_ref, o_ref, sout_ref, s_ref, *, n_heads, chained, head_group):
    i = pl.program_id(1)
    tb, d = z_ref.shape
    hd = d // n_heads
    nch = tb // CHUNK
    dot = functools.partial(jnp.dot, preferred_element_type=F32)

    if chained:
        @pl.when(i == 0)
        def _():
            s_ref[...] = s0_ref[...]

    bg = bg_ref[...]
    lane = lax.broadcasted_iota(jnp.int32, (tb, hd), 1)
    rowmod = lax.broadcasted_iota(jnp.int32, (tb, hd), 0) & (CHUNK - 1)
    ri = lax.broadcasted_iota(jnp.int32, (tb, tb), 0)
    ci = lax.broadcasted_iota(jnp.int32, (tb, tb), 1)
    same = (ri >> 6) == (ci >> 6)
    low_incl = same & (ri >= ci)
    strict = same & (ri > ci)
    eye_c = (lax.broadcasted_iota(jnp.int32, (CHUNK, tb), 0)
             == (lax.broadcasted_iota(jnp.int32, (CHUNK, tb), 1) & (CHUNK - 1))).astype(F32)
    def compact(x):
        acc = x[0:CHUNK]
        for c in range(1, nch):
            acc = acc + x[c * CHUNK:(c + 1) * CHUNK]
        return acc

    def expand(xc):
        return jnp.where(same, jnp.concatenate([xc] * nch, axis=0), jnp.zeros((), xc.dtype))

    def split_expand(xc):
        hi, lo = _split2(xc)
        return hi, lo, expand(hi), expand(lo)

    def each(fn, *lists):
        return [fn(*a) for a in zip(*lists)]

    def gc_diff(gc):
        gh, gm, gl = (piece.astype(F32) for piece in _split3(gc))
        lhs = jnp.where(lane == 0, gh, jnp.where(lane == 1, gm, jnp.where(lane == 2, gl,
                                                                          jnp.where(lane < 6, 1.0, 0.0))))
        rhs_t = jnp.where(lane < 3, 1.0, jnp.where(lane == 3, -gh, jnp.where(lane == 4, -gm,
                                                                             jnp.where(lane == 5, -gl, 0.0))))
        return _mm_nt(lhs, rhs_t)

    for h0 in range(0, n_heads, head_group):
        hs = list(range(h0, h0 + head_group))
        q = [act_ref[:, h * hd:(h + 1) * hd] for h in hs]
        k = [act_ref[:, d + h * hd:d + (h + 1) * hd] for h in hs]
        v = [act_ref[:, 2 * d + h * hd:2 * d + (h + 1) * hd] for h in hs]
        beta = [jnp.broadcast_to(jnp.sum(jnp.where(lane == h, bg, 0.0), axis=-1, keepdims=True), (tb, hd))
                for h in hs]
        g = [jnp.broadcast_to(jnp.sum(jnp.where(lane == n_heads + h, bg, 0.0), axis=-1, keepdims=True), (tb, hd))
             for h in hs]
        gc = each(lambda x: _seg_cumsum(x, rowmod, reverse=False), g)
        grev = each(lambda x: _seg_cumsum(x, rowmod, reverse=True) - x, g)
        decay = each(lambda x: jnp.exp(jnp.where(low_incl, gc_diff(x), -jnp.inf)), gc)
        kb = each(lambda a, b: a * b, k, beta)
        lmat = each(lambda a, b, dc: jnp.where(strict, _mm_nt(a, b) * dc, 0.0), kb, k, decay)
        egc = each(jnp.exp, gc)
        rhs = each(lambda a, b, c_, e: jnp.concatenate([a * b, c_ * e], axis=1), v, beta, kb, egc)

        xc = each(compact, lmat)
        tinv = each(lambda x: eye_c - x, xc)
        xs = each(split_expand, xc)
        level = 0
        p = 1
        while 2 * p < CHUNK:
            if level < 2:
                xc = each(lambda s4: dot(s4[0], s4[2]) + (dot(s4[0], s4[3]) + dot(s4[1], s4[2])), xs)
                xs = each(split_expand, xc)
                ts = each(_split2, tinv)
                tinv = each(lambda t, t2, s4: t + (dot(t2[0], s4[2]) + (dot(t2[0], s4[3]) + dot(t2[1], s4[2]))),
                            tinv, ts, xs)
            else:
                xh = each(lambda s4: dot(s4[0], s4[2]).astype(BF16), xs)
                xs = each(lambda x: (x, None, expand(x), None), xh)
                tinv = each(lambda t, s4: t + dot(t.astype(BF16), s4[2]), tinv, xs)
            level += 1
            p *= 2
        ts = each(split_expand, tinv)
        rs = each(_split2, rhs)
        sol = each(lambda t4, r2: dot(t4[2], r2[0]) + (dot(t4[2], r2[1]) + dot(t4[3], r2[0])), ts, rs)

        qk = each(lambda a, b, dc: jnp.where(low_incl, _mm_nt(a, b) * dc, 0.0), q, k, decay)
        q_dec = each(lambda a, e: a * e, q, egc)
        k_dec = each(lambda a, r: a * jnp.exp(r), k, grev)
        g_last = each(lambda a, r: jnp.exp(a + r), gc, grev)

        v_news = [[] for _ in hs]
        o_inter = [[] for _ in hs]
        s = [s_ref[h] for h in hs] if chained else None
        for c in range(nch):
            rc = slice(c * CHUNK, (c + 1) * CHUNK)
            if not chained:
                s = [s0_ref[c, h] for h in hs]
            v_new = each(lambda so, st: so[rc, :hd] - _mm(so[rc, hd:], st), sol, s)
            o_c = each(lambda qd, st: _mm(qd[rc], st), q_dec, s)
            s = each(lambda st, gl, kd, vn: st * gl[c * CHUNK:c * CHUNK + 1, :] + _mm_tn(kd[rc], vn),
                     s, g_last, k_dec, v_new)
            for j, h in enumerate(hs):
                v_news[j].append(v_new[j])
                o_inter[j].append(o_c[j])
                if not chained:
                    sout_ref[c, h] = s[j]
        if chained:
            for j, h in enumerate(hs):
                s_ref[h] = s[j]

            @pl.when(i == pl.num_programs(1) - 1)
            def _(s=s, hs=hs):
                for j, h in enumerate(hs):
                    sout_ref[h] = s[j]

        o = each(lambda oi, m, vn: jnp.concatenate(oi, axis=0) + _mm(m, jnp.concatenate(vn, axis=0)),
                 o_inter, qk, v_news)
        for j, h in enumerate(hs):
            o_ref[:, h * hd:(h + 1) * hd] = _rms(o[j], go_ref[...]) * _silu(z_ref[:, h * hd:(h + 1) * hd])


def _gdn_delta(act, bg, zsrc, zcol, n_seq, t, s0, g_o, n_heads, out_rows=None):
    d = act.shape[1] // 3
    hd = d // n_heads
    tb = 4 * CHUNK
    chained = t > CHUNK
    if chained:
        assert t % tb == 0
        grid = (n_seq, t // tb)
        rows = lambda b, i: b * (t // tb) + i
        sspec = pl.BlockSpec((None, n_heads, hd, hd), lambda b, i: (b, 0, 0, 0))
    else:
        assert t == CHUNK and n_seq % (tb // CHUNK) == 0
        grid = (n_seq * t // tb, 1)
        rows = lambda b, i: b
        sspec = pl.BlockSpec((tb // CHUNK, n_heads, hd, hd), lambda b, i: (b, 0, 0, 0))
    return pl.pallas_call(
        functools.partial(_gdn_delta_kernel, n_heads=n_heads, chained=chained, head_group=4),
        grid=grid,
        in_specs=[pl.BlockSpec((tb, 3 * d), lambda b, i: (rows(b, i), 0)),
                  pl.BlockSpec((tb, d), lambda b, i: (rows(b, i), zcol)),
                  pl.BlockSpec((tb, LANES), lambda b, i: (rows(b, i), 0)),
                  sspec,
                  pl.BlockSpec((1, hd), lambda b, i: (0, 0))],
        out_specs=[pl.BlockSpec((tb, d), lambda b, i: (rows(b, i), 0)), sspec],
        out_shape=[jax.ShapeDtypeStruct((out_rows or n_seq * t, d), F32),
                   jax.ShapeDtypeStruct((n_seq, n_heads, hd, hd), F32)],
        scratch_shapes=[pltpu.VMEM((n_heads, hd, hd), F32)],
        compiler_params=_cp("parallel", "arbitrary"),
        name="gdn_delta",
    )(act, zsrc, bg, s0, g_o.reshape(1, hd))


def _rglru_kernel(h_ref, g_ref, win_ref, cw_ref, cb_ref, wgate_ref, bra_ref, bri_ref, lam_ref, wout_ref,
                  hist_ref, h0_ref, oin_ref, o_ref, hlast_ref, xtail_ref, cx_ref, ch_ref, *, n_blocks):
    del oin_ref
    i = pl.program_id(1)
    tb, d = h_ref.shape
    cw = lam_ref.shape[1]
    blk = cw // n_blocks

    @pl.when(i == 0)
    def _():
        cx_ref[...] = hist_ref[...]
        ch_ref[...] = h0_ref[...]

    h = h_ref[...]
    proj = jnp.dot(_rms(h, g_ref[...]).astype(BF16), win_ref[...], preferred_element_type=F32)
    y_br = proj[:, :cw]
    x_br = proj[:, cw:]
    xc = _dwconv(x_br, cx_ref[...], cw_ref[...]) + cb_ref[...]
    r_parts, i_parts = [], []
    for n in range(n_blocks):
        ri = jnp.dot(xc[:, n * blk:(n + 1) * blk].astype(BF16), wgate_ref[n], preferred_element_type=F32)
        r_parts.append(ri[:, :blk])
        i_parts.append(ri[:, blk:])
    r = _sigmoid(jnp.concatenate(r_parts, axis=1) + bra_ref[...])
    ig = _sigmoid(jnp.concatenate(i_parts, axis=1) + bri_ref[...])
    log_a = -LRU_C * r * _softplus(-lam_ref[...])
    a = jnp.exp(log_a)
    th = jnp.tanh(log_a)
    b = jnp.sqrt(-2.0 * th / (1.0 - th)) * (ig * xc)
    row = lax.broadcasted_iota(jnp.int32, (tb, cw), 0)
    h_prev = ch_ref[SUBLANES - 1:SUBLANES, :]
    b = b + jnp.where(row == 0, a * h_prev, 0.0)
    step = 1
    while step < tb:
        a_s = jnp.where(row >= step, pltpu.roll(a, step, axis=0), 1.0)
        b_s = jnp.where(row >= step, pltpu.roll(b, step, axis=0), 0.0)
        b = a * b_s + b
        a = a * a_s
        step *= 2
    hs = b
    cx_ref[...] = x_br[tb - SUBLANES:, :]
    ch_ref[...] = hs[tb - SUBLANES:, :]

    @pl.when(i == pl.num_programs(1) - 1)
    def _():
        hlast_ref[...] = hs[tb - SUBLANES:, :]
        xtail_ref[...] = x_br[tb - SUBLANES:, :]

    o_ref[...] = h + jnp.dot((hs * _gelu(y_br)).astype(BF16), wout_ref[...], preferred_element_type=F32)


def _rglru(h, h_out, row0, n_seq, t, tb, g, w_in, conv_w, conv_b, wgate, b_ra, b_ri, lam, w_out, hist8, h08):
    m, d = h.shape
    cw = lam.shape[0]
    nb = wgate.shape[0]
    nt = t // tb
    r0 = row0 // tb
    vec = lambda x: x.reshape(1, -1)
    cst = lambda shape: pl.BlockSpec(shape, lambda b, i: (0,) * len(shape))
    in_specs = [pl.BlockSpec((tb, d), lambda b, i: (r0 + b * nt + i, 0)),
                cst((1, d)), cst((d, 2 * cw)), cst((CONV_W, cw)), cst((1, cw)),
                cst(wgate.shape), cst((1, cw)), cst((1, cw)), cst((1, cw)), cst((cw, d)),
                pl.BlockSpec((None, SUBLANES, cw), lambda b, i: (b, 0, 0)),
                pl.BlockSpec((None, SUBLANES, cw), lambda b, i: (b, 0, 0))]
    args = [h, vec(g), w_in, conv_w, vec(conv_b), wgate, vec(b_ra), vec(b_ri), vec(lam), w_out, hist8, h08]
    aliases = {}
    if h_out is not None:
        in_specs.append(pl.BlockSpec(memory_space=pl.ANY))
        args.append(h_out)
        aliases = {len(args) - 1: 0}
    kern = functools.partial(_rglru_kernel, n_blocks=nb)
    if h_out is None:
        kern = lambda *refs: _rglru_kernel(*refs[:12], None, *refs[12:], n_blocks=nb)
    return pl.pallas_call(
        kern,
        grid=(n_seq, nt),
        in_specs=in_specs,
        out_specs=[pl.BlockSpec((tb, d), lambda b, i: (r0 + b * nt + i, 0)),
                   pl.BlockSpec((None, SUBLANES, cw), lambda b, i: (b, 0, 0)),
                   pl.BlockSpec((None, SUBLANES, cw), lambda b, i: (b, 0, 0))],
        out_shape=[jax.ShapeDtypeStruct((m, d), F32),
                   jax.ShapeDtypeStruct((n_seq, SUBLANES, cw), F32),
                   jax.ShapeDtypeStruct((n_seq, SUBLANES, cw), F32)],
        scratch_shapes=[pltpu.VMEM((SUBLANES, cw), F32), pltpu.VMEM((SUBLANES, cw), F32)],
        input_output_aliases=aliases,
        compiler_params=_cp("parallel", "arbitrary"),
        name="rglru",
    )(*args)


def _gmlp_kernel(h_ref, g_ref, win_ref, bin_ref, lng_ref, lnb_ref, ws_ref, bs_ref, wout_ref, o_ref, vn_ref,
                 *, n_groups, first_sample_tile, lc_prompt, lc_sample):
    i = pl.program_id(0)
    tm, d = h_ref.shape
    half = lng_ref.shape[1]
    gw = half // n_groups
    h = h_ref[...]
    act = _gelu(jnp.dot(_rms(h, g_ref[...]).astype(BF16), win_ref[...], preferred_element_type=F32) + bin_ref[...])
    u = act[:, :half]
    v = act[:, half:]
    vc = v - jnp.mean(v, axis=-1, keepdims=True)
    vn = vc * lax.rsqrt(jnp.mean(vc * vc, axis=-1, keepdims=True) + EPS) * lng_ref[...] + lnb_ref[...]
    vn_ref[...] = vn
    shift = jnp.where(i >= first_sample_tile, lc_sample.bit_length() - 1, lc_prompt.bit_length() - 1)
    ri = lax.broadcasted_iota(jnp.int32, (tm, tm), 0)
    ci = lax.broadcasted_iota(jnp.int32, (tm, tm), 1)
    keep = (lax.shift_right_logical(ri, shift) == lax.shift_right_logical(ci, shift)) & (ri >= ci)
    vnb = vn.astype(BF16)
    parts = []
    for gi in range(n_groups):
        ws = jnp.where(keep, ws_ref[0, gi], 0.0).astype(BF16)
        sg = jnp.dot(ws, vnb[:, gi * gw:(gi + 1) * gw], preferred_element_type=F32)
        parts.append(u[:, gi * gw:(gi + 1) * gw] * (sg + bs_ref[0, :, gi * gw:(gi + 1) * gw]))
    gated = jnp.concatenate(parts, axis=1).astype(BF16)
    o_ref[...] = h + jnp.dot(gated, wout_ref[...], preferred_element_type=F32)


def _gmlp(h, m_p, t_s, g, w_in, b_in, ln_g, ln_b, w_s, b_s, w_out):
    m, d = h.shape
    half = ln_g.shape[0]
    ng = w_s.shape[0]
    gw = half // ng
    tm = 256
    lc_p, lc_s = D_CHUNK, min(t_s, D_CHUNK)
    assert m_p % tm == 0 and (m - m_p) % tm == 0 and tm % lc_p == 0 and tm % lc_s == 0

    def tiled(lc):
        wt = jnp.tile(w_s[:, :lc, :lc], (1, tm // lc, tm // lc))
        bt = jnp.repeat(jnp.tile(b_s[:, :lc].T, (tm // lc, 1)), gw, axis=1)
        return wt, bt

    wt_p, bt_p = tiled(lc_p)
    wt_s, bt_s = tiled(lc_s)
    ws_all = jnp.stack([wt_p, wt_s])
    bs_all = jnp.stack([bt_p, bt_s])
    fst = m_p // tm
    vec = lambda x: x.reshape(1, -1)
    cst = lambda shape: pl.BlockSpec(shape, lambda i: (0,) * len(shape))
    return pl.pallas_call(
        functools.partial(_gmlp_kernel, n_groups=ng, first_sample_tile=fst, lc_prompt=lc_p, lc_sample=lc_s),
        grid=(m // tm,),
        in_specs=[pl.BlockSpec((tm, d), lambda i: (i, 0)),
                  cst((1, d)), cst((d, 2 * half)), cst((1, 2 * half)), cst((1, half)), cst((1, half)),
                  pl.BlockSpec((1, ng, tm, tm), lambda i: (i // fst, 0, 0, 0)),
                  pl.BlockSpec((1, tm, half), lambda i: (i // fst, 0, 0)),
                  cst((half, d))],
        out_specs=[pl.BlockSpec((tm, d), lambda i: (i, 0)),
                   pl.BlockSpec((tm, half), lambda i: (jnp.maximum(i - fst, 0), 0))],
        out_shape=[jax.ShapeDtypeStruct((m, d), F32),
                   jax.ShapeDtypeStruct((m - m_p, half), F32)],
        compiler_params=_cp("arbitrary"),
        name="gmlp",
    )(h, vec(g), w_in, vec(b_in), vec(ln_g), vec(ln_b), ws_all, bs_all, w_out)


def _pad_rows8(x3):
    n, r, c = x3.shape
    return jnp.concatenate([jnp.zeros((n, SUBLANES - r, c), x3.dtype), x3], axis=1)


def kernel(x_prompt, x_sample, p_prompt, p_sample, cache_k_a, cache_v_a, state_s_b, state_conv_b, state_h_c, state_conv_c, g_mix, g_ffn, g_ple, g_final, w_in_a, g_q_a, g_k_a, rel_a, w_out_a, w_in_b, conv_b, a_log_b, dt_bias_b, g_o_b, w_out_b, w_in_c, conv_c, conv_bias_c, w_ra_c, b_ra_c, w_ri_c, b_ri_c, lam_c, w_out_c, w_in_d, b_in_d, ln_g_d, ln_b_d, w_s_d, b_s_d, w_out_d, w_gate_ff, w_up_ff, w_down_ff, w_router, w_gate_e, w_up_e, w_down_e, w_ple, w_ple_gate):
    n_b, t, d = x_prompt.shape
    n_s, t_s, _ = x_sample.shape
    depth = p_prompt.shape[0]
    pd = p_prompt.shape[-1]
    m_p, m_s = n_b * t, n_s * t_s
    m = m_p + m_s
    a_heads = d // g_q_a.shape[-1]
    b_heads = a_log_b.shape[-1]
    n_exp = w_router.shape[-1]
    bf = lambda x: x.astype(BF16)

    h = jnp.concatenate([x_prompt.reshape(m_p, d), x_sample.reshape(m_s, d)], axis=0)
    pp_all = p_prompt.reshape(depth, m_p, pd)
    ps_all = p_sample.reshape(depth, m_s, pd)

    outs = {name: [] for name in ("k_a_p", "v_a_p", "k_a_s", "v_a_s", "s_b_p", "conv_b_p", "s_b_s", "conv_b_s",
                                  "h_c_p", "conv_c_p", "h_c_s", "conv_c_s", "v_d_s")}

    def tail_rows(x2, n_seq, tt, row0, cols):
        keep = CONV_W - 1
        if n_seq <= 4:
            return jnp.stack([x2[row0 + (b + 1) * tt - keep:row0 + (b + 1) * tt, :cols] for b in range(n_seq)])
        return x2[row0:row0 + n_seq * tt, :cols].reshape(n_seq, tt, cols)[:, tt - keep:]

    y = None
    for li in range(depth):
        kind, c = li % 4, li // 4
        if kind == 0:
            qkv = _proj(h, g_mix[li], bf(w_in_a[c]), 1024)
            w_keep = cache_k_a.shape[2]
            o, kn_p, kn_s = _attention(qkv, n_b, t, n_s, t_s,
                                       cache_k_a[c].reshape(n_s, w_keep, d), cache_v_a[c].reshape(n_s, w_keep, d),
                                       rel_a[c], g_q_a[c], g_k_a[c], a_heads)
            hd = d // a_heads
            keep = min(A_PREV_CHUNKS * CHUNK, t)
            v_p = jnp.stack([qkv[(b + 1) * t - keep:(b + 1) * t, 2 * d:] for b in range(n_b)])
            outs["k_a_p"].append(kn_p.reshape(n_b, keep, a_heads, hd))
            outs["v_a_p"].append(v_p.reshape(n_b, keep, a_heads, hd))
            outs["k_a_s"].append(kn_s.reshape(n_s, t_s, a_heads, hd))
            outs["v_a_s"].append(qkv[m_p:, 2 * d:].reshape(n_s, t_s, a_heads, hd))
            h = _outproj_res(h, o, bf(w_out_a[c]))
        elif kind == 1:
            w_in = w_in_b[c]
            nq = 3 * d
            w_main = bf(w_in[:, :nq + d])
            w_aux = bf(jnp.pad(w_in[:, nq + d:], ((0, 0), (0, LANES - 2 * b_heads))))
            proj, aux = _proj(h, g_mix[li], w_main, 1024, w_aux)
            pvec = jnp.zeros((2, LANES), F32)
            pvec = pvec.at[0, b_heads:2 * b_heads].set(a_log_b[c]).at[1, b_heads:2 * b_heads].set(dt_bias_b[c])
            hd = d // b_heads
            act_p, bg_p = _gdn_act(proj, aux, 0, n_b, t, 256, jnp.zeros((n_b, SUBLANES, nq), F32),
                                   conv_b[c], pvec, b_heads)
            o_p, s_p = _gdn_delta(act_p, bg_p, proj, 3, n_b, t,
                                  jnp.zeros((n_b, b_heads, hd, hd), F32), g_o_b[c], b_heads, out_rows=m)
            act_s, bg_s = _gdn_act(proj, aux, m_p, n_s, t_s, t_s, _pad_rows8(state_conv_b[c]),
                                   conv_b[c], pvec, b_heads)
            tp = -(-t_s // CHUNK) * CHUNK
            padt = lambda x2: jnp.pad(x2.reshape(n_s, t_s, -1), ((0, 0), (0, tp - t_s), (0, 0))).reshape(n_s * tp, -1)
            z_s = padt(proj[m_p:, nq:nq + d])
            o_s, s_s = _gdn_delta(padt(act_s), padt(bg_s), z_s, 0, n_s, tp, state_s_b[c], g_o_b[c], b_heads)
            o_s = o_s.reshape(n_s, tp, d)[:, :t_s].reshape(m_s, d)
            outs["s_b_p"].append(s_p)
            outs["s_b_s"].append(s_s)
            outs["conv_b_p"].append(tail_rows(proj, n_b, t, 0, nq))
            outs["conv_b_s"].append(tail_rows(proj, n_s, t_s, m_p, nq))
            h = _outproj_res(h, lax.dynamic_update_slice(o_p, o_s, (m_p, 0)), bf(w_out_b[c]))
        elif kind == 2:
            cw = lam_c.shape[-1]
            wgate = bf(jnp.concatenate([w_ra_c[c], w_ri_c[c]], axis=-1))
            common = (g_mix[li], bf(w_in_c[c]), conv_c[c], conv_bias_c[c], wgate, b_ra_c[c], b_ri_c[c],
                      lam_c[c], bf(w_out_c[c]))
            zeros8 = jnp.zeros((n_b, SUBLANES, cw), F32)
            h_new, hl_p, xt_p = _rglru(h, None, 0, n_b, t, 256, *common, zeros8, zeros8)
            h_new, hl_s, xt_s = _rglru(h, h_new, m_p, n_s, t_s, t_s, *common,
                                       _pad_rows8(state_conv_c[c]), _pad_rows8(state_h_c[c][:, None, :]))
            h = h_new
            outs["h_c_p"].append(hl_p[:, SUBLANES - 1])
            outs["h_c_s"].append(hl_s[:, SUBLANES - 1])
            outs["conv_c_p"].append(xt_p[:, SUBLANES - (CONV_W - 1):])
            outs["conv_c_s"].append(xt_s[:, SUBLANES - (CONV_W - 1):])
        else:
            h, vn_s = _gmlp(h, m_p, t_s, g_mix[li], bf(w_in_d[c]), b_in_d[c], ln_g_d[c], ln_b_d[c],
                            w_s_d[c], b_s_d[c], bf(w_out_d[c]))
            outs["v_d_s"].append(vn_s.reshape(n_s, t_s, -1))

        j = li // 2
        if li % 2 == 0:
            h = _ffn_dense(h, g_ffn[li], bf(w_gate_ff[j]), bf(w_up_ff[j]), bf(w_down_ff[j]))
        else:
            wr = jnp.pad(w_router[j], ((0, 0), (0, LANES - n_exp)))
            h = _moe(h, g_ffn[li], wr, bf(w_gate_e[j]), bf(w_up_e[j]), bf(w_down_e[j]))
        last = li == depth - 1
        res = _ple(h, pp_all[li], ps_all[li], g_ple[li], bf(w_ple[li]), bf(w_ple_gate[li]),
                   g_final if last else None)
        if last:
            y = res
        else:
            h = res

    st = {name: jnp.stack(rows) for name, rows in outs.items()}
    return (y[0].reshape(n_b, t, d), y[1].reshape(n_s, t_s, d),
            st["k_a_p"], st["v_a_p"], st["k_a_s"], st["v_a_s"],
            st["s_b_p"], st["conv_b_p"], st["s_b_s"], st["conv_b_s"],
            st["h_c_p"], st["conv_c_p"], st["h_c_s"], st["conv_c_s"],
            st["v_d_s"])
```

```python
import functools

import jax
import jax.numpy as jnp
from jax import lax
from jax.experimental import pallas as pl
from jax.experimental.pallas import tpu as pltpu

F32 = jnp.float32
BF16 = jnp.bfloat16
EPS = 1e-6
CHUNK = 64
CONV_W = 4
A_REL_CLIP = 128
A_PREV_CHUNKS = 8
LRU_C = 8.0
D_CHUNK = 128
TOP_K = 2
LANES = 128
SUBLANES = 8
VMEM_LIMIT = 48 * 1024 * 1024


def _cp(*sem, vmem=VMEM_LIMIT):
    return pltpu.CompilerParams(dimension_semantics=sem, vmem_limit_bytes=vmem)


def _tile_m(m):
    for t in (1024, 512, 256):
        if m % t == 0:
            return t
    raise ValueError(f"token count {m} must be a multiple of 256")


def _mm(a, b):
    return jnp.dot(a.astype(BF16), b.astype(BF16), preferred_element_type=F32)


def _mm_nt(a, b):
    return lax.dot_general(a.astype(BF16), b.astype(BF16), (((1,), (1,)), ((), ())),
                           preferred_element_type=F32)


def _mm_tn(a, b):
    return lax.dot_general(a.astype(BF16), b.astype(BF16), (((0,), (0,)), ((), ())),
                           preferred_element_type=F32)


def _split2(x):
    hi = x.astype(BF16)
    lo = (x - hi.astype(F32)).astype(BF16)
    return hi, lo


def _split3(x):
    hi = x.astype(BF16)
    r = x - hi.astype(F32)
    mid = r.astype(BF16)
    lo = (r - mid.astype(F32)).astype(BF16)
    return hi, mid, lo


def _mm3(a, b):
    ah, al = _split2(a)
    bh, bl = _split2(b)
    d = functools.partial(jnp.dot, preferred_element_type=F32)
    return d(ah, bh) + (d(ah, bl) + d(al, bh))


def _rms(x, g):
    return x * lax.rsqrt(jnp.mean(x * x, axis=-1, keepdims=True) + EPS) * g


def _sigmoid(x):
    return 0.5 * (1.0 + jnp.tanh(0.5 * x))


def _silu(x):
    return x * _sigmoid(x)


def _softplus(x):
    return jnp.maximum(x, 0.0) + jnp.log1p(jnp.exp(-jnp.abs(x)))


def _gelu(x):
    c = 0.7978845608028654
    return x * (0.5 * (1.0 + jnp.tanh(c * (x + 0.044715 * (x * x * x)))))


def _shift_rows(x, prev8, s):
    r = pltpu.roll(x, s, axis=0)
    pr = pltpu.roll(prev8, s, axis=0)
    row8 = lax.broadcasted_iota(jnp.int32, pr.shape, 0)
    top = jnp.where(row8 < s, pr, r[0:SUBLANES])
    return jnp.concatenate([top, r[SUBLANES:]], axis=0)


def _dwconv(x, prev8, w):
    y = None
    for k in range(CONV_W):
        s = CONV_W - 1 - k
        xs = x if s == 0 else _shift_rows(x, prev8, s)
        term = xs * w[k:k + 1]
        y = term if y is None else y + term
    return y


def _proj_kernel(*refs, has_aux):
    if has_aux:
        x_ref, g_ref, w_ref, waux_ref, o_ref, oaux_ref, hn_ref = refs
    else:
        x_ref, g_ref, w_ref, o_ref, hn_ref = refs

    @pl.when(pl.program_id(1) == 0)
    def _():
        hn = _rms(x_ref[...], g_ref[...]).astype(BF16)
        hn_ref[...] = hn
        if has_aux:
            oaux_ref[...] = jnp.dot(hn, waux_ref[...], preferred_element_type=F32)

    o_ref[...] = jnp.dot(hn_ref[...], w_ref[...], preferred_element_type=F32)


def _proj(h, g, w, tn, waux=None):
    m, k = h.shape
    n = w.shape[1]
    tm = _tile_m(m)
    in_specs = [pl.BlockSpec((tm, k), lambda i, j: (i, 0)),
                pl.BlockSpec((1, k), lambda i, j: (0, 0)),
                pl.BlockSpec((k, tn), lambda i, j: (0, j))]
    out_shape = [jax.ShapeDtypeStruct((m, n), F32)]
    out_specs = [pl.BlockSpec((tm, tn), lambda i, j: (i, j))]
    args = [h, g.reshape(1, k), w]
    if waux is not None:
        na = waux.shape[1]
        in_specs.append(pl.BlockSpec((k, na), lambda i, j: (0, 0)))
        out_shape.append(jax.ShapeDtypeStruct((m, na), F32))
        out_specs.append(pl.BlockSpec((tm, na), lambda i, j: (i, 0)))
        args.append(waux)
    res = pl.pallas_call(
        functools.partial(_proj_kernel, has_aux=waux is not None),
        grid=(m // tm, n // tn),
        in_specs=in_specs, out_specs=out_specs, out_shape=out_shape,
        scratch_shapes=[pltpu.VMEM((tm, k), BF16)],
        compiler_params=_cp("parallel", "arbitrary"),
        name="proj",
    )(*args)
    return res if waux is not None else res[0]


def _outproj_kernel(h_ref, o_ref, w_ref, out_ref):
    out_ref[...] = h_ref[...] + jnp.dot(o_ref[...].astype(BF16), w_ref[...], preferred_element_type=F32)


def _outproj_res(h, o, w):
    m, d = h.shape
    k = o.shape[1]
    tm = _tile_m(m)
    return pl.pallas_call(
        _outproj_kernel,
        grid=(m // tm,),
        in_specs=[pl.BlockSpec((tm, d), lambda i: (i, 0)),
                  pl.BlockSpec((tm, k), lambda i: (i, 0)),
                  pl.BlockSpec((k, d), lambda i: (0, 0))],
        out_specs=pl.BlockSpec((tm, d), lambda i: (i, 0)),
        out_shape=jax.ShapeDtypeStruct((m, d), F32),
        compiler_params=_cp("parallel"),
        name="outproj_res",
    )(h, o, w)


def _ffn_kernel(x_ref, g_ref, wg_ref, wu_ref, wd_ref, o_ref, hn_ref):
    @pl.when(pl.program_id(1) == 0)
    def _():
        x = x_ref[...]
        hn_ref[...] = _rms(x, g_ref[...]).astype(BF16)
        o_ref[...] = x

    hn = hn_ref[...]
    gg = jnp.dot(hn, wg_ref[...], preferred_element_type=F32)
    uu = jnp.dot(hn, wu_ref[...], preferred_element_type=F32)
    hid = (_silu(gg) * uu).astype(BF16)
    o_ref[...] += jnp.dot(hid, wd_ref[...], preferred_element_type=F32)


def _ffn_dense(h, g, wg, wu, wd):
    m, d = h.shape
    f = wg.shape[1]
    tm = _tile_m(m)
    tf = 256
    return pl.pallas_call(
        _ffn_kernel,
        grid=(m // tm, f // tf),
        in_specs=[pl.BlockSpec((tm, d), lambda i, j: (i, 0)),
                  pl.BlockSpec((1, d), lambda i, j: (0, 0)),
                  pl.BlockSpec((d, tf), lambda i, j: (0, j)),
                  pl.BlockSpec((d, tf), lambda i, j: (0, j)),
                  pl.BlockSpec((tf, d), lambda i, j: (j, 0))],
        out_specs=pl.BlockSpec((tm, d), lambda i, j: (i, 0)),
        out_shape=jax.ShapeDtypeStruct((m, d), F32),
        scratch_shapes=[pltpu.VMEM((tm, d), BF16)],
        compiler_params=_cp("parallel", "arbitrary"),
        name="ffn_dense",
    )(h, g.reshape(1, d), wg, wu, wd)


def _moe_kernel(x_ref, g_ref, wr_ref, wg_ref, wu_ref, wd_ref, o_ref, hn_ref, gates_ref, rank_ref, rankt_ref,
                *, n_experts, pass_rows):
    e = pl.program_id(1)
    n_sub, _, sub = rankt_ref.shape

    @pl.when(e == 0)
    def _():
        ri = lax.broadcasted_iota(jnp.int32, (sub, sub), 0)
        ci = lax.broadcasted_iota(jnp.int32, (sub, sub), 1)
        tril = jnp.where(ri >= ci, 1.0, 0.0).astype(BF16)
        lane = lax.broadcasted_iota(jnp.int32, (sub, LANES), 1)
        for s in range(n_sub):
            rs = slice(s * sub, (s + 1) * sub)
            x = x_ref[rs, :]
            hn = _rms(x, g_ref[...])
            hn_ref[rs, :] = hn.astype(BF16)
            o_ref[rs, :] = x
            logits = _mm3(hn, wr_ref[...])
            lg = jnp.where(lane < n_experts, logits, -jnp.inf)
            m1 = jnp.max(lg, axis=-1, keepdims=True)
            i1 = jnp.min(jnp.where(lg == m1, lane, LANES), axis=-1, keepdims=True)
            lg2 = jnp.where(lane == i1, -jnp.inf, lg)
            m2 = jnp.max(lg2, axis=-1, keepdims=True)
            i2 = jnp.min(jnp.where(lg2 == m2, lane, LANES), axis=-1, keepdims=True)
            e2 = jnp.exp(m2 - m1)
            w1 = 1.0 / (1.0 + e2)
            w2 = e2 / (1.0 + e2)
            gates_ref[rs, :] = jnp.where(lane == i1, w1, 0.0) + jnp.where(lane == i2, w2, 0.0)
            sel = jnp.where(lane == i1, 1.0, jnp.where(lane == i2, 1.0, 0.0))
            count = jnp.dot(tril, sel.astype(BF16), preferred_element_type=F32)
            rank = jnp.where(sel > 0.0, count - 1.0, -1.0)
            rank_ref[rs, :] = rank
            rankt_ref[s] = rank.T

    lane = lax.broadcasted_iota(jnp.int32, (sub, LANES), 1)
    for s in range(n_sub):
        rs = slice(s * sub, (s + 1) * sub)
        rank_row = rankt_ref[s, pl.ds(e, 1), :]
        n_routed = (jnp.max(rank_row) + 1.0).astype(jnp.int32)
        rank_col = jnp.sum(jnp.where(lane == e, rank_ref[rs, :], 0.0), axis=-1, keepdims=True)
        gate_col = jnp.sum(jnp.where(lane == e, gates_ref[rs, :], 0.0), axis=-1, keepdims=True)

        def one_pass(cap, base, rs=rs, rank_row=rank_row, rank_col=rank_col, gate_col=gate_col):
            row_id = lax.broadcasted_iota(jnp.int32, (cap, sub), 0).astype(F32) + base
            col_id = lax.broadcasted_iota(jnp.int32, (sub, cap), 1).astype(F32) + base
            pick = jnp.where(rank_row == row_id, 1.0, 0.0).astype(BF16)
            xe = jnp.dot(pick, hn_ref[rs, :], preferred_element_type=F32).astype(BF16)
            gg = jnp.dot(xe, wg_ref[0], preferred_element_type=F32)
            uu = jnp.dot(xe, wu_ref[0], preferred_element_type=F32)
            hid = (_silu(gg) * uu).astype(BF16)
            ye = jnp.dot(hid, wd_ref[0], preferred_element_type=F32).astype(BF16)
            put = jnp.where(rank_col == col_id, gate_col, 0.0).astype(BF16)
            o_ref[rs, :] += jnp.dot(put, ye, preferred_element_type=F32)

        small, mid, full = pass_rows
        is_small = (n_routed > 0) & (n_routed <= small)
        is_mid = (n_routed > small) & (n_routed <= mid)

        @pl.when(is_small)
        def _(one_pass=one_pass):
            one_pass(small, 0.0)

        @pl.when(is_mid)
        def _(one_pass=one_pass):
            one_pass(mid, 0.0)

        @pl.when(n_routed > mid)
        def _(one_pass=one_pass, n_routed=n_routed):
            def body(cc, carry):
                one_pass(full, (cc * full).astype(F32))
                return carry
            lax.fori_loop(0, (n_routed + full - 1) // full, body, 0)


def _moe(h, g, wr, wg, wu, wd):
    m, d = h.shape
    ne, _, ef = wg.shape
    sub = 512
    tm = 2 * sub if m % (2 * sub) == 0 else sub
    mean_rows = sub * TOP_K // ne
    pass_rows = (mean_rows, mean_rows * 3 // 2, mean_rows * 2)
    assert m % tm == 0
    return pl.pallas_call(
        functools.partial(_moe_kernel, n_experts=ne, pass_rows=pass_rows),
        grid=(m // tm, ne),
        in_specs=[pl.BlockSpec((tm, d), lambda i, e: (i, 0)),
                  pl.BlockSpec((1, d), lambda i, e: (0, 0)),
                  pl.BlockSpec((d, LANES), lambda i, e: (0, 0)),
                  pl.BlockSpec((1, d, ef), lambda i, e: (e, 0, 0)),
                  pl.BlockSpec((1, d, ef), lambda i, e: (e, 0, 0)),
                  pl.BlockSpec((1, ef, d), lambda i, e: (e, 0, 0))],
        out_specs=pl.BlockSpec((tm, d), lambda i, e: (i, 0)),
        out_shape=jax.ShapeDtypeStruct((m, d), F32),
        scratch_shapes=[pltpu.VMEM((tm, d), BF16), pltpu.VMEM((tm, LANES), F32),
                        pltpu.VMEM((tm, LANES), F32), pltpu.VMEM((tm // sub, LANES, sub), F32)],
        compiler_params=_cp("parallel", "arbitrary"),
        name="moe",
    )(h, g.reshape(1, d), wr, wg, wu, wd)


def _ple_kernel(*refs, final, n_prompt_tiles):
    if final:
        x_ref, pp_ref, ps_ref, g_ref, wp_ref, wg_ref, gf_ref, yp_ref, ys_ref = refs
    else:
        x_ref, pp_ref, ps_ref, g_ref, wp_ref, wg_ref, o_ref = refs
    i = pl.program_id(0)
    is_sample = i >= n_prompt_tiles
    x = x_ref[...]
    hn = _rms(x, g_ref[...]).astype(BF16)
    gate = _sigmoid(jnp.dot(hn, wg_ref[...], preferred_element_type=F32))
    p = jnp.where(is_sample, ps_ref[...], pp_ref[...])
    pp = jnp.dot(p.astype(BF16), wp_ref[...], preferred_element_type=F32)
    out = x + pp * gate
    if final:
        y = _rms(out, gf_ref[...])

        @pl.when(jnp.logical_not(is_sample))
        def _():
            yp_ref[...] = y

        @pl.when(is_sample)
        def _():
            ys_ref[...] = y
    else:
        o_ref[...] = out


def _ple(h, p_prompt, p_sample, g, wp, wgate, g_final=None):
    m, d = h.shape
    m_p, pd = p_prompt.shape
    m_s = p_sample.shape[0]
    tm = _tile_m(m)
    assert m_p % tm == 0 and m_s % tm == 0
    npt, nst = m_p // tm, m_s // tm
    final = g_final is not None
    in_specs = [pl.BlockSpec((tm, d), lambda i: (i, 0)),
                pl.BlockSpec((tm, pd), lambda i: (jnp.minimum(i, npt - 1), 0)),
                pl.BlockSpec((tm, pd), lambda i: (jnp.maximum(i - npt, 0), 0)),
                pl.BlockSpec((1, d), lambda i: (0, 0)),
                pl.BlockSpec((pd, d), lambda i: (0, 0)),
                pl.BlockSpec((d, d), lambda i: (0, 0))]
    args = [h, p_prompt, p_sample, g.reshape(1, d), wp, wgate]
    if final:
        in_specs.append(pl.BlockSpec((1, d), lambda i: (0, 0)))
        args.append(g_final.reshape(1, d))
        out_specs = [pl.BlockSpec((tm, d), lambda i: (jnp.minimum(i, npt - 1), 0)),
                     pl.BlockSpec((tm, d), lambda i: (jnp.maximum(i - npt, 0), 0))]
        out_shape = [jax.ShapeDtypeStruct((m_p, d), F32), jax.ShapeDtypeStruct((m_s, d), F32)]
    else:
        out_specs = pl.BlockSpec((tm, d), lambda i: (i, 0))
        out_shape = jax.ShapeDtypeStruct((m, d), F32)
    return pl.pallas_call(
        functools.partial(_ple_kernel, final=final, n_prompt_tiles=npt),
        grid=(npt + nst,),
        in_specs=in_specs, out_specs=out_specs, out_shape=out_shape,
        compiler_params=_cp("arbitrary"),
        name="ple",
    )(*args)


def _pair_rms(x, g, hd):
    lane = lax.broadcasted_iota(jnp.int32, x.shape, 1)
    lo = lane < hd
    x2 = x * x
    s0 = jnp.sum(jnp.where(lo, x2, 0.0), axis=-1, keepdims=True)
    s1 = jnp.sum(jnp.where(lo, 0.0, x2), axis=-1, keepdims=True)
    ms = jnp.where(lo, s0, s1) * (1.0 / hd)
    return x * lax.rsqrt(ms + EPS) * g


def _stack_heads(q, hd):
    lane = lax.broadcasted_iota(jnp.int32, q.shape, 1)
    lo = lane < hd
    return jnp.concatenate([jnp.where(lo, q, 0.0), jnp.where(lo, 0.0, q)], axis=0).astype(BF16)


def _unstack_heads(o2, hd):
    n = o2.shape[0] // 2
    lane = lax.broadcasted_iota(jnp.int32, (n, o2.shape[1]), 1)
    return jnp.where(lane < hd, o2[:n], o2[n:])


def _attend(q2, kw, vw, bias, valid):
    s = _mm_nt(q2, kw) + bias
    if valid is not None:
        s = jnp.where(valid, s, -jnp.inf)
    mx = jnp.max(s, axis=-1, keepdims=True)
    e = jnp.exp(s - mx)
    den = jnp.sum(e, axis=-1, keepdims=True)
    return jnp.dot(e.astype(BF16), vw, preferred_element_type=F32) / den


def _attn_prompt_kernel(q_ref, k_ref, v_ref, b_ref, gq_ref, gk_ref, o_ref, kn_ref, kwin_ref, vwin_ref,
                        *, hd, scale, reach, chunk_group):
    i = pl.program_id(2)
    tq = q_ref.shape[0]
    band = reach + CHUNK
    qn = _pair_rms(q_ref[...], gq_ref[...], hd) * scale
    knc = _pair_rms(k_ref[...], gk_ref[...], hd)

    @pl.when(i == pl.num_programs(2) - 1)
    def _():
        kn_ref[...] = knc

    @pl.when(i == 0)
    def _():
        kwin_ref[0:tq, :] = jnp.zeros((tq, 2 * hd), BF16)
        vwin_ref[0:tq, :] = jnp.zeros((tq, 2 * hd), BF16)

    @pl.when(i > 0)
    def _():
        kwin_ref[0:tq, :] = kwin_ref[tq:2 * tq, :]
        vwin_ref[0:tq, :] = vwin_ref[tq:2 * tq, :]

    kwin_ref[tq:2 * tq, :] = knc.astype(BF16)
    vwin_ref[tq:2 * tq, :] = v_ref[...].astype(BF16)
    bias = b_ref[0]
    col = lax.broadcasted_iota(jnp.int32, (1, band), 1)
    n_chunks = tq // CHUNK
    w0s = [tq - reach + jj * CHUNK for jj in range(n_chunks)]

    def attend(first_block, jjs):
        q2 = [_stack_heads(qn[jj * CHUNK:(jj + 1) * CHUNK], hd) for jj in jjs]
        s = [_mm_nt(x, kwin_ref[w0s[jj]:w0s[jj] + band, :]) + bias for x, jj in zip(q2, jjs)]
        if first_block:
            s = [jnp.where(col + w0s[jj] >= tq, x, -jnp.inf) for x, jj in zip(s, jjs)]
        mx = [jnp.max(x, axis=-1, keepdims=True) for x in s]
        e = [jnp.exp(x - m) for x, m in zip(s, mx)]
        den = [jnp.sum(x, axis=-1, keepdims=True) for x in e]
        o2 = [jnp.dot(x.astype(BF16), vwin_ref[w0s[jj]:w0s[jj] + band, :], preferred_element_type=F32)
              for x, jj in zip(e, jjs)]
        for x, dn, jj in zip(o2, den, jjs):
            o_ref[jj * CHUNK:(jj + 1) * CHUNK, :] = _unstack_heads(x / dn, hd)

    def attend_all(first_block):
        for j0 in range(0, n_chunks, chunk_group):
            attend(first_block, list(range(j0, j0 + chunk_group)))

    @pl.when(i == 0)
    def _():
        attend_all(True)

    @pl.when(i > 0)
    def _():
        attend_all(False)


def _attn_sample_kernel(q_ref, k_ref, v_ref, ck_ref, cv_ref, b_ref, gq_ref, gk_ref, oin_ref, o_ref, kn_ref,
                        *, hd, scale):
    del oin_ref
    pw = 2 * hd
    pairs = [slice(p * pw, (p + 1) * pw) for p in range(q_ref.shape[1] // pw)]
    qn = [_pair_rms(q_ref[:, c], gq_ref[...], hd) * scale for c in pairs]
    kn = [_pair_rms(k_ref[:, c], gk_ref[...], hd) for c in pairs]
    for c, x in zip(pairs, kn):
        kn_ref[:, c] = x
    kcat = [jnp.concatenate([ck_ref[:, c], x], axis=0).astype(BF16) for c, x in zip(pairs, kn)]
    vcat = [jnp.concatenate([cv_ref[:, c], v_ref[:, c]], axis=0).astype(BF16) for c in pairs]
    s = [_mm_nt(_stack_heads(x, hd), kk) + b_ref[p] for p, (x, kk) in enumerate(zip(qn, kcat))]
    mx = [jnp.max(x, axis=-1, keepdims=True) for x in s]
    e = [jnp.exp(x - m) for x, m in zip(s, mx)]
    den = [jnp.sum(x, axis=-1, keepdims=True) for x in e]
    o2 = [jnp.dot(x.astype(BF16), vv, preferred_element_type=F32) for x, vv in zip(e, vcat)]
    for c, x, dn in zip(pairs, o2, den):
        o_ref[:, c] = _unstack_heads(x / dn, hd)


def _rel_bias_pairs(table, q0, nq, nk):
    span = nq + nk - 1
    dist = jnp.arange(span) + (q0 - (nk - 1))
    line = table[:, jnp.clip(dist, -A_REL_CLIP, A_REL_CLIP) + A_REL_CLIP]
    flipped = line[:, ::-1]
    b = jnp.stack([flipped[:, nq - 1 - q:nq - 1 - q + nk] for q in range(nq)], axis=1)
    nh = b.shape[0]
    return b.reshape(nh // 2, 2 * nq, nk)


def _attention(qkv, n_b, t, n_s, t_s, cache_k, cache_v, rel, g_q, g_k, n_heads):
    m_tot, d3 = qkv.shape
    d = d3 // 3
    hd = d // n_heads
    pw = 2 * hd
    n_pairs = d // pw
    m_p = n_b * t
    reach = A_PREV_CHUNKS * CHUNK
    tq = reach
    assert pw == LANES and t % tq == 0 and t_s % 16 == 0
    nqb = t // tq
    scale = hd ** -0.5
    gq2 = jnp.tile(g_q, 2).reshape(1, pw)
    gk2 = jnp.tile(g_k, 2).reshape(1, pw)

    bias_p = _rel_bias_pairs(rel, reach, CHUNK, reach + CHUNK)
    kcol, vcol = n_pairs, 2 * n_pairs

    def rows(b, p, i):
        return b * nqb + i

    blk = (tq, pw)
    o_all, kn_p = pl.pallas_call(
        functools.partial(_attn_prompt_kernel, hd=hd, scale=scale, reach=reach, chunk_group=4),
        grid=(n_b, n_pairs, nqb),
        in_specs=[pl.BlockSpec(blk, lambda b, p, i: (rows(b, p, i), p)),
                  pl.BlockSpec(blk, lambda b, p, i: (rows(b, p, i), kcol + p)),
                  pl.BlockSpec(blk, lambda b, p, i: (rows(b, p, i), vcol + p)),
                  pl.BlockSpec((1, 2 * CHUNK, reach + CHUNK), lambda b, p, i: (p, 0, 0)),
                  pl.BlockSpec((1, pw), lambda b, p, i: (0, 0)),
                  pl.BlockSpec((1, pw), lambda b, p, i: (0, 0))],
        out_specs=[pl.BlockSpec(blk, lambda b, p, i: (rows(b, p, i), p)),
                   pl.BlockSpec(blk, lambda b, p, i: (b, p))],
        out_shape=[jax.ShapeDtypeStruct((m_tot, d), F32),
                   jax.ShapeDtypeStruct((n_b * tq, d), F32)],
        scratch_shapes=[pltpu.VMEM((2 * tq, pw), BF16), pltpu.VMEM((2 * tq, pw), BF16)],
        compiler_params=_cp("parallel", "parallel", "arbitrary"),
        name="attn_prompt",
    )(qkv, qkv, qkv, bias_p, gq2, gk2)

    w = cache_k.shape[1]
    bias_s = _rel_bias_pairs(rel, w, t_s, w + t_s)
    srow = m_p // t_s
    sblk = (t_s, d)
    o_all, kn_s = pl.pallas_call(
        functools.partial(_attn_sample_kernel, hd=hd, scale=scale),
        grid=(n_s,),
        in_specs=[pl.BlockSpec(sblk, lambda b: (srow + b, 0)),
                  pl.BlockSpec(sblk, lambda b: (srow + b, 1)),
                  pl.BlockSpec(sblk, lambda b: (srow + b, 2)),
                  pl.BlockSpec((None, w, d), lambda b: (b, 0, 0)),
                  pl.BlockSpec((None, w, d), lambda b: (b, 0, 0)),
                  pl.BlockSpec((n_pairs, 2 * t_s, w + t_s), lambda b: (0, 0, 0)),
                  pl.BlockSpec((1, pw), lambda b: (0, 0)),
                  pl.BlockSpec((1, pw), lambda b: (0, 0)),
                  pl.BlockSpec(memory_space=pl.ANY)],
        out_specs=[pl.BlockSpec(sblk, lambda b: (srow + b, 0)),
                   pl.BlockSpec(sblk, lambda b: (b, 0))],
        out_shape=[jax.ShapeDtypeStruct((m_tot, d), F32),
                   jax.ShapeDtypeStruct((n_s * t_s, d), F32)],
        input_output_aliases={8: 0},
        compiler_params=_cp("parallel"),
        name="attn_sample",
    )(qkv, qkv, qkv, cache_k, cache_v, bias_s, gq2, gk2, o_all)
    return o_all, kn_p, kn_s


def _gdn_act_kernel(xq_ref, xk_ref, xv_ref, pq_ref, pk_ref, pv_ref, hq_ref, hk_ref, hv_ref,
                    aux_ref, cw_ref, pvec_ref, o_ref, bg_ref, *, blocks_per_seq, n_heads, scale):
    i = pl.program_id(0)
    start = (i % blocks_per_seq) == 0
    d = xq_ref.shape[1]
    hd = d // n_heads
    parts = ((xq_ref, pq_ref, hq_ref), (xk_ref, pk_ref, hk_ref), (xv_ref, pv_ref, hv_ref))
    for c, (x_ref, p_ref, h_ref) in enumerate(parts):
        prev8 = jnp.where(start, h_ref[...], p_ref[...])
        a = _silu(_dwconv(x_ref[...], prev8, cw_ref[:, c * d:(c + 1) * d]))
        for hh in range(n_heads):
            ah = a[:, hh * hd:(hh + 1) * hd]
            if c < 2:
                ah = ah * lax.rsqrt(jnp.sum(ah * ah, axis=-1, keepdims=True) + EPS)
            if c == 0:
                ah = ah * scale
            o_ref[:, c * d + hh * hd:c * d + (hh + 1) * hd] = ah
    ax = aux_ref[...]
    lane = lax.broadcasted_iota(jnp.int32, ax.shape, 1)
    beta = _sigmoid(ax)
    gdec = -jnp.exp(pvec_ref[0:1, :]) * _softplus(ax + pvec_ref[1:2, :])
    bg_ref[...] = jnp.where(lane < n_heads, beta, jnp.where(lane < 2 * n_heads, gdec, 0.0))


def _gdn_act(proj, aux, row0, n_seq, t, tb, hist8, conv_w, pvec, n_heads):
    d = proj.shape[1] // 4
    hd = d // n_heads
    bps = t // tb
    r0 = row0 // tb
    p8 = tb // SUBLANES

    def xspec(c):
        return pl.BlockSpec((tb, d), lambda i, c=c: (r0 + i, c))

    def pspec(c):
        return pl.BlockSpec((SUBLANES, d), lambda i, c=c: (jnp.maximum((r0 + i) * p8 - 1, 0), c))

    def hspec(c):
        return pl.BlockSpec((None, SUBLANES, d), lambda i, c=c: (i // bps, 0, c))

    m = n_seq * t
    return pl.pallas_call(
        functools.partial(_gdn_act_kernel, blocks_per_seq=bps, n_heads=n_heads, scale=hd ** -0.5),
        grid=(m // tb,),
        in_specs=[xspec(0), xspec(1), xspec(2), pspec(0), pspec(1), pspec(2), hspec(0), hspec(1), hspec(2),
                  pl.BlockSpec((tb, LANES), lambda i: (r0 + i, 0)),
                  pl.BlockSpec((CONV_W, 3 * d), lambda i: (0, 0)),
                  pl.BlockSpec((2, LANES), lambda i: (0, 0))],
        out_specs=[pl.BlockSpec((tb, 3 * d), lambda i: (i, 0)),
                   pl.BlockSpec((tb, LANES), lambda i: (i, 0))],
        out_shape=[jax.ShapeDtypeStruct((m, 3 * d), F32), jax.ShapeDtypeStruct((m, LANES), F32)],
        compiler_params=_cp("parallel"),
        name="gdn_act",
    )(proj, proj, proj, proj, proj, proj, hist8, hist8, hist8, aux, conv_w, pvec)


def _seg_cumsum(x, rowmod, reverse):
    n = x.shape[0]
    d = 1
    while d < CHUNK:
        if reverse:
            sh = pltpu.roll(x, n - d, axis=0)
            x = x + jnp.where(rowmod < CHUNK - d, sh, 0.0)
        else:
            sh = pltpu.roll(x, d, axis=0)
            x = x + jnp.where(rowmod >= d, sh, 0.0)
        d *= 2
    return x


def _gdn_delta_kernel(act_ref, z_ref, bg_ref, s0_ref, go_ref, o_ref, sout_ref, s_ref, *, n_heads, chained, head_group):
    i = pl.program_id(1)
    tb, d = z_ref.shape
    hd = d // n_heads
    nch = tb // CHUNK
    dot = functools.partial(jnp.dot, preferred_element_type=F32)

    if chained:
        @pl.when(i == 0)
        def _():
            s_ref[...] = s0_ref[...]

    bg = bg_ref[...]
    lane = lax.broadcasted_iota(jnp.int32, (tb, hd), 1)
    rowmod = lax.broadcasted_iota(jnp.int32, (tb, hd), 0) & (CHUNK - 1)
    ri = lax.broadcasted_iota(jnp.int32, (tb, tb), 0)
    ci = lax.broadcasted_iota(jnp.int32, (tb, tb), 1)
    same = (ri >> 6) == (ci >> 6)
    low_incl = same & (ri >= ci)
    strict = same & (ri > ci)
    eye_c = (lax.broadcasted_iota(jnp.int32, (CHUNK, tb), 0)
             == (lax.broadcasted_iota(jnp.int32, (CHUNK, tb), 1) & (CHUNK - 1))).astype(F32)
    def compact(x):
        acc = x[0:CHUNK]
        for c in range(1, nch):
            acc = acc + x[c * CHUNK:(c + 1) * CHUNK]
        return acc

    def expand(xc):
        return jnp.where(same, jnp.concatenate([xc] * nch, axis=0), jnp.zeros((), xc.dtype))

    def split_expand(xc):
        hi, lo = _split2(xc)
        return hi, lo, expand(hi), expand(lo)

    def each(fn, *lists):
        return [fn(*a) for a in zip(*lists)]

    cum_fwd = _seg_cumsum(bg, rowmod, reverse=False)
    cum_rev = _seg_cumsum(bg, rowmod, reverse=True) - bg

    def column(arr, idx):
        return jnp.broadcast_to(jnp.sum(jnp.where(lane == idx, arr, 0.0), axis=-1, keepdims=True), (tb, hd))

    def gc_diff(gc):
        gh, gm, gl = (piece.astype(F32) for piece in _split3(gc))
        lhs = jnp.where(lane == 0, gh, jnp.where(lane == 1, gm, jnp.where(lane == 2, gl,
                                                                          jnp.where(lane < 6, 1.0, 0.0))))
        rhs_t = jnp.where(lane < 3, 1.0, jnp.where(lane == 3, -gh, jnp.where(lane == 4, -gm,
                                                                             jnp.where(lane == 5, -gl, 0.0))))
        return _mm_nt(lhs, rhs_t)

    for h0 in range(0, n_heads, head_group):
        hs = list(range(h0, h0 + head_group))
        q = [act_ref[:, h * hd:(h + 1) * hd] for h in hs]
        k = [act_ref[:, d + h * hd:d + (h + 1) * hd] for h in hs]
        v = [act_ref[:, 2 * d + h * hd:2 * d + (h + 1) * hd] for h in hs]
        beta = [column(bg, h) for h in hs]
        gc = [column(cum_fwd, n_heads + h) for h in hs]
        grev = [column(cum_rev, n_heads + h) for h in hs]
        decay = each(lambda x: jnp.exp(jnp.where(low_incl, gc_diff(x), -jnp.inf)), gc)
        kb = each(lambda a, b: a * b, k, beta)
        lmat = each(lambda a, b, dc: jnp.where(strict, _mm_nt(a, b) * dc, 0.0), kb, k, decay)
        egc = each(jnp.exp, gc)
        rhs = each(lambda a, b, c_, e: jnp.concatenate([a * b, c_ * e], axis=1), v, beta, kb, egc)

        xc = each(compact, lmat)
        tinv = each(lambda x: eye_c - x, xc)
        xs = each(split_expand, xc)
        level = 0
        p = 1
        while 2 * p < CHUNK:
            if level < 2:
                xc = each(lambda s4: dot(s4[0], s4[2]) + (dot(s4[0], s4[3]) + dot(s4[1], s4[2])), xs)
                xs = each(split_expand, xc)
                ts = each(_split2, tinv)
                tinv = each(lambda t, t2, s4: t + (dot(t2[0], s4[2]) + (dot(t2[0], s4[3]) + dot(t2[1], s4[2]))),
                            tinv, ts, xs)
            else:
                xh = each(lambda s4: dot(s4[0], s4[2]).astype(BF16), xs)
                xs = each(lambda x: (x, None, expand(x), None), xh)
                tinv = each(lambda t, s4: t + dot(t.astype(BF16), s4[2]), tinv, xs)
            level += 1
            p *= 2
        ts = each(split_expand, tinv)
        rs = each(_split2, rhs)
        sol = each(lambda t4, r2: dot(t4[2], r2[0]) + (dot(t4[2], r2[1]) + dot(t4[3], r2[0])), ts, rs)

        qk = each(lambda a, b, dc: jnp.where(low_incl, _mm_nt(a, b) * dc, 0.0), q, k, decay)
        q_dec = each(lambda a, e: a * e, q, egc)
        k_dec = each(lambda a, r: a * jnp.exp(r), k, grev)
        g_last = each(lambda a, r: jnp.exp(a + r), gc, grev)

        v_news = [[] for _ in hs]
        o_inter = [[] for _ in hs]
        s = [s_ref[h] for h in hs] if chained else None
        for c in range(nch):
            rc = slice(c * CHUNK, (c + 1) * CHUNK)
            if not chained:
                s = [s0_ref[c, h] for h in hs]
            v_new = each(lambda so, st: so[rc, :hd] - _mm(so[rc, hd:], st), sol, s)
            o_c = each(lambda qd, st: _mm(qd[rc], st), q_dec, s)
            s = each(lambda st, gl, kd, vn: st * gl[c * CHUNK:c * CHUNK + 1, :] + _mm_tn(kd[rc], vn),
                     s, g_last, k_dec, v_new)
            for j, h in enumerate(hs):
                v_news[j].append(v_new[j])
                o_inter[j].append(o_c[j])
                if not chained:
                    sout_ref[c, h] = s[j]
        if chained:
            for j, h in enumerate(hs):
                s_ref[h] = s[j]

            @pl.when(i == pl.num_programs(1) - 1)
            def _(s=s, hs=hs):
                for j, h in enumerate(hs):
                    sout_ref[h] = s[j]

        o = each(lambda oi, m, vn: jnp.concatenate(oi, axis=0) + _mm(m, jnp.concatenate(vn, axis=0)),
                 o_inter, qk, v_news)
        for j, h in enumerate(hs):
            o_ref[:, h * hd:(h + 1) * hd] = _rms(o[j], go_ref[...]) * _silu(z_ref[:, h * hd:(h + 1) * hd])


def _gdn_delta(act, bg, zsrc, zcol, n_seq, t, s0, g_o, n_heads, out_rows=None):
    d = act.shape[1] // 3
    hd = d // n_heads
    tb = 4 * CHUNK
    chained = t > CHUNK
    if chained:
        assert t % tb == 0
        grid = (n_seq, t // tb)
        rows = lambda b, i: b * (t // tb) + i
        sspec = pl.BlockSpec((None, n_heads, hd, hd), lambda b, i: (b, 0, 0, 0))
    else:
        assert t == CHUNK and n_seq % (tb // CHUNK) == 0
        grid = (n_seq * t // tb, 1)
        rows = lambda b, i: b
        sspec = pl.BlockSpec((tb // CHUNK, n_heads, hd, hd), lambda b, i: (b, 0, 0, 0))
    return pl.pallas_call(
        functools.partial(_gdn_delta_kernel, n_heads=n_heads, chained=chained, head_group=8),
        grid=grid,
        in_specs=[pl.BlockSpec((tb, 3 * d), lambda b, i: (rows(b, i), 0)),
                  pl.BlockSpec((tb, d), lambda b, i: (rows(b, i), zcol)),
                  pl.BlockSpec((tb, LANES), lambda b, i: (rows(b, i), 0)),
                  sspec,
                  pl.BlockSpec((1, hd), lambda b, i: (0, 0))],
        out_specs=[pl.BlockSpec((tb, d), lambda b, i: (rows(b, i), 0)), sspec],
        out_shape=[jax.ShapeDtypeStruct((out_rows or n_seq * t, d), F32),
                   jax.ShapeDtypeStruct((n_seq, n_heads, hd, hd), F32)],
        scratch_shapes=[pltpu.VMEM((n_heads, hd, hd), F32)],
        compiler_params=_cp("parallel", "arbitrary"),
        name="gdn_delta",
    )(act, zsrc, bg, s0, g_o.reshape(1, hd))


def _rglru_kernel(h_ref, g_ref, win_ref, cw_ref, cb_ref, wgate_ref, bra_ref, bri_ref, lam_ref, wout_ref,
                  hist_ref, h0_ref, oin_ref, o_ref, hlast_ref, xtail_ref, cx_ref, ch_ref, *, n_blocks):
    del oin_ref
    i = pl.program_id(1)
    tb, d = h_ref.shape
    cw = lam_ref.shape[1]
    blk = cw // n_blocks

    @pl.when(i == 0)
    def _():
        cx_ref[...] = hist_ref[...]
        ch_ref[...] = h0_ref[...]

    h = h_ref[...]
    proj = jnp.dot(_rms(h, g_ref[...]).astype(BF16), win_ref[...], preferred_element_type=F32)
    y_br = proj[:, :cw]
    x_br = proj[:, cw:]
    xc = _dwconv(x_br, cx_ref[...], cw_ref[...]) + cb_ref[...]
    r_parts, i_parts = [], []
    for n in range(n_blocks):
        ri = jnp.dot(xc[:, n * blk:(n + 1) * blk].astype(BF16), wgate_ref[n], preferred_element_type=F32)
        r_parts.append(ri[:, :blk])
        i_parts.append(ri[:, blk:])
    r = _sigmoid(jnp.concatenate(r_parts, axis=1) + bra_ref[...])
    ig = _sigmoid(jnp.concatenate(i_parts, axis=1) + bri_ref[...])
    log_a = -LRU_C * r * _softplus(-lam_ref[...])
    a = jnp.exp(log_a)
    th = jnp.tanh(log_a)
    b = jnp.sqrt(-2.0 * th / (1.0 - th)) * (ig * xc)
    row8 = lax.broadcasted_iota(jnp.int32, (tb, cw), 0) & (SUBLANES - 1)
    step = 1
    while step < SUBLANES:
        a_s = jnp.where(row8 >= step, pltpu.roll(a, step, axis=0), 1.0)
        b_s = jnp.where(row8 >= step, pltpu.roll(b, step, axis=0), 0.0)
        b = a * b_s + b
        a = a * a_s
        step *= 2
    h_in = ch_ref[SUBLANES - 1:SUBLANES, :]
    groups = []
    for j in range(tb // SUBLANES):
        rows = slice(j * SUBLANES, (j + 1) * SUBLANES)
        hg = b[rows] + a[rows] * h_in
        groups.append(hg)
        h_in = hg[SUBLANES - 1:SUBLANES, :]
    hs = jnp.concatenate(groups, axis=0)
    cx_ref[...] = x_br[tb - SUBLANES:, :]
    ch_ref[...] = hs[tb - SUBLANES:, :]

    @pl.when(i == pl.num_programs(1) - 1)
    def _():
        hlast_ref[...] = hs[tb - SUBLANES:, :]
        xtail_ref[...] = x_br[tb - SUBLANES:, :]

    o_ref[...] = h + jnp.dot((hs * _gelu(y_br)).astype(BF16), wout_ref[...], preferred_element_type=F32)


def _rglru(h, h_out, row0, n_seq, t, tb, g, w_in, conv_w, conv_b, wgate, b_ra, b_ri, lam, w_out, hist8, h08):
    m, d = h.shape
    cw = lam.shape[0]
    nb = wgate.shape[0]
    nt = t // tb
    r0 = row0 // tb
    vec = lambda x: x.reshape(1, -1)
    cst = lambda shape: pl.BlockSpec(shape, lambda b, i: (0,) * len(shape))
    in_specs = [pl.BlockSpec((tb, d), lambda b, i: (r0 + b * nt + i, 0)),
                cst((1, d)), cst((d, 2 * cw)), cst((CONV_W, cw)), cst((1, cw)),
                cst(wgate.shape), cst((1, cw)), cst((1, cw)), cst((1, cw)), cst((cw, d)),
                pl.BlockSpec((None, SUBLANES, cw), lambda b, i: (b, 0, 0)),
                pl.BlockSpec((None, SUBLANES, cw), lambda b, i: (b, 0, 0))]
    args = [h, vec(g), w_in, conv_w, vec(conv_b), wgate, vec(b_ra), vec(b_ri), vec(lam), w_out, hist8, h08]
    aliases = {}
    if h_out is not None:
        in_specs.append(pl.BlockSpec(memory_space=pl.ANY))
        args.append(h_out)
        aliases = {len(args) - 1: 0}
    kern = functools.partial(_rglru_kernel, n_blocks=nb)
    if h_out is None:
        kern = lambda *refs: _rglru_kernel(*refs[:12], None, *refs[12:], n_blocks=nb)
    return pl.pallas_call(
        kern,
        grid=(n_seq, nt),
        in_specs=in_specs,
        out_specs=[pl.BlockSpec((tb, d), lambda b, i: (r0 + b * nt + i, 0)),
                   pl.BlockSpec((None, SUBLANES, cw), lambda b, i: (b, 0, 0)),
                   pl.BlockSpec((None, SUBLANES, cw), lambda b, i: (b, 0, 0))],
        out_shape=[jax.ShapeDtypeStruct((m, d), F32),
                   jax.ShapeDtypeStruct((n_seq, SUBLANES, cw), F32),
                   jax.ShapeDtypeStruct((n_seq, SUBLANES, cw), F32)],
        scratch_shapes=[pltpu.VMEM((SUBLANES, cw), F32), pltpu.VMEM((SUBLANES, cw), F32)],
        input_output_aliases=aliases,
        compiler_params=_cp("parallel", "arbitrary"),
        name="rglru",
    )(*args)


def _gmlp_kernel(h_ref, g_ref, win_ref, bin_ref, lng_ref, lnb_ref, ws_ref, bs_ref, wout_ref, o_ref, vn_ref,
                 *, n_groups, first_sample_tile, lc_prompt, lc_sample):
    i = pl.program_id(0)
    tm, d = h_ref.shape
    half = lng_ref.shape[1]
    gw = half // n_groups
    h = h_ref[...]
    act = _gelu(jnp.dot(_rms(h, g_ref[...]).astype(BF16), win_ref[...], preferred_element_type=F32) + bin_ref[...])
    u = act[:, :half]
    v = act[:, half:]
    vc = v - jnp.mean(v, axis=-1, keepdims=True)
    vn = vc * lax.rsqrt(jnp.mean(vc * vc, axis=-1, keepdims=True) + EPS) * lng_ref[...] + lnb_ref[...]
    vn_ref[...] = vn
    shift = jnp.where(i >= first_sample_tile, lc_sample.bit_length() - 1, lc_prompt.bit_length() - 1)
    ri = lax.broadcasted_iota(jnp.int32, (tm, tm), 0)
    ci = lax.broadcasted_iota(jnp.int32, (tm, tm), 1)
    keep = (lax.shift_right_logical(ri, shift) == lax.shift_right_logical(ci, shift)) & (ri >= ci)
    vnb = vn.astype(BF16)
    parts = []
    for gi in range(n_groups):
        ws = jnp.where(keep, ws_ref[0, gi], 0.0).astype(BF16)
        sg = jnp.dot(ws, vnb[:, gi * gw:(gi + 1) * gw], preferred_element_type=F32)
        parts.append(u[:, gi * gw:(gi + 1) * gw] * (sg + bs_ref[0, :, gi * gw:(gi + 1) * gw]))
    gated = jnp.concatenate(parts, axis=1).astype(BF16)
    o_ref[...] = h + jnp.dot(gated, wout_ref[...], preferred_element_type=F32)


def _gmlp(h, m_p, t_s, g, w_in, b_in, ln_g, ln_b, w_s, b_s, w_out):
    m, d = h.shape
    half = ln_g.shape[0]
    ng = w_s.shape[0]
    gw = half // ng
    tm = 256
    lc_p, lc_s = D_CHUNK, min(t_s, D_CHUNK)
    assert m_p % tm == 0 and (m - m_p) % tm == 0 and tm % lc_p == 0 and tm % lc_s == 0

    def tiled(lc):
        wt = jnp.tile(w_s[:, :lc, :lc], (1, tm // lc, tm // lc))
        bt = jnp.repeat(jnp.tile(b_s[:, :lc].T, (tm // lc, 1)), gw, axis=1)
        return wt, bt

    wt_p, bt_p = tiled(lc_p)
    wt_s, bt_s = tiled(lc_s)
    ws_all = jnp.stack([wt_p, wt_s])
    bs_all = jnp.stack([bt_p, bt_s])
    fst = m_p // tm
    vec = lambda x: x.reshape(1, -1)
    cst = lambda shape: pl.BlockSpec(shape, lambda i: (0,) * len(shape))
    return pl.pallas_call(
        functools.partial(_gmlp_kernel, n_groups=ng, first_sample_tile=fst, lc_prompt=lc_p, lc_sample=lc_s),
        grid=(m // tm,),
        in_specs=[pl.BlockSpec((tm, d), lambda i: (i, 0)),
                  cst((1, d)), cst((d, 2 * half)), cst((1, 2 * half)), cst((1, half)), cst((1, half)),
                  pl.BlockSpec((1, ng, tm, tm), lambda i: (i // fst, 0, 0, 0)),
                  pl.BlockSpec((1, tm, half), lambda i: (i // fst, 0, 0)),
                  cst((half, d))],
        out_specs=[pl.BlockSpec((tm, d), lambda i: (i, 0)),
                   pl.BlockSpec((tm, half), lambda i: (jnp.maximum(i - fst, 0), 0))],
        out_shape=[jax.ShapeDtypeStruct((m, d), F32),
                   jax.ShapeDtypeStruct((m - m_p, half), F32)],
        compiler_params=_cp("arbitrary"),
        name="gmlp",
    )(h, vec(g), w_in, vec(b_in), vec(ln_g), vec(ln_b), ws_all, bs_all, w_out)


def _pad_rows8(x3):
    n, r, c = x3.shape
    return jnp.concatenate([jnp.zeros((n, SUBLANES - r, c), x3.dtype), x3], axis=1)


def kernel(x_prompt, x_sample, p_prompt, p_sample, cache_k_a, cache_v_a, state_s_b, state_conv_b, state_h_c, state_conv_c, g_mix, g_ffn, g_ple, g_final, w_in_a, g_q_a, g_k_a, rel_a, w_out_a, w_in_b, conv_b, a_log_b, dt_bias_b, g_o_b, w_out_b, w_in_c, conv_c, conv_bias_c, w_ra_c, b_ra_c, w_ri_c, b_ri_c, lam_c, w_out_c, w_in_d, b_in_d, ln_g_d, ln_b_d, w_s_d, b_s_d, w_out_d, w_gate_ff, w_up_ff, w_down_ff, w_router, w_gate_e, w_up_e, w_down_e, w_ple, w_ple_gate):
    n_b, t, d = x_prompt.shape
    n_s, t_s, _ = x_sample.shape
    depth = p_prompt.shape[0]
    pd = p_prompt.shape[-1]
    m_p, m_s = n_b * t, n_s * t_s
    m = m_p + m_s
    a_heads = d // g_q_a.shape[-1]
    b_heads = a_log_b.shape[-1]
    n_exp = w_router.shape[-1]
    bf = lambda x: x.astype(BF16)

    h = jnp.concatenate([x_prompt.reshape(m_p, d), x_sample.reshape(m_s, d)], axis=0)
    pp_all = p_prompt.reshape(depth, m_p, pd)
    ps_all = p_sample.reshape(depth, m_s, pd)

    outs = {name: [] for name in ("k_a_p", "v_a_p", "k_a_s", "v_a_s", "s_b_p", "conv_b_p", "s_b_s", "conv_b_s",
                                  "h_c_p", "conv_c_p", "h_c_s", "conv_c_s", "v_d_s")}

    def tail_rows(x2, n_seq, tt, row0, cols):
        keep = CONV_W - 1
        if n_seq <= 4:
            return jnp.stack([x2[row0 + (b + 1) * tt - keep:row0 + (b + 1) * tt, :cols] for b in range(n_seq)])
        return x2[row0:row0 + n_seq * tt, :cols].reshape(n_seq, tt, cols)[:, tt - keep:]

    y = None
    for li in range(depth):
        kind, c = li % 4, li // 4
        if kind == 0:
            qkv = _proj(h, g_mix[li], bf(w_in_a[c]), 1024)
            w_keep = cache_k_a.shape[2]
            o, kn_p, kn_s = _attention(qkv, n_b, t, n_s, t_s,
                                       cache_k_a[c].reshape(n_s, w_keep, d), cache_v_a[c].reshape(n_s, w_keep, d),
                                       rel_a[c], g_q_a[c], g_k_a[c], a_heads)
            hd = d // a_heads
            keep = min(A_PREV_CHUNKS * CHUNK, t)
            v_p = jnp.stack([qkv[(b + 1) * t - keep:(b + 1) * t, 2 * d:] for b in range(n_b)])
            outs["k_a_p"].append(kn_p.reshape(n_b, keep, a_heads, hd))
            outs["v_a_p"].append(v_p.reshape(n_b, keep, a_heads, hd))
            outs["k_a_s"].append(kn_s.reshape(n_s, t_s, a_heads, hd))
            outs["v_a_s"].append(qkv[m_p:, 2 * d:].reshape(n_s, t_s, a_heads, hd))
            h = _outproj_res(h, o, bf(w_out_a[c]))
        elif kind == 1:
            w_in = w_in_b[c]
            nq = 3 * d
            w_main = bf(w_in[:, :nq + d])
            w_aux = bf(jnp.pad(w_in[:, nq + d:], ((0, 0), (0, LANES - 2 * b_heads))))
            proj, aux = _proj(h, g_mix[li], w_main, 1024, w_aux)
            pvec = jnp.zeros((2, LANES), F32)
            pvec = pvec.at[0, b_heads:2 * b_heads].set(a_log_b[c]).at[1, b_heads:2 * b_heads].set(dt_bias_b[c])
            hd = d // b_heads
            act_p, bg_p = _gdn_act(proj, aux, 0, n_b, t, 256, jnp.zeros((n_b, SUBLANES, nq), F32),
                                   conv_b[c], pvec, b_heads)
            o_p, s_p = _gdn_delta(act_p, bg_p, proj, 3, n_b, t,
                                  jnp.zeros((n_b, b_heads, hd, hd), F32), g_o_b[c], b_heads, out_rows=m)
            act_s, bg_s = _gdn_act(proj, aux, m_p, n_s, t_s, t_s, _pad_rows8(state_conv_b[c]),
                                   conv_b[c], pvec, b_heads)
            tp = -(-t_s // CHUNK) * CHUNK
            padt = lambda x2: jnp.pad(x2.reshape(n_s, t_s, -1), ((0, 0), (0, tp - t_s), (0, 0))).reshape(n_s * tp, -1)
            z_s = padt(proj[m_p:, nq:nq + d])
            o_s, s_s = _gdn_delta(padt(act_s), padt(bg_s), z_s, 0, n_s, tp, state_s_b[c], g_o_b[c], b_heads)
            o_s = o_s.reshape(n_s, tp, d)[:, :t_s].reshape(m_s, d)
            outs["s_b_p"].append(s_p)
            outs["s_b_s"].append(s_s)
            outs["conv_b_p"].append(tail_rows(proj, n_b, t, 0, nq))
            outs["conv_b_s"].append(tail_rows(proj, n_s, t_s, m_p, nq))
            h = _outproj_res(h, lax.dynamic_update_slice(o_p, o_s, (m_p, 0)), bf(w_out_b[c]))
        elif kind == 2:
            cw = lam_c.shape[-1]
            wgate = bf(jnp.concatenate([w_ra_c[c], w_ri_c[c]], axis=-1))
            common = (g_mix[li], bf(w_in_c[c]), conv_c[c], conv_bias_c[c], wgate, b_ra_c[c], b_ri_c[c],
                      lam_c[c], bf(w_out_c[c]))
            zeros8 = jnp.zeros((n_b, SUBLANES, cw), F32)
            h_new, hl_p, xt_p = _rglru(h, None, 0, n_b, t, 256, *common, zeros8, zeros8)
            h_new, hl_s, xt_s = _rglru(h, h_new, m_p, n_s, t_s, t_s, *common,
                                       _pad_rows8(state_conv_c[c]), _pad_rows8(state_h_c[c][:, None, :]))
            h = h_new
            outs["h_c_p"].append(hl_p[:, SUBLANES - 1])
            outs["h_c_s"].append(hl_s[:, SUBLANES - 1])
            outs["conv_c_p"].append(xt_p[:, SUBLANES - (CONV_W - 1):])
            outs["conv_c_s"].append(xt_s[:, SUBLANES - (CONV_W - 1):])
        else:
            h, vn_s = _gmlp(h, m_p, t_s, g_mix[li], bf(w_in_d[c]), b_in_d[c], ln_g_d[c], ln_b_d[c],
                            w_s_d[c], b_s_d[c], bf(w_out_d[c]))
            outs["v_d_s"].append(vn_s.reshape(n_s, t_s, -1))

        j = li // 2
        if li % 2 == 0:
            h = _ffn_dense(h, g_ffn[li], bf(w_gate_ff[j]), bf(w_up_ff[j]), bf(w_down_ff[j]))
        else:
            wr = jnp.pad(w_router[j], ((0, 0), (0, LANES - n_exp)))
            h = _moe(h, g_ffn[li], wr, bf(w_gate_e[j]), bf(w_up_e[j]), bf(w_down_e[j]))
        last = li == depth - 1
        res = _ple(h, pp_all[li], ps_all[li], g_ple[li], bf(w_ple[li]), bf(w_ple_gate[li]),
                   g_final if last else None)
        if last:
            y = res
        else:
            h = res

    st = {name: jnp.stack(rows) for name, rows in outs.items()}
    return (y[0].reshape(n_b, t, d), y[1].reshape(n_s, t_s, d),
            st["k_a_p"], st["v_a_p"], st["k_a_s"], st["v_a_s"],
            st["s_b_p"], st["conv_b_p"], st["s_b_s"], st["conv_b_s"],
            st["h_c_p"], st["conv_c_p"], st["h_c_s"], st["conv_c_s"],
            st["v_d_s"])
```

```python
import functools

import jax
import jax.numpy as jnp
from jax import lax
from jax.experimental import pallas as pl
from jax.experimental.pallas import tpu as pltpu

F32 = jnp.float32
BF16 = jnp.bfloat16
EPS = 1e-6
CHUNK = 64
CONV_W = 4
A_REL_CLIP = 128
A_PREV_CHUNKS = 8
LRU_C = 8.0
D_CHUNK = 128
TOP_K = 2
LANES = 128
SUBLANES = 8
VMEM_LIMIT = 48 * 1024 * 1024


def _cp(*sem, vmem=VMEM_LIMIT):
    return pltpu.CompilerParams(dimension_semantics=sem, vmem_limit_bytes=vmem)


def _tile_m(m):
    for t in (1024, 512, 256):
        if m % t == 0:
            return t
    raise ValueError(f"token count {m} must be a multiple of 256")


def _mm(a, b):
    return jnp.dot(a.astype(BF16), b.astype(BF16), preferred_element_type=F32)


def _mm_nt(a, b):
    return lax.dot_general(a.astype(BF16), b.astype(BF16), (((1,), (1,)), ((), ())),
                           preferred_element_type=F32)


def _mm_tn(a, b):
    return lax.dot_general(a.astype(BF16), b.astype(BF16), (((0,), (0,)), ((), ())),
                           preferred_element_type=F32)


def _split2(x):
    hi = x.astype(BF16)
    lo = (x - hi.astype(F32)).astype(BF16)
    return hi, lo


def _split3(x):
    hi = x.astype(BF16)
    r = x - hi.astype(F32)
    mid = r.astype(BF16)
    lo = (r - mid.astype(F32)).astype(BF16)
    return hi, mid, lo


def _mm3(a, b):
    ah, al = _split2(a)
    bh, bl = _split2(b)
    d = functools.partial(jnp.dot, preferred_element_type=F32)
    return d(ah, bh) + (d(ah, bl) + d(al, bh))


def _rms(x, g):
    return x * lax.rsqrt(jnp.mean(x * x, axis=-1, keepdims=True) + EPS) * g


def _sigmoid(x):
    return 0.5 * (1.0 + jnp.tanh(0.5 * x))


def _silu(x):
    return x * _sigmoid(x)


def _softplus(x):
    return jnp.maximum(x, 0.0) + jnp.log1p(jnp.exp(-jnp.abs(x)))


def _gelu(x):
    c = 0.7978845608028654
    return x * (0.5 * (1.0 + jnp.tanh(c * (x + 0.044715 * (x * x * x)))))


def _shift_rows(x, prev8, s):
    r = pltpu.roll(x, s, axis=0)
    pr = pltpu.roll(prev8, s, axis=0)
    row8 = lax.broadcasted_iota(jnp.int32, pr.shape, 0)
    top = jnp.where(row8 < s, pr, r[0:SUBLANES])
    return jnp.concatenate([top, r[SUBLANES:]], axis=0)


def _dwconv(x, prev8, w):
    y = None
    for k in range(CONV_W):
        s = CONV_W - 1 - k
        xs = x if s == 0 else _shift_rows(x, prev8, s)
        term = xs * w[k:k + 1]
        y = term if y is None else y + term
    return y


def _proj_kernel(*refs, has_aux):
    if has_aux:
        x_ref, g_ref, w_ref, waux_ref, o_ref, oaux_ref, hn_ref = refs
    else:
        x_ref, g_ref, w_ref, o_ref, hn_ref = refs

    @pl.when(pl.program_id(1) == 0)
    def _():
        hn = _rms(x_ref[...], g_ref[...]).astype(BF16)
        hn_ref[...] = hn
        if has_aux:
            oaux_ref[...] = jnp.dot(hn, waux_ref[...], preferred_element_type=F32)

    o_ref[...] = jnp.dot(hn_ref[...], w_ref[...], preferred_element_type=F32)


def _proj(h, g, w, tn, waux=None):
    m, k = h.shape
    n = w.shape[1]
    tm = _tile_m(m)
    in_specs = [pl.BlockSpec((tm, k), lambda i, j: (i, 0)),
                pl.BlockSpec((1, k), lambda i, j: (0, 0)),
                pl.BlockSpec((k, tn), lambda i, j: (0, j))]
    out_shape = [jax.ShapeDtypeStruct((m, n), F32)]
    out_specs = [pl.BlockSpec((tm, tn), lambda i, j: (i, j))]
    args = [h, g.reshape(1, k), w]
    if waux is not None:
        na = waux.shape[1]
        in_specs.append(pl.BlockSpec((k, na), lambda i, j: (0, 0)))
        out_shape.append(jax.ShapeDtypeStruct((m, na), F32))
        out_specs.append(pl.BlockSpec((tm, na), lambda i, j: (i, 0)))
        args.append(waux)
    res = pl.pallas_call(
        functools.partial(_proj_kernel, has_aux=waux is not None),
        grid=(m // tm, n // tn),
        in_specs=in_specs, out_specs=out_specs, out_shape=out_shape,
        scratch_shapes=[pltpu.VMEM((tm, k), BF16)],
        compiler_params=_cp("parallel", "arbitrary"),
        name="proj",
    )(*args)
    return res if waux is not None else res[0]


def _outproj_kernel(h_ref, o_ref, w_ref, out_ref):
    out_ref[...] = h_ref[...] + jnp.dot(o_ref[...].astype(BF16), w_ref[...], preferred_element_type=F32)


def _outproj_res(h, o, w):
    m, d = h.shape
    k = o.shape[1]
    tm = _tile_m(m)
    return pl.pallas_call(
        _outproj_kernel,
        grid=(m // tm,),
        in_specs=[pl.BlockSpec((tm, d), lambda i: (i, 0)),
                  pl.BlockSpec((tm, k), lambda i: (i, 0)),
                  pl.BlockSpec((k, d), lambda i: (0, 0))],
        out_specs=pl.BlockSpec((tm, d), lambda i: (i, 0)),
        out_shape=jax.ShapeDtypeStruct((m, d), F32),
        compiler_params=_cp("parallel"),
        name="outproj_res",
    )(h, o, w)


def _ffn_kernel(x_ref, g_ref, wg_ref, wu_ref, wd_ref, o_ref, hn_ref):
    @pl.when(pl.program_id(1) == 0)
    def _():
        x = x_ref[...]
        hn_ref[...] = _rms(x, g_ref[...]).astype(BF16)
        o_ref[...] = x

    hn = hn_ref[...]
    gg = jnp.dot(hn, wg_ref[...], preferred_element_type=F32)
    uu = jnp.dot(hn, wu_ref[...], preferred_element_type=F32)
    hid = (_silu(gg) * uu).astype(BF16)
    o_ref[...] += jnp.dot(hid, wd_ref[...], preferred_element_type=F32)


def _ffn_dense(h, g, wg, wu, wd):
    m, d = h.shape
    f = wg.shape[1]
    tm = 512
    tf = f // 2 if (f // 2) % LANES == 0 else 256
    return pl.pallas_call(
        _ffn_kernel,
        grid=(m // tm, f // tf),
        in_specs=[pl.BlockSpec((tm, d), lambda i, j: (i, 0)),
                  pl.BlockSpec((1, d), lambda i, j: (0, 0)),
                  pl.BlockSpec((d, tf), lambda i, j: (0, j)),
                  pl.BlockSpec((d, tf), lambda i, j: (0, j)),
                  pl.BlockSpec((tf, d), lambda i, j: (j, 0))],
        out_specs=pl.BlockSpec((tm, d), lambda i, j: (i, 0)),
        out_shape=jax.ShapeDtypeStruct((m, d), F32),
        scratch_shapes=[pltpu.VMEM((tm, d), BF16)],
        compiler_params=_cp("parallel", "arbitrary"),
        name="ffn_dense",
    )(h, g.reshape(1, d), wg, wu, wd)


def _moe_kernel(x_ref, g_ref, wr_ref, wg_ref, wu_ref, wd_ref, o_ref, hn_ref, gates_ref, rank_ref, rankt_ref,
                *, n_experts, pass_rows):
    e = pl.program_id(1)
    n_sub, _, sub = rankt_ref.shape

    @pl.when(e == 0)
    def _():
        ri = lax.broadcasted_iota(jnp.int32, (sub, sub), 0)
        ci = lax.broadcasted_iota(jnp.int32, (sub, sub), 1)
        tril = jnp.where(ri >= ci, 1.0, 0.0).astype(BF16)
        lane = lax.broadcasted_iota(jnp.int32, (sub, LANES), 1)
        for s in range(n_sub):
            rs = slice(s * sub, (s + 1) * sub)
            x = x_ref[rs, :]
            hn = _rms(x, g_ref[...])
            hn_ref[rs, :] = hn.astype(BF16)
            o_ref[rs, :] = x
            logits = _mm3(hn, wr_ref[...])
            lg = jnp.where(lane < n_experts, logits, -jnp.inf)
            m1 = jnp.max(lg, axis=-1, keepdims=True)
            i1 = jnp.min(jnp.where(lg == m1, lane, LANES), axis=-1, keepdims=True)
            lg2 = jnp.where(lane == i1, -jnp.inf, lg)
            m2 = jnp.max(lg2, axis=-1, keepdims=True)
            i2 = jnp.min(jnp.where(lg2 == m2, lane, LANES), axis=-1, keepdims=True)
            e2 = jnp.exp(m2 - m1)
            w1 = 1.0 / (1.0 + e2)
            w2 = e2 / (1.0 + e2)
            gates_ref[rs, :] = jnp.where(lane == i1, w1, 0.0) + jnp.where(lane == i2, w2, 0.0)
            sel = jnp.where(lane == i1, 1.0, jnp.where(lane == i2, 1.0, 0.0))
            count = jnp.dot(tril, sel.astype(BF16), preferred_element_type=F32)
            rank = jnp.where(sel > 0.0, count - 1.0, -1.0)
            rank_ref[rs, :] = rank
            rankt_ref[s] = rank.T

    lane = lax.broadcasted_iota(jnp.int32, (sub, LANES), 1)
    for s in range(n_sub):
        rs = slice(s * sub, (s + 1) * sub)
        rank_row = rankt_ref[s, pl.ds(e, 1), :]
        n_routed = (jnp.max(rank_row) + 1.0).astype(jnp.int32)
        rank_col = jnp.sum(jnp.where(lane == e, rank_ref[rs, :], 0.0), axis=-1, keepdims=True)
        gate_col = jnp.sum(jnp.where(lane == e, gates_ref[rs, :], 0.0), axis=-1, keepdims=True)

        def one_pass(cap, base, rs=rs, rank_row=rank_row, rank_col=rank_col, gate_col=gate_col):
            row_id = lax.broadcasted_iota(jnp.int32, (cap, sub), 0).astype(F32) + base
            col_id = lax.broadcasted_iota(jnp.int32, (sub, cap), 1).astype(F32) + base
            pick = jnp.where(rank_row == row_id, 1.0, 0.0).astype(BF16)
            xe = jnp.dot(pick, hn_ref[rs, :], preferred_element_type=F32).astype(BF16)
            gg = jnp.dot(xe, wg_ref[0], preferred_element_type=F32)
            uu = jnp.dot(xe, wu_ref[0], preferred_element_type=F32)
            hid = (_silu(gg) * uu).astype(BF16)
            ye = jnp.dot(hid, wd_ref[0], preferred_element_type=F32).astype(BF16)
            put = jnp.where(rank_col == col_id, gate_col, 0.0).astype(BF16)
            o_ref[rs, :] += jnp.dot(put, ye, preferred_element_type=F32)

        small, mid, full = pass_rows
        is_small = (n_routed > 0) & (n_routed <= small)
        is_mid = (n_routed > small) & (n_routed <= mid)

        @pl.when(is_small)
        def _(one_pass=one_pass):
            one_pass(small, 0.0)

        @pl.when(is_mid)
        def _(one_pass=one_pass):
            one_pass(mid, 0.0)

        @pl.when(n_routed > mid)
        def _(one_pass=one_pass, n_routed=n_routed):
            def body(cc, carry):
                one_pass(full, (cc * full).astype(F32))
                return carry
            lax.fori_loop(0, (n_routed + full - 1) // full, body, 0)


def _moe(h, g, wr, wg, wu, wd):
    m, d = h.shape
    ne, _, ef = wg.shape
    sub = 512
    tm = 2 * sub if m % (2 * sub) == 0 else sub
    mean_rows = sub * TOP_K // ne
    pass_rows = (mean_rows, mean_rows * 3 // 2, mean_rows * 2)
    assert m % tm == 0
    return pl.pallas_call(
        functools.partial(_moe_kernel, n_experts=ne, pass_rows=pass_rows),
        grid=(m // tm, ne),
        in_specs=[pl.BlockSpec((tm, d), lambda i, e: (i, 0)),
                  pl.BlockSpec((1, d), lambda i, e: (0, 0)),
                  pl.BlockSpec((d, LANES), lambda i, e: (0, 0)),
                  pl.BlockSpec((1, d, ef), lambda i, e: (e, 0, 0)),
                  pl.BlockSpec((1, d, ef), lambda i, e: (e, 0, 0)),
                  pl.BlockSpec((1, ef, d), lambda i, e: (e, 0, 0))],
        out_specs=pl.BlockSpec((tm, d), lambda i, e: (i, 0)),
        out_shape=jax.ShapeDtypeStruct((m, d), F32),
        scratch_shapes=[pltpu.VMEM((tm, d), BF16), pltpu.VMEM((tm, LANES), F32),
                        pltpu.VMEM((tm, LANES), F32), pltpu.VMEM((tm // sub, LANES, sub), F32)],
        compiler_params=_cp("parallel", "arbitrary"),
        name="moe",
    )(h, g.reshape(1, d), wr, wg, wu, wd)


def _ple_kernel(*refs, final, n_prompt_tiles):
    if final:
        x_ref, pp_ref, ps_ref, g_ref, wp_ref, wg_ref, gf_ref, yp_ref, ys_ref = refs
    else:
        x_ref, pp_ref, ps_ref, g_ref, wp_ref, wg_ref, o_ref = refs
    i = pl.program_id(0)
    is_sample = i >= n_prompt_tiles
    x = x_ref[...]
    hn = _rms(x, g_ref[...]).astype(BF16)
    gate = _sigmoid(jnp.dot(hn, wg_ref[...], preferred_element_type=F32))
    p = jnp.where(is_sample, ps_ref[...], pp_ref[...])
    pp = jnp.dot(p.astype(BF16), wp_ref[...], preferred_element_type=F32)
    out = x + pp * gate
    if final:
        y = _rms(out, gf_ref[...])

        @pl.when(jnp.logical_not(is_sample))
        def _():
            yp_ref[...] = y

        @pl.when(is_sample)
        def _():
            ys_ref[...] = y
    else:
        o_ref[...] = out


def _ple(h, p_prompt, p_sample, layer, g, wp, wgate, g_final=None):
    m, d = h.shape
    _, m_p, pd = p_prompt.shape
    m_s = p_sample.shape[1]
    tm = _tile_m(m)
    assert m_p % tm == 0 and m_s % tm == 0
    npt, nst = m_p // tm, m_s // tm
    final = g_final is not None
    in_specs = [pl.BlockSpec((tm, d), lambda i: (i, 0)),
                pl.BlockSpec((None, tm, pd), lambda i: (layer, jnp.minimum(i, npt - 1), 0)),
                pl.BlockSpec((None, tm, pd), lambda i: (layer, jnp.maximum(i - npt, 0), 0)),
                pl.BlockSpec((1, d), lambda i: (0, 0)),
                pl.BlockSpec((pd, d), lambda i: (0, 0)),
                pl.BlockSpec((d, d), lambda i: (0, 0))]
    args = [h, p_prompt, p_sample, g.reshape(1, d), wp, wgate]
    if final:
        in_specs.append(pl.BlockSpec((1, d), lambda i: (0, 0)))
        args.append(g_final.reshape(1, d))
        out_specs = [pl.BlockSpec((tm, d), lambda i: (jnp.minimum(i, npt - 1), 0)),
                     pl.BlockSpec((tm, d), lambda i: (jnp.maximum(i - npt, 0), 0))]
        out_shape = [jax.ShapeDtypeStruct((m_p, d), F32), jax.ShapeDtypeStruct((m_s, d), F32)]
    else:
        out_specs = pl.BlockSpec((tm, d), lambda i: (i, 0))
        out_shape = jax.ShapeDtypeStruct((m, d), F32)
    return pl.pallas_call(
        functools.partial(_ple_kernel, final=final, n_prompt_tiles=npt),
        grid=(npt + nst,),
        in_specs=in_specs, out_specs=out_specs, out_shape=out_shape,
        compiler_params=_cp("arbitrary"),
        name="ple",
    )(*args)


def _pair_rms(x, g, hd):
    lane = lax.broadcasted_iota(jnp.int32, x.shape, 1)
    lo = lane < hd
    x2 = x * x
    s0 = jnp.sum(jnp.where(lo, x2, 0.0), axis=-1, keepdims=True)
    s1 = jnp.sum(jnp.where(lo, 0.0, x2), axis=-1, keepdims=True)
    ms = jnp.where(lo, s0, s1) * (1.0 / hd)
    return x * lax.rsqrt(ms + EPS) * g


def _stack_heads(q, hd):
    lane = lax.broadcasted_iota(jnp.int32, q.shape, 1)
    lo = lane < hd
    return jnp.concatenate([jnp.where(lo, q, 0.0), jnp.where(lo, 0.0, q)], axis=0).astype(BF16)


def _unstack_heads(o2, hd):
    n = o2.shape[0] // 2
    lane = lax.broadcasted_iota(jnp.int32, (n, o2.shape[1]), 1)
    return jnp.where(lane < hd, o2[:n], o2[n:])


def _attend(q2, kw, vw, bias, valid):
    s = _mm_nt(q2, kw) + bias
    if valid is not None:
        s = jnp.where(valid, s, -jnp.inf)
    mx = jnp.max(s, axis=-1, keepdims=True)
    e = jnp.exp(s - mx)
    den = jnp.sum(e, axis=-1, keepdims=True)
    return jnp.dot(e.astype(BF16), vw, preferred_element_type=F32) / den


def _attn_prompt_kernel(q_ref, k_ref, v_ref, b_ref, gq_ref, gk_ref, o_ref, kn_ref, kwin_ref, vwin_ref,
                        *, hd, scale, reach, chunk_group):
    i = pl.program_id(2)
    tq, width = q_ref.shape
    band = reach + CHUNK
    pw = 2 * hd
    lanes = [slice(p * pw, (p + 1) * pw) for p in range(width // pw)]
    qn = [_pair_rms(q_ref[:, c], gq_ref[...], hd) * scale for c in lanes]
    knc = [_pair_rms(k_ref[:, c], gk_ref[...], hd) for c in lanes]

    @pl.when(i == pl.num_programs(2) - 1)
    def _():
        for c, x in zip(lanes, knc):
            kn_ref[:, c] = x

    @pl.when(i == 0)
    def _():
        kwin_ref[0:tq, :] = jnp.zeros((tq, width), BF16)
        vwin_ref[0:tq, :] = jnp.zeros((tq, width), BF16)

    @pl.when(i > 0)
    def _():
        kwin_ref[0:tq, :] = kwin_ref[tq:2 * tq, :]
        vwin_ref[0:tq, :] = vwin_ref[tq:2 * tq, :]

    for c, x in zip(lanes, knc):
        kwin_ref[tq:2 * tq, c] = x.astype(BF16)
    vwin_ref[tq:2 * tq, :] = v_ref[...].astype(BF16)
    col = lax.broadcasted_iota(jnp.int32, (1, band), 1)
    n_chunks = tq // CHUNK
    w0s = [tq - reach + jj * CHUNK for jj in range(n_chunks)]

    def attend(first_block, items):
        q2 = [_stack_heads(qn[p][jj * CHUNK:(jj + 1) * CHUNK], hd) for p, jj in items]
        s = [_mm_nt(x, kwin_ref[w0s[jj]:w0s[jj] + band, lanes[p]]) + b_ref[p] for x, (p, jj) in zip(q2, items)]
        if first_block:
            s = [jnp.where(col + w0s[jj] >= tq, x, -jnp.inf) for x, (p, jj) in zip(s, items)]
        mx = [jnp.max(x, axis=-1, keepdims=True) for x in s]
        e = [jnp.exp(x - m) for x, m in zip(s, mx)]
        den = [jnp.sum(x, axis=-1, keepdims=True) for x in e]
        o2 = [jnp.dot(x.astype(BF16), vwin_ref[w0s[jj]:w0s[jj] + band, lanes[p]], preferred_element_type=F32)
              for x, (p, jj) in zip(e, items)]
        for x, dn, (p, jj) in zip(o2, den, items):
            o_ref[jj * CHUNK:(jj + 1) * CHUNK, lanes[p]] = _unstack_heads(x / dn, hd)

    def attend_all(first_block):
        items = [(p, jj) for jj in range(n_chunks) for p in range(len(lanes))]
        for j0 in range(0, len(items), chunk_group):
            attend(first_block, items[j0:j0 + chunk_group])

    @pl.when(i == 0)
    def _():
        attend_all(True)

    @pl.when(i > 0)
    def _():
        attend_all(False)


def _attn_sample_kernel(q_ref, k_ref, v_ref, ck_ref, cv_ref, b_ref, gq_ref, gk_ref, oin_ref, o_ref, kn_ref,
                        *, hd, scale):
    del oin_ref
    pw = 2 * hd
    pairs = [slice(p * pw, (p + 1) * pw) for p in range(q_ref.shape[1] // pw)]
    qn = [_pair_rms(q_ref[:, c], gq_ref[...], hd) * scale for c in pairs]
    kn = [_pair_rms(k_ref[:, c], gk_ref[...], hd) for c in pairs]
    for c, x in zip(pairs, kn):
        kn_ref[:, c] = x
    kcat = [jnp.concatenate([ck_ref[:, c], x], axis=0).astype(BF16) for c, x in zip(pairs, kn)]
    vcat = [jnp.concatenate([cv_ref[:, c], v_ref[:, c]], axis=0).astype(BF16) for c in pairs]
    s = [_mm_nt(_stack_heads(x, hd), kk) + b_ref[p] for p, (x, kk) in enumerate(zip(qn, kcat))]
    mx = [jnp.max(x, axis=-1, keepdims=True) for x in s]
    e = [jnp.exp(x - m) for x, m in zip(s, mx)]
    den = [jnp.sum(x, axis=-1, keepdims=True) for x in e]
    o2 = [jnp.dot(x.astype(BF16), vv, preferred_element_type=F32) for x, vv in zip(e, vcat)]
    for c, x, dn in zip(pairs, o2, den):
        o_ref[:, c] = _unstack_heads(x / dn, hd)


def _rel_bias_pairs(table, q0, nq, nk):
    nh = table.shape[0]
    span = nq + nk - 1
    dmin, dmax = q0 - (nk - 1), q0 + nq - 1
    lo, hi = max(dmin, -A_REL_CLIP), min(dmax, A_REL_CLIP)
    line = jnp.concatenate([jnp.broadcast_to(table[:, :1], (nh, lo - dmin)),
                            table[:, lo + A_REL_CLIP:hi + A_REL_CLIP + 1],
                            jnp.broadcast_to(table[:, -1:], (nh, dmax - hi))], axis=1)
    flipped = line[:, ::-1]
    period = jnp.concatenate([flipped, jnp.zeros((nh, 1), table.dtype)], axis=1)
    shifted = jnp.tile(period, (1, nq))[:, :nq * span].reshape(nh, nq, span)
    b = shifted[:, :, nq - 1:nq - 1 + nk]
    return b.reshape(nh // 2, 2 * nq, nk)


def _attention(qkv, n_b, t, n_s, t_s, cache_k, cache_v, rel, g_q, g_k, n_heads):
    m_tot, d3 = qkv.shape
    d = d3 // 3
    hd = d // n_heads
    pw = 2 * hd
    n_pairs = d // pw
    m_p = n_b * t
    reach = A_PREV_CHUNKS * CHUNK
    tq = reach
    assert pw == LANES and t % tq == 0 and t_s % 16 == 0
    nqb = t // tq
    scale = hd ** -0.5
    gq2 = jnp.tile(g_q, 2).reshape(1, pw)
    gk2 = jnp.tile(g_k, 2).reshape(1, pw)

    bias_p = _rel_bias_pairs(rel, reach, CHUNK, reach + CHUNK)
    kcol, vcol = n_pairs, 2 * n_pairs

    def rows(b, p, i):
        return b * nqb + i

    ppb = 2 if n_pairs % 2 == 0 else 1
    npb = n_pairs // ppb
    blk = (tq, ppb * pw)
    o_all, kn_p = pl.pallas_call(
        functools.partial(_attn_prompt_kernel, hd=hd, scale=scale, reach=reach, chunk_group=4),
        grid=(n_b, npb, nqb),
        in_specs=[pl.BlockSpec(blk, lambda b, p, i: (rows(b, p, i), p)),
                  pl.BlockSpec(blk, lambda b, p, i: (rows(b, p, i), npb + p)),
                  pl.BlockSpec(blk, lambda b, p, i: (rows(b, p, i), 2 * npb + p)),
                  pl.BlockSpec((ppb, 2 * CHUNK, reach + CHUNK), lambda b, p, i: (p, 0, 0)),
                  pl.BlockSpec((1, pw), lambda b, p, i: (0, 0)),
                  pl.BlockSpec((1, pw), lambda b, p, i: (0, 0))],
        out_specs=[pl.BlockSpec(blk, lambda b, p, i: (rows(b, p, i), p)),
                   pl.BlockSpec(blk, lambda b, p, i: (b, p))],
        out_shape=[jax.ShapeDtypeStruct((m_tot, d), F32),
                   jax.ShapeDtypeStruct((n_b * tq, d), F32)],
        scratch_shapes=[pltpu.VMEM((2 * tq, ppb * pw), BF16), pltpu.VMEM((2 * tq, ppb * pw), BF16)],
        compiler_params=_cp("parallel", "parallel", "arbitrary"),
        name="attn_prompt",
    )(qkv, qkv, qkv, bias_p, gq2, gk2)

    w = cache_k.shape[1]
    bias_s = _rel_bias_pairs(rel, w, t_s, w + t_s)
    srow = m_p // t_s
    sblk = (t_s, d)
    o_all, kn_s = pl.pallas_call(
        functools.partial(_attn_sample_kernel, hd=hd, scale=scale),
        grid=(n_s,),
        in_specs=[pl.BlockSpec(sblk, lambda b: (srow + b, 0)),
                  pl.BlockSpec(sblk, lambda b: (srow + b, 1)),
                  pl.BlockSpec(sblk, lambda b: (srow + b, 2)),
                  pl.BlockSpec((None, w, d), lambda b: (b, 0, 0)),
                  pl.BlockSpec((None, w, d), lambda b: (b, 0, 0)),
                  pl.BlockSpec((n_pairs, 2 * t_s, w + t_s), lambda b: (0, 0, 0)),
                  pl.BlockSpec((1, pw), lambda b: (0, 0)),
                  pl.BlockSpec((1, pw), lambda b: (0, 0)),
                  pl.BlockSpec(memory_space=pl.ANY)],
        out_specs=[pl.BlockSpec(sblk, lambda b: (srow + b, 0)),
                   pl.BlockSpec(sblk, lambda b: (b, 0))],
        out_shape=[jax.ShapeDtypeStruct((m_tot, d), F32),
                   jax.ShapeDtypeStruct((n_s * t_s, d), F32)],
        input_output_aliases={8: 0},
        compiler_params=_cp("parallel"),
        name="attn_sample",
    )(qkv, qkv, qkv, cache_k, cache_v, bias_s, gq2, gk2, o_all)
    return o_all, kn_p, kn_s


def _gdn_act_kernel(xq_ref, xk_ref, xv_ref, pq_ref, pk_ref, pv_ref, hq_ref, hk_ref, hv_ref,
                    aux_ref, cw_ref, pvec_ref, o_ref, bg_ref, *, blocks_per_seq, n_heads, scale):
    i = pl.program_id(0)
    start = (i % blocks_per_seq) == 0
    d = xq_ref.shape[1]
    hd = d // n_heads
    parts = ((xq_ref, pq_ref, hq_ref), (xk_ref, pk_ref, hk_ref), (xv_ref, pv_ref, hv_ref))
    for c, (x_ref, p_ref, h_ref) in enumerate(parts):
        prev8 = jnp.where(start, h_ref[...], p_ref[...])
        a = _silu(_dwconv(x_ref[...], prev8, cw_ref[:, c * d:(c + 1) * d]))
        for hh in range(n_heads):
            ah = a[:, hh * hd:(hh + 1) * hd]
            if c < 2:
                ah = ah * lax.rsqrt(jnp.sum(ah * ah, axis=-1, keepdims=True) + EPS)
            if c == 0:
                ah = ah * scale
            o_ref[:, c * d + hh * hd:c * d + (hh + 1) * hd] = ah
    ax = aux_ref[...]
    lane = lax.broadcasted_iota(jnp.int32, ax.shape, 1)
    beta = _sigmoid(ax)
    gdec = -jnp.exp(pvec_ref[0:1, :]) * _softplus(ax + pvec_ref[1:2, :])
    bg_ref[...] = jnp.where(lane < n_heads, beta, jnp.where(lane < 2 * n_heads, gdec, 0.0))


def _gdn_act(proj, aux, row0, n_seq, t, tb, hist8, conv_w, pvec, n_heads):
    d = proj.shape[1] // 4
    hd = d // n_heads
    bps = t // tb
    r0 = row0 // tb
    p8 = tb // SUBLANES

    def xspec(c):
        return pl.BlockSpec((tb, d), lambda i, c=c: (r0 + i, c))

    def pspec(c):
        return pl.BlockSpec((SUBLANES, d), lambda i, c=c: (jnp.maximum((r0 + i) * p8 - 1, 0), c))

    def hspec(c):
        return pl.BlockSpec((None, SUBLANES, d), lambda i, c=c: (i // bps, 0, c))

    m = n_seq * t
    return pl.pallas_call(
        functools.partial(_gdn_act_kernel, blocks_per_seq=bps, n_heads=n_heads, scale=hd ** -0.5),
        grid=(m // tb,),
        in_specs=[xspec(0), xspec(1), xspec(2), pspec(0), pspec(1), pspec(2), hspec(0), hspec(1), hspec(2),
                  pl.BlockSpec((tb, LANES), lambda i: (r0 + i, 0)),
                  pl.BlockSpec((CONV_W, 3 * d), lambda i: (0, 0)),
                  pl.BlockSpec((2, LANES), lambda i: (0, 0))],
        out_specs=[pl.BlockSpec((tb, 3 * d), lambda i: (i, 0)),
                   pl.BlockSpec((tb, LANES), lambda i: (i, 0))],
        out_shape=[jax.ShapeDtypeStruct((m, 3 * d), F32), jax.ShapeDtypeStruct((m, LANES), F32)],
        compiler_params=_cp("parallel"),
        name="gdn_act",
    )(proj, proj, proj, proj, proj, proj, hist8, hist8, hist8, aux, conv_w, pvec)


def _seg_cumsum(x, rowmod, reverse):
    n = x.shape[0]
    d = 1
    while d < CHUNK:
        if reverse:
            sh = pltpu.roll(x, n - d, axis=0)
            x = x + jnp.where(rowmod < CHUNK - d, sh, 0.0)
        else:
            sh = pltpu.roll(x, d, axis=0)
            x = x + jnp.where(rowmod >= d, sh, 0.0)
        d *= 2
    return x


def _gdn_delta_kernel(act_ref, z_ref, bg_ref, s0_ref, go_ref, o_ref, sout_ref, s_ref, *, n_heads, chained, head_group):
    i = pl.program_id(1)
    tb, d = z_ref.shape
    hd = d // n_heads
    nch = tb // CHUNK
    dot = functools.partial(jnp.dot, preferred_element_type=F32)

    if chained:
        @pl.when(i == 0)
        def _():
            s_ref[...] = s0_ref[...]

    bg = bg_ref[...]
    lane = lax.broadcasted_iota(jnp.int32, (tb, hd), 1)
    rowmod = lax.broadcasted_iota(jnp.int32, (tb, hd), 0) & (CHUNK - 1)
    ri = lax.broadcasted_iota(jnp.int32, (tb, tb), 0)
    ci = lax.broadcasted_iota(jnp.int32, (tb, tb), 1)
    same = (ri >> 6) == (ci >> 6)
    low_incl = same & (ri >= ci)
    strict = same & (ri > ci)
    eye_c = (lax.broadcasted_iota(jnp.int32, (CHUNK, tb), 0)
             == (lax.broadcasted_iota(jnp.int32, (CHUNK, tb), 1) & (CHUNK - 1))).astype(F32)
    def compact(x):
        acc = x[0:CHUNK]
        for c in range(1, nch):
            acc = acc + x[c * CHUNK:(c + 1) * CHUNK]
        return acc

    def expand(xc):
        return jnp.where(same, jnp.concatenate([xc] * nch, axis=0), jnp.zeros((), xc.dtype))

    def split_expand(xc):
        hi, lo = _split2(xc)
        return hi, lo, expand(hi), expand(lo)

    def each(fn, *lists):
        return [fn(*a) for a in zip(*lists)]

    cum_fwd = _seg_cumsum(bg, rowmod, reverse=False)
    cum_rev = _seg_cumsum(bg, rowmod, reverse=True) - bg

    def column(arr, idx):
        return jnp.broadcast_to(jnp.sum(jnp.where(lane == idx, arr, 0.0), axis=-1, keepdims=True), (tb, hd))

    def gc_diff(gc):
        gh, gm, gl = (piece.astype(F32) for piece in _split3(gc))
        lhs = jnp.where(lane == 0, gh, jnp.where(lane == 1, gm, jnp.where(lane == 2, gl,
                                                                          jnp.where(lane < 6, 1.0, 0.0))))
        rhs_t = jnp.where(lane < 3, 1.0, jnp.where(lane == 3, -gh, jnp.where(lane == 4, -gm,
                                                                             jnp.where(lane == 5, -gl, 0.0))))
        return _mm_nt(lhs, rhs_t)

    for h0 in range(0, n_heads, head_group):
        hs = list(range(h0, h0 + head_group))
        q = [act_ref[:, h * hd:(h + 1) * hd] for h in hs]
        k = [act_ref[:, d + h * hd:d + (h + 1) * hd] for h in hs]
        v = [act_ref[:, 2 * d + h * hd:2 * d + (h + 1) * hd] for h in hs]
        beta = [column(bg, h) for h in hs]
        gc = [column(cum_fwd, n_heads + h) for h in hs]
        grev = [column(cum_rev, n_heads + h) for h in hs]
        decay = each(lambda x: jnp.exp(jnp.where(low_incl, gc_diff(x), -jnp.inf)), gc)
        kb = each(lambda a, b: a * b, k, beta)
        lmat = each(lambda a, b, dc: jnp.where(strict, _mm_nt(a, b) * dc, 0.0), kb, k, decay)
        egc = each(jnp.exp, gc)
        rhs = each(lambda a, b, c_, e: jnp.concatenate([a * b, c_ * e], axis=1), v, beta, kb, egc)

        xc = each(compact, lmat)
        tinv = each(lambda x: eye_c - x, xc)
        xs = each(split_expand, xc)
        level = 0
        p = 1
        while 2 * p < CHUNK:
            if level < 2:
                xc = each(lambda s4: dot(s4[0], s4[2]) + (dot(s4[0], s4[3]) + dot(s4[1], s4[2])), xs)
                xs = each(split_expand, xc)
                ts = each(_split2, tinv)
                tinv = each(lambda t, t2, s4: t + (dot(t2[0], s4[2]) + (dot(t2[0], s4[3]) + dot(t2[1], s4[2]))),
                            tinv, ts, xs)
            else:
                xh = each(lambda s4: dot(s4[0], s4[2]).astype(BF16), xs)
                xs = each(lambda x: (x, None, expand(x), None), xh)
                tinv = each(lambda t, s4: t + dot(t.astype(BF16), s4[2]), tinv, xs)
            level += 1
            p *= 2
        ts = each(split_expand, tinv)
        rs = each(_split2, rhs)
        sol = each(lambda t4, r2: dot(t4[2], r2[0]) + (dot(t4[2], r2[1]) + dot(t4[3], r2[0])), ts, rs)

        qk = each(lambda a, b, dc: jnp.where(low_incl, _mm_nt(a, b) * dc, 0.0), q, k, decay)
        q_dec = each(lambda a, e: a * e, q, egc)
        k_dec = each(lambda a, r: a * jnp.exp(r), k, grev)
        g_last = each(lambda a, r: jnp.exp(a + r), gc, grev)

        v_news = [[] for _ in hs]
        o_inter = [[] for _ in hs]
        s = [s_ref[h] for h in hs] if chained else None
        for c in range(nch):
            rc = slice(c * CHUNK, (c + 1) * CHUNK)
            if not chained:
                s = [s0_ref[c, h] for h in hs]
            v_new = each(lambda so, st: so[rc, :hd] - _mm(so[rc, hd:], st), sol, s)
            o_c = each(lambda qd, st: _mm(qd[rc], st), q_dec, s)
            s = each(lambda st, gl, kd, vn: st * gl[c * CHUNK:c * CHUNK + 1, :] + _mm_tn(kd[rc], vn),
                     s, g_last, k_dec, v_new)
            for j, h in enumerate(hs):
                v_news[j].append(v_new[j])
                o_inter[j].append(o_c[j])
                if not chained:
                    sout_ref[c, h] = s[j]
        if chained:
            for j, h in enumerate(hs):
                s_ref[h] = s[j]

            @pl.when(i == pl.num_programs(1) - 1)
            def _(s=s, hs=hs):
                for j, h in enumerate(hs):
                    sout_ref[h] = s[j]

        o = each(lambda oi, m, vn: jnp.concatenate(oi, axis=0) + _mm(m, jnp.concatenate(vn, axis=0)),
                 o_inter, qk, v_news)
        for j, h in enumerate(hs):
            o_ref[:, h * hd:(h + 1) * hd] = _rms(o[j], go_ref[...]) * _silu(z_ref[:, h * hd:(h + 1) * hd])


def _gdn_delta(act, bg, zsrc, zcol, n_seq, t, s0, g_o, n_heads, out_rows=None):
    d = act.shape[1] // 3
    hd = d // n_heads
    tb = 4 * CHUNK
    chained = t > CHUNK
    if chained:
        assert t % tb == 0
        grid = (n_seq, t // tb)
        rows = lambda b, i: b * (t // tb) + i
        sspec = pl.BlockSpec((None, n_heads, hd, hd), lambda b, i: (b, 0, 0, 0))
    else:
        assert t == CHUNK and n_seq % (tb // CHUNK) == 0
        grid = (n_seq * t // tb, 1)
        rows = lambda b, i: b
        sspec = pl.BlockSpec((tb // CHUNK, n_heads, hd, hd), lambda b, i: (b, 0, 0, 0))
    return pl.pallas_call(
        functools.partial(_gdn_delta_kernel, n_heads=n_heads, chained=chained, head_group=8),
        grid=grid,
        in_specs=[pl.BlockSpec((tb, 3 * d), lambda b, i: (rows(b, i), 0)),
                  pl.BlockSpec((tb, d), lambda b, i: (rows(b, i), zcol)),
                  pl.BlockSpec((tb, LANES), lambda b, i: (rows(b, i), 0)),
                  sspec,
                  pl.BlockSpec((1, hd), lambda b, i: (0, 0))],
        out_specs=[pl.BlockSpec((tb, d), lambda b, i: (rows(b, i), 0)), sspec],
        out_shape=[jax.ShapeDtypeStruct((out_rows or n_seq * t, d), F32),
                   jax.ShapeDtypeStruct((n_seq, n_heads, hd, hd), F32)],
        scratch_shapes=[pltpu.VMEM((n_heads, hd, hd), F32)],
        compiler_params=_cp("parallel", "arbitrary"),
        name="gdn_delta",
    )(act, zsrc, bg, s0, g_o.reshape(1, hd))


def _rglru_kernel(h_ref, g_ref, win_ref, cw_ref, cb_ref, wgate_ref, bra_ref, bri_ref, lam_ref, wout_ref,
                  hist_ref, h0_ref, oin_ref, o_ref, hlast_ref, xtail_ref, cx_ref, ch_ref, *, n_blocks):
    del oin_ref
    i = pl.program_id(1)
    tb, d = h_ref.shape
    cw = lam_ref.shape[1]
    blk = cw // n_blocks

    @pl.when(i == 0)
    def _():
        cx_ref[...] = hist_ref[...]
        ch_ref[...] = h0_ref[...]

    h = h_ref[...]
    proj = jnp.dot(_rms(h, g_ref[...]).astype(BF16), win_ref[...], preferred_element_type=F32)
    y_br = proj[:, :cw]
    x_br = proj[:, cw:]
    xc = _dwconv(x_br, cx_ref[...], cw_ref[...]) + cb_ref[...]
    r_parts, i_parts = [], []
    for n in range(n_blocks):
        ri = jnp.dot(xc[:, n * blk:(n + 1) * blk].astype(BF16), wgate_ref[n], preferred_element_type=F32)
        r_parts.append(ri[:, :blk])
        i_parts.append(ri[:, blk:])
    r = _sigmoid(jnp.concatenate(r_parts, axis=1) + bra_ref[...])
    ig = _sigmoid(jnp.concatenate(i_parts, axis=1) + bri_ref[...])
    log_a = -LRU_C * r * _softplus(-lam_ref[...])
    a = jnp.exp(log_a)
    th = jnp.tanh(log_a)
    b = jnp.sqrt(-2.0 * th / (1.0 - th)) * (ig * xc)
    row8 = lax.broadcasted_iota(jnp.int32, (tb, cw), 0) & (SUBLANES - 1)
    step = 1
    while step < SUBLANES:
        a_s = jnp.where(row8 >= step, pltpu.roll(a, step, axis=0), 1.0)
        b_s = jnp.where(row8 >= step, pltpu.roll(b, step, axis=0), 0.0)
        b = a * b_s + b
        a = a * a_s
        step *= 2
    h_in = ch_ref[SUBLANES - 1:SUBLANES, :]
    groups = []
    for j in range(tb // SUBLANES):
        rows = slice(j * SUBLANES, (j + 1) * SUBLANES)
        hg = b[rows] + a[rows] * h_in
        groups.append(hg)
        h_in = hg[SUBLANES - 1:SUBLANES, :]
    hs = jnp.concatenate(groups, axis=0)
    cx_ref[...] = x_br[tb - SUBLANES:, :]
    ch_ref[...] = hs[tb - SUBLANES:, :]

    @pl.when(i == pl.num_programs(1) - 1)
    def _():
        hlast_ref[...] = hs[tb - SUBLANES:, :]
        xtail_ref[...] = x_br[tb - SUBLANES:, :]

    o_ref[...] = h + jnp.dot((hs * _gelu(y_br)).astype(BF16), wout_ref[...], preferred_element_type=F32)


def _rglru(h, h_out, row0, n_seq, t, tb, g, w_in, conv_w, conv_b, wgate, b_ra, b_ri, lam, w_out, hist8, h08):
    m, d = h.shape
    cw = lam.shape[0]
    nb = wgate.shape[0]
    nt = t // tb
    r0 = row0 // tb
    vec = lambda x: x.reshape(1, -1)
    cst = lambda shape: pl.BlockSpec(shape, lambda b, i: (0,) * len(shape))
    in_specs = [pl.BlockSpec((tb, d), lambda b, i: (r0 + b * nt + i, 0)),
                cst((1, d)), cst((d, 2 * cw)), cst((CONV_W, cw)), cst((1, cw)),
                cst(wgate.shape), cst((1, cw)), cst((1, cw)), cst((1, cw)), cst((cw, d)),
                pl.BlockSpec((None, SUBLANES, cw), lambda b, i: (b, 0, 0)),
                pl.BlockSpec((None, SUBLANES, cw), lambda b, i: (b, 0, 0))]
    args = [h, vec(g), w_in, conv_w, vec(conv_b), wgate, vec(b_ra), vec(b_ri), vec(lam), w_out, hist8, h08]
    aliases = {}
    if h_out is not None:
        in_specs.append(pl.BlockSpec(memory_space=pl.ANY))
        args.append(h_out)
        aliases = {len(args) - 1: 0}
    kern = functools.partial(_rglru_kernel, n_blocks=nb)
    if h_out is None:
        kern = lambda *refs: _rglru_kernel(*refs[:12], None, *refs[12:], n_blocks=nb)
    return pl.pallas_call(
        kern,
        grid=(n_seq, nt),
        in_specs=in_specs,
        out_specs=[pl.BlockSpec((tb, d), lambda b, i: (r0 + b * nt + i, 0)),
                   pl.BlockSpec((None, SUBLANES, cw), lambda b, i: (b, 0, 0)),
                   pl.BlockSpec((None, SUBLANES, cw), lambda b, i: (b, 0, 0))],
        out_shape=[jax.ShapeDtypeStruct((m, d), F32),
                   jax.ShapeDtypeStruct((n_seq, SUBLANES, cw), F32),
                   jax.ShapeDtypeStruct((n_seq, SUBLANES, cw), F32)],
        scratch_shapes=[pltpu.VMEM((SUBLANES, cw), F32), pltpu.VMEM((SUBLANES, cw), F32)],
        input_output_aliases=aliases,
        compiler_params=_cp("parallel", "arbitrary"),
        name="rglru",
    )(*args)


def _gmlp_kernel(h_ref, g_ref, win_ref, bin_ref, lng_ref, lnb_ref, ws_ref, bs_ref, wout_ref, o_ref, vn_ref,
                 *, n_groups, first_sample_tile, lc_prompt, lc_sample):
    i = pl.program_id(0)
    tm, d = h_ref.shape
    half = lng_ref.shape[1]
    gw = half // n_groups
    h = h_ref[...]
    act = _gelu(jnp.dot(_rms(h, g_ref[...]).astype(BF16), win_ref[...], preferred_element_type=F32) + bin_ref[...])
    u = act[:, :half]
    v = act[:, half:]
    vc = v - jnp.mean(v, axis=-1, keepdims=True)
    vn = vc * lax.rsqrt(jnp.mean(vc * vc, axis=-1, keepdims=True) + EPS) * lng_ref[...] + lnb_ref[...]
    vn_ref[...] = vn
    shift = jnp.where(i >= first_sample_tile, lc_sample.bit_length() - 1, lc_prompt.bit_length() - 1)
    ri = lax.broadcasted_iota(jnp.int32, (tm, tm), 0)
    ci = lax.broadcasted_iota(jnp.int32, (tm, tm), 1)
    keep = (lax.shift_right_logical(ri, shift) == lax.shift_right_logical(ci, shift)) & (ri >= ci)
    vnb = vn.astype(BF16)
    parts = []
    for gi in range(n_groups):
        ws = jnp.where(keep, ws_ref[0, gi], 0.0).astype(BF16)
        sg = jnp.dot(ws, vnb[:, gi * gw:(gi + 1) * gw], preferred_element_type=F32)
        parts.append(u[:, gi * gw:(gi + 1) * gw] * (sg + bs_ref[0, :, gi * gw:(gi + 1) * gw]))
    gated = jnp.concatenate(parts, axis=1).astype(BF16)
    o_ref[...] = h + jnp.dot(gated, wout_ref[...], preferred_element_type=F32)


def _gmlp(h, m_p, t_s, g, w_in, b_in, ln_g, ln_b, w_s, b_s, w_out):
    m, d = h.shape
    half = ln_g.shape[0]
    ng = w_s.shape[0]
    gw = half // ng
    tm = 256
    lc_p, lc_s = D_CHUNK, min(t_s, D_CHUNK)
    assert m_p % tm == 0 and (m - m_p) % tm == 0 and tm % lc_p == 0 and tm % lc_s == 0

    def tiled(lc):
        wt = jnp.tile(w_s[:, :lc, :lc], (1, tm // lc, tm // lc))
        bt = jnp.repeat(jnp.tile(b_s[:, :lc].T, (tm // lc, 1)), gw, axis=1)
        return wt, bt

    wt_p, bt_p = tiled(lc_p)
    wt_s, bt_s = tiled(lc_s)
    ws_all = jnp.stack([wt_p, wt_s])
    bs_all = jnp.stack([bt_p, bt_s])
    fst = m_p // tm
    vec = lambda x: x.reshape(1, -1)
    cst = lambda shape: pl.BlockSpec(shape, lambda i: (0,) * len(shape))
    return pl.pallas_call(
        functools.partial(_gmlp_kernel, n_groups=ng, first_sample_tile=fst, lc_prompt=lc_p, lc_sample=lc_s),
        grid=(m // tm,),
        in_specs=[pl.BlockSpec((tm, d), lambda i: (i, 0)),
                  cst((1, d)), cst((d, 2 * half)), cst((1, 2 * half)), cst((1, half)), cst((1, half)),
                  pl.BlockSpec((1, ng, tm, tm), lambda i: (i // fst, 0, 0, 0)),
                  pl.BlockSpec((1, tm, half), lambda i: (i // fst, 0, 0)),
                  cst((half, d))],
        out_specs=[pl.BlockSpec((tm, d), lambda i: (i, 0)),
                   pl.BlockSpec((tm, half), lambda i: (jnp.maximum(i - fst, 0), 0))],
        out_shape=[jax.ShapeDtypeStruct((m, d), F32),
                   jax.ShapeDtypeStruct((m - m_p, half), F32)],
        compiler_params=_cp("arbitrary"),
        name="gmlp",
    )(h, vec(g), w_in, vec(b_in), vec(ln_g), vec(ln_b), ws_all, bs_all, w_out)


def _pad_rows8(x3):
    n, r, c = x3.shape
    return jnp.concatenate([jnp.zeros((n, SUBLANES - r, c), x3.dtype), x3], axis=1)


def kernel(x_prompt, x_sample, p_prompt, p_sample, cache_k_a, cache_v_a, state_s_b, state_conv_b, state_h_c, state_conv_c, g_mix, g_ffn, g_ple, g_final, w_in_a, g_q_a, g_k_a, rel_a, w_out_a, w_in_b, conv_b, a_log_b, dt_bias_b, g_o_b, w_out_b, w_in_c, conv_c, conv_bias_c, w_ra_c, b_ra_c, w_ri_c, b_ri_c, lam_c, w_out_c, w_in_d, b_in_d, ln_g_d, ln_b_d, w_s_d, b_s_d, w_out_d, w_gate_ff, w_up_ff, w_down_ff, w_router, w_gate_e, w_up_e, w_down_e, w_ple, w_ple_gate):
    n_b, t, d = x_prompt.shape
    n_s, t_s, _ = x_sample.shape
    depth = p_prompt.shape[0]
    pd = p_prompt.shape[-1]
    m_p, m_s = n_b * t, n_s * t_s
    m = m_p + m_s
    a_heads = d // g_q_a.shape[-1]
    b_heads = a_log_b.shape[-1]
    n_exp = w_router.shape[-1]
    bf = lambda x: x.astype(BF16)

    h = jnp.concatenate([x_prompt.reshape(m_p, d), x_sample.reshape(m_s, d)], axis=0)
    pp_all = p_prompt.reshape(depth, m_p, pd)
    ps_all = p_sample.reshape(depth, m_s, pd)

    outs = {name: [] for name in ("k_a_p", "v_a_p", "k_a_s", "v_a_s", "s_b_p", "conv_b_p", "s_b_s", "conv_b_s",
                                  "h_c_p", "conv_c_p", "h_c_s", "conv_c_s", "v_d_s")}

    def tail_rows(x2, n_seq, tt, row0, cols):
        keep = CONV_W - 1
        if n_seq <= 4:
            return jnp.stack([x2[row0 + (b + 1) * tt - keep:row0 + (b + 1) * tt, :cols] for b in range(n_seq)])
        return x2[row0:row0 + n_seq * tt, :cols].reshape(n_seq, tt, cols)[:, tt - keep:]

    y = None
    for li in range(depth):
        kind, c = li % 4, li // 4
        if kind == 0:
            qkv = _proj(h, g_mix[li], bf(w_in_a[c]), 1024)
            w_keep = cache_k_a.shape[2]
            o, kn_p, kn_s = _attention(qkv, n_b, t, n_s, t_s,
                                       cache_k_a[c].reshape(n_s, w_keep, d), cache_v_a[c].reshape(n_s, w_keep, d),
                                       rel_a[c], g_q_a[c], g_k_a[c], a_heads)
            hd = d // a_heads
            keep = min(A_PREV_CHUNKS * CHUNK, t)
            v_p = jnp.stack([qkv[(b + 1) * t - keep:(b + 1) * t, 2 * d:] for b in range(n_b)])
            outs["k_a_p"].append(kn_p.reshape(n_b, keep, a_heads, hd))
            outs["v_a_p"].append(v_p.reshape(n_b, keep, a_heads, hd))
            outs["k_a_s"].append(kn_s.reshape(n_s, t_s, a_heads, hd))
            outs["v_a_s"].append(qkv[m_p:, 2 * d:].reshape(n_s, t_s, a_heads, hd))
            h = _outproj_res(h, o, bf(w_out_a[c]))
        elif kind == 1:
            w_in = w_in_b[c]
            nq = 3 * d
            w_main = bf(w_in[:, :nq + d])
            w_aux = bf(jnp.pad(w_in[:, nq + d:], ((0, 0), (0, LANES - 2 * b_heads))))
            proj, aux = _proj(h, g_mix[li], w_main, 1024, w_aux)
            pvec = jnp.zeros((2, LANES), F32)
            pvec = pvec.at[0, b_heads:2 * b_heads].set(a_log_b[c]).at[1, b_heads:2 * b_heads].set(dt_bias_b[c])
            hd = d // b_heads
            act_p, bg_p = _gdn_act(proj, aux, 0, n_b, t, 256, jnp.zeros((n_b, SUBLANES, nq), F32),
                                   conv_b[c], pvec, b_heads)
            o_p, s_p = _gdn_delta(act_p, bg_p, proj, 3, n_b, t,
                                  jnp.zeros((n_b, b_heads, hd, hd), F32), g_o_b[c], b_heads, out_rows=m)
            act_s, bg_s = _gdn_act(proj, aux, m_p, n_s, t_s, t_s, _pad_rows8(state_conv_b[c]),
                                   conv_b[c], pvec, b_heads)
            tp = -(-t_s // CHUNK) * CHUNK
            padt = lambda x2: jnp.pad(x2.reshape(n_s, t_s, -1), ((0, 0), (0, tp - t_s), (0, 0))).reshape(n_s * tp, -1)
            z_s = padt(proj[m_p:, nq:nq + d])
            o_s, s_s = _gdn_delta(padt(act_s), padt(bg_s), z_s, 0, n_s, tp, state_s_b[c], g_o_b[c], b_heads)
            o_s = o_s.reshape(n_s, tp, d)[:, :t_s].reshape(m_s, d)
            outs["s_b_p"].append(s_p)
            outs["s_b_s"].append(s_s)
            outs["conv_b_p"].append(tail_rows(proj, n_b, t, 0, nq))
            outs["conv_b_s"].append(tail_rows(proj, n_s, t_s, m_p, nq))
            h = _outproj_res(h, lax.dynamic_update_slice(o_p, o_s, (m_p, 0)), bf(w_out_b[c]))
        elif kind == 2:
            cw = lam_c.shape[-1]
            wgate = bf(jnp.concatenate([w_ra_c[c], w_ri_c[c]], axis=-1))
            common = (g_mix[li], bf(w_in_c[c]), conv_c[c], conv_bias_c[c], wgate, b_ra_c[c], b_ri_c[c],
                      lam_c[c], bf(w_out_c[c]))
            zeros8 = jnp.zeros((n_b, SUBLANES, cw), F32)
            h_new, hl_p, xt_p = _rglru(h, None, 0, n_b, t, 256, *common, zeros8, zeros8)
            h_new, hl_s, xt_s = _rglru(h, h_new, m_p, n_s, t_s, t_s, *common,
                                       _pad_rows8(state_conv_c[c]), _pad_rows8(state_h_c[c][:, None, :]))
            h = h_new
            outs["h_c_p"].append(hl_p[:, SUBLANES - 1])
            outs["h_c_s"].append(hl_s[:, SUBLANES - 1])
            outs["conv_c_p"].append(xt_p[:, SUBLANES - (CONV_W - 1):])
            outs["conv_c_s"].append(xt_s[:, SUBLANES - (CONV_W - 1):])
        else:
            h, vn_s = _gmlp(h, m_p, t_s, g_mix[li], bf(w_in_d[c]), b_in_d[c], ln_g_d[c], ln_b_d[c],
                            w_s_d[c], b_s_d[c], bf(w_out_d[c]))
            outs["v_d_s"].append(vn_s.reshape(n_s, t_s, -1))

        j = li // 2
        if li % 2 == 0:
            h = _ffn_dense(h, g_ffn[li], bf(w_gate_ff[j]), bf(w_up_ff[j]), bf(w_down_ff[j]))
        else:
            wr = jnp.pad(w_router[j], ((0, 0), (0, LANES - n_exp)))
            h = _moe(h, g_ffn[li], wr, bf(w_gate_e[j]), bf(w_up_e[j]), bf(w_down_e[j]))
        last = li == depth - 1
        res = _ple(h, pp_all, ps_all, li, g_ple[li], bf(w_ple[li]), bf(w_ple_gate[li]),
                   g_final if last else None)
        if last:
            y = res
        else:
            h = res

    st = {name: jnp.stack(rows) for name, rows in outs.items()}
    return (y[0].reshape(n_b, t, d), y[1].reshape(n_s, t_s, d),
            st["k_a_p"], st["v_a_p"], st["k_a_s"], st["v_a_s"],
            st["s_b_p"], st["conv_b_p"], st["s_b_s"], st["conv_b_s"],
            st["h_c_p"], st["conv_c_p"], st["h_c_s"], st["conv_c_s"],
            st["v_d_s"])
```

```python
import functools

import jax
import jax.numpy as jnp
from jax import lax
from jax.experimental import pallas as pl
from jax.experimental.pallas import tpu as pltpu

F32 = jnp.float32
BF16 = jnp.bfloat16
EPS = 1e-6
CHUNK = 64
CONV_W = 4
A_REL_CLIP = 128
A_PREV_CHUNKS = 8
LRU_C = 8.0
D_CHUNK = 128
TOP_K = 2
LANES = 128
SUBLANES = 8
VMEM_LIMIT = 48 * 1024 * 1024


def _cp(*sem, vmem=VMEM_LIMIT):
    return pltpu.CompilerParams(dimension_semantics=sem, vmem_limit_bytes=vmem)


def _tile_m(m):
    for t in (1024, 512, 256):
        if m % t == 0:
            return t
    raise ValueError(f"token count {m} must be a multiple of 256")


def _mm(a, b):
    return jnp.dot(a.astype(BF16), b.astype(BF16), preferred_element_type=F32)


def _mm_nt(a, b):
    return lax.dot_general(a.astype(BF16), b.astype(BF16), (((1,), (1,)), ((), ())),
                           preferred_element_type=F32)


def _mm_tn(a, b):
    return lax.dot_general(a.astype(BF16), b.astype(BF16), (((0,), (0,)), ((), ())),
                           preferred_element_type=F32)


def _split2(x):
    hi = x.astype(BF16)
    lo = (x - hi.astype(F32)).astype(BF16)
    return hi, lo


def _split3(x):
    hi = x.astype(BF16)
    r = x - hi.astype(F32)
    mid = r.astype(BF16)
    lo = (r - mid.astype(F32)).astype(BF16)
    return hi, mid, lo


def _mm3(a, b):
    ah, al = _split2(a)
    bh, bl = _split2(b)
    d = functools.partial(jnp.dot, preferred_element_type=F32)
    return d(ah, bh) + (d(ah, bl) + d(al, bh))


def _rms(x, g):
    return x * lax.rsqrt(jnp.mean(x * x, axis=-1, keepdims=True) + EPS) * g


def _sigmoid(x):
    return 0.5 * (1.0 + jnp.tanh(0.5 * x))


def _silu(x):
    return x * _sigmoid(x)


def _softplus(x):
    return jnp.maximum(x, 0.0) + jnp.log1p(jnp.exp(-jnp.abs(x)))


def _gelu(x):
    c = 0.7978845608028654
    return x * (0.5 * (1.0 + jnp.tanh(c * (x + 0.044715 * (x * x * x)))))


def _shift_rows(x, prev8, s):
    r = pltpu.roll(x, s, axis=0)
    pr = pltpu.roll(prev8, s, axis=0)
    row8 = lax.broadcasted_iota(jnp.int32, pr.shape, 0)
    top = jnp.where(row8 < s, pr, r[0:SUBLANES])
    return jnp.concatenate([top, r[SUBLANES:]], axis=0)


def _dwconv(x, prev8, w):
    y = None
    for k in range(CONV_W):
        s = CONV_W - 1 - k
        xs = x if s == 0 else _shift_rows(x, prev8, s)
        term = xs * w[k:k + 1]
        y = term if y is None else y + term
    return y


def _proj_kernel(*refs, has_aux):
    if has_aux:
        x_ref, g_ref, w_ref, waux_ref, o_ref, oaux_ref, hn_ref = refs
    else:
        x_ref, g_ref, w_ref, o_ref, hn_ref = refs

    @pl.when(pl.program_id(1) == 0)
    def _():
        hn = _rms(x_ref[...], g_ref[...]).astype(BF16)
        hn_ref[...] = hn
        if has_aux:
            oaux_ref[...] = jnp.dot(hn, waux_ref[...], preferred_element_type=F32)

    o_ref[...] = jnp.dot(hn_ref[...], w_ref[...], preferred_element_type=F32)


def _proj(h, g, w, tn, waux=None):
    m, k = h.shape
    n = w.shape[1]
    tm = _tile_m(m)
    in_specs = [pl.BlockSpec((tm, k), lambda i, j: (i, 0)),
                pl.BlockSpec((1, k), lambda i, j: (0, 0)),
                pl.BlockSpec((k, tn), lambda i, j: (0, j))]
    out_shape = [jax.ShapeDtypeStruct((m, n), F32)]
    out_specs = [pl.BlockSpec((tm, tn), lambda i, j: (i, j))]
    args = [h, g.reshape(1, k), w]
    if waux is not None:
        na = waux.shape[1]
        in_specs.append(pl.BlockSpec((k, na), lambda i, j: (0, 0)))
        out_shape.append(jax.ShapeDtypeStruct((m, na), F32))
        out_specs.append(pl.BlockSpec((tm, na), lambda i, j: (i, 0)))
        args.append(waux)
    res = pl.pallas_call(
        functools.partial(_proj_kernel, has_aux=waux is not None),
        grid=(m // tm, n // tn),
        in_specs=in_specs, out_specs=out_specs, out_shape=out_shape,
        scratch_shapes=[pltpu.VMEM((tm, k), BF16)],
        compiler_params=_cp("parallel", "arbitrary"),
        name="proj",
    )(*args)
    return res if waux is not None else res[0]


def _outproj_kernel(h_ref, o_ref, w_ref, out_ref):
    out_ref[...] = h_ref[...] + jnp.dot(o_ref[...].astype(BF16), w_ref[...], preferred_element_type=F32)


def _outproj_res(h, o, w):
    m, d = h.shape
    k = o.shape[1]
    tm = _tile_m(m)
    return pl.pallas_call(
        _outproj_kernel,
        grid=(m // tm,),
        in_specs=[pl.BlockSpec((tm, d), lambda i: (i, 0)),
                  pl.BlockSpec((tm, k), lambda i: (i, 0)),
                  pl.BlockSpec((k, d), lambda i: (0, 0))],
        out_specs=pl.BlockSpec((tm, d), lambda i: (i, 0)),
        out_shape=jax.ShapeDtypeStruct((m, d), F32),
        compiler_params=_cp("parallel"),
        name="outproj_res",
    )(h, o, w)


def _ffn_kernel(x_ref, g_ref, wg_ref, wu_ref, wd_ref, o_ref, hn_ref):
    @pl.when(pl.program_id(1) == 0)
    def _():
        x = x_ref[...]
        hn_ref[...] = _rms(x, g_ref[...]).astype(BF16)
        o_ref[...] = x

    hn = hn_ref[...]
    gg = jnp.dot(hn, wg_ref[...], preferred_element_type=F32)
    uu = jnp.dot(hn, wu_ref[...], preferred_element_type=F32)
    hid = (_silu(gg) * uu).astype(BF16)
    o_ref[...] += jnp.dot(hid, wd_ref[...], preferred_element_type=F32)


def _ffn_dense(h, g, wg, wu, wd):
    m, d = h.shape
    f = wg.shape[1]
    tm = 512
    tf = f // 2 if (f // 2) % LANES == 0 else 256
    return pl.pallas_call(
        _ffn_kernel,
        grid=(m // tm, f // tf),
        in_specs=[pl.BlockSpec((tm, d), lambda i, j: (i, 0)),
                  pl.BlockSpec((1, d), lambda i, j: (0, 0)),
                  pl.BlockSpec((d, tf), lambda i, j: (0, j)),
                  pl.BlockSpec((d, tf), lambda i, j: (0, j)),
                  pl.BlockSpec((tf, d), lambda i, j: (j, 0))],
        out_specs=pl.BlockSpec((tm, d), lambda i, j: (i, 0)),
        out_shape=jax.ShapeDtypeStruct((m, d), F32),
        scratch_shapes=[pltpu.VMEM((tm, d), BF16)],
        compiler_params=_cp("parallel", "arbitrary"),
        name="ffn_dense",
    )(h, g.reshape(1, d), wg, wu, wd)


def _moe_kernel(x_ref, g_ref, wr_ref, wg_ref, wu_ref, wd_ref, o_ref, hn_ref, gates_ref, rank_ref, rankt_ref,
                *, n_experts, pass_rows):
    e = pl.program_id(1)
    n_sub, _, sub = rankt_ref.shape

    @pl.when(e == 0)
    def _():
        ri = lax.broadcasted_iota(jnp.int32, (sub, sub), 0)
        ci = lax.broadcasted_iota(jnp.int32, (sub, sub), 1)
        tril = jnp.where(ri >= ci, 1.0, 0.0).astype(BF16)
        lane = lax.broadcasted_iota(jnp.int32, (sub, LANES), 1)
        for s in range(n_sub):
            rs = slice(s * sub, (s + 1) * sub)
            x = x_ref[rs, :]
            hn = _rms(x, g_ref[...])
            hn_ref[rs, :] = hn.astype(BF16)
            o_ref[rs, :] = x
            logits = _mm3(hn, wr_ref[...])
            lg = jnp.where(lane < n_experts, logits, -jnp.inf)
            m1 = jnp.max(lg, axis=-1, keepdims=True)
            i1 = jnp.min(jnp.where(lg == m1, lane, LANES), axis=-1, keepdims=True)
            lg2 = jnp.where(lane == i1, -jnp.inf, lg)
            m2 = jnp.max(lg2, axis=-1, keepdims=True)
            i2 = jnp.min(jnp.where(lg2 == m2, lane, LANES), axis=-1, keepdims=True)
            e2 = jnp.exp(m2 - m1)
            w1 = 1.0 / (1.0 + e2)
            w2 = e2 / (1.0 + e2)
            gates_ref[rs, :] = jnp.where(lane == i1, w1, 0.0) + jnp.where(lane == i2, w2, 0.0)
            sel = jnp.where(lane == i1, 1.0, jnp.where(lane == i2, 1.0, 0.0))
            count = jnp.dot(tril, sel.astype(BF16), preferred_element_type=F32)
            rank = jnp.where(sel > 0.0, count - 1.0, -1.0)
            rank_ref[rs, :] = rank
            rankt_ref[s] = rank.T

    lane = lax.broadcasted_iota(jnp.int32, (sub, LANES), 1)
    for s in range(n_sub):
        rs = slice(s * sub, (s + 1) * sub)
        rank_row = rankt_ref[s, pl.ds(e, 1), :]
        n_routed = (jnp.max(rank_row) + 1.0).astype(jnp.int32)
        rank_col = jnp.sum(jnp.where(lane == e, rank_ref[rs, :], 0.0), axis=-1, keepdims=True)
        gate_col = jnp.sum(jnp.where(lane == e, gates_ref[rs, :], 0.0), axis=-1, keepdims=True)

        def one_pass(cap, base, rs=rs, rank_row=rank_row, rank_col=rank_col, gate_col=gate_col):
            row_id = lax.broadcasted_iota(jnp.int32, (cap, sub), 0).astype(F32) + base
            col_id = lax.broadcasted_iota(jnp.int32, (sub, cap), 1).astype(F32) + base
            pick = jnp.where(rank_row == row_id, 1.0, 0.0).astype(BF16)
            xe = jnp.dot(pick, hn_ref[rs, :], preferred_element_type=F32).astype(BF16)
            gg = jnp.dot(xe, wg_ref[0], preferred_element_type=F32)
            uu = jnp.dot(xe, wu_ref[0], preferred_element_type=F32)
            hid = (_silu(gg) * uu).astype(BF16)
            ye = jnp.dot(hid, wd_ref[0], preferred_element_type=F32).astype(BF16)
            put = jnp.where(rank_col == col_id, gate_col, 0.0).astype(BF16)
            o_ref[rs, :] += jnp.dot(put, ye, preferred_element_type=F32)

        *single, full = pass_rows
        below = 0
        for rows_ in single:
            @pl.when((n_routed > below) & (n_routed <= rows_))
            def _(one_pass=one_pass, rows_=rows_):
                one_pass(rows_, 0.0)
            below = rows_

        @pl.when(n_routed > below)
        def _(one_pass=one_pass, n_routed=n_routed):
            def body(cc, carry):
                one_pass(full, (cc * full).astype(F32))
                return carry
            lax.fori_loop(0, (n_routed + full - 1) // full, body, 0)


def _moe(h, g, wr, wg, wu, wd):
    m, d = h.shape
    ne, _, ef = wg.shape
    sub = 512
    tm = 2 * sub if m % (2 * sub) == 0 else sub
    mean_rows = sub * TOP_K // ne
    pass_rows = tuple(mean_rows * k // 4 for k in (4, 5, 6, 7, 8))
    assert m % tm == 0
    return pl.pallas_call(
        functools.partial(_moe_kernel, n_experts=ne, pass_rows=pass_rows),
        grid=(m // tm, ne),
        in_specs=[pl.BlockSpec((tm, d), lambda i, e: (i, 0)),
                  pl.BlockSpec((1, d), lambda i, e: (0, 0)),
                  pl.BlockSpec((d, LANES), lambda i, e: (0, 0)),
                  pl.BlockSpec((1, d, ef), lambda i, e: (e, 0, 0)),
                  pl.BlockSpec((1, d, ef), lambda i, e: (e, 0, 0)),
                  pl.BlockSpec((1, ef, d), lambda i, e: (e, 0, 0))],
        out_specs=pl.BlockSpec((tm, d), lambda i, e: (i, 0)),
        out_shape=jax.ShapeDtypeStruct((m, d), F32),
        scratch_shapes=[pltpu.VMEM((tm, d), BF16), pltpu.VMEM((tm, LANES), F32),
                        pltpu.VMEM((tm, LANES), F32), pltpu.VMEM((tm // sub, LANES, sub), F32)],
        compiler_params=_cp("parallel", "arbitrary"),
        name="moe",
    )(h, g.reshape(1, d), wr, wg, wu, wd)


def _ple_kernel(*refs, final, n_prompt_tiles):
    if final:
        x_ref, pp_ref, ps_ref, g_ref, wp_ref, wg_ref, gf_ref, yp_ref, ys_ref = refs
    else:
        x_ref, pp_ref, ps_ref, g_ref, wp_ref, wg_ref, o_ref = refs
    i = pl.program_id(0)
    is_sample = i >= n_prompt_tiles
    x = x_ref[...]
    hn = _rms(x, g_ref[...]).astype(BF16)
    gate = _sigmoid(jnp.dot(hn, wg_ref[...], preferred_element_type=F32))
    p = jnp.where(is_sample, ps_ref[...], pp_ref[...])
    pp = jnp.dot(p.astype(BF16), wp_ref[...], preferred_element_type=F32)
    out = x + pp * gate
    if final:
        y = _rms(out, gf_ref[...])

        @pl.when(jnp.logical_not(is_sample))
        def _():
            yp_ref[...] = y

        @pl.when(is_sample)
        def _():
            ys_ref[...] = y
    else:
        o_ref[...] = out


def _ple(h, p_prompt, p_sample, layer, g, wp, wgate, g_final=None):
    m, d = h.shape
    _, m_p, pd = p_prompt.shape
    m_s = p_sample.shape[1]
    tm = _tile_m(m)
    assert m_p % tm == 0 and m_s % tm == 0
    npt, nst = m_p // tm, m_s // tm
    final = g_final is not None
    in_specs = [pl.BlockSpec((tm, d), lambda i: (i, 0)),
                pl.BlockSpec((None, tm, pd), lambda i: (layer, jnp.minimum(i, npt - 1), 0)),
                pl.BlockSpec((None, tm, pd), lambda i: (layer, jnp.maximum(i - npt, 0), 0)),
                pl.BlockSpec((1, d), lambda i: (0, 0)),
                pl.BlockSpec((pd, d), lambda i: (0, 0)),
                pl.BlockSpec((d, d), lambda i: (0, 0))]
    args = [h, p_prompt, p_sample, g.reshape(1, d), wp, wgate]
    if final:
        in_specs.append(pl.BlockSpec((1, d), lambda i: (0, 0)))
        args.append(g_final.reshape(1, d))
        out_specs = [pl.BlockSpec((tm, d), lambda i: (jnp.minimum(i, npt - 1), 0)),
                     pl.BlockSpec((tm, d), lambda i: (jnp.maximum(i - npt, 0), 0))]
        out_shape = [jax.ShapeDtypeStruct((m_p, d), F32), jax.ShapeDtypeStruct((m_s, d), F32)]
    else:
        out_specs = pl.BlockSpec((tm, d), lambda i: (i, 0))
        out_shape = jax.ShapeDtypeStruct((m, d), F32)
    return pl.pallas_call(
        functools.partial(_ple_kernel, final=final, n_prompt_tiles=npt),
        grid=(npt + nst,),
        in_specs=in_specs, out_specs=out_specs, out_shape=out_shape,
        compiler_params=_cp("arbitrary"),
        name="ple",
    )(*args)


def _pair_rms(x, g, hd):
    lane = lax.broadcasted_iota(jnp.int32, x.shape, 1)
    lo = lane < hd
    x2 = x * x
    s0 = jnp.sum(jnp.where(lo, x2, 0.0), axis=-1, keepdims=True)
    s1 = jnp.sum(jnp.where(lo, 0.0, x2), axis=-1, keepdims=True)
    ms = jnp.where(lo, s0, s1) * (1.0 / hd)
    return x * lax.rsqrt(ms + EPS) * g


def _stack_heads(q, hd):
    lane = lax.broadcasted_iota(jnp.int32, q.shape, 1)
    lo = lane < hd
    return jnp.concatenate([jnp.where(lo, q, 0.0), jnp.where(lo, 0.0, q)], axis=0).astype(BF16)


def _unstack_heads(o2, hd):
    n = o2.shape[0] // 2
    lane = lax.broadcasted_iota(jnp.int32, (n, o2.shape[1]), 1)
    return jnp.where(lane < hd, o2[:n], o2[n:])


def _attend(q2, kw, vw, bias, valid):
    s = _mm_nt(q2, kw) + bias
    if valid is not None:
        s = jnp.where(valid, s, -jnp.inf)
    mx = jnp.max(s, axis=-1, keepdims=True)
    e = jnp.exp(s - mx)
    den = jnp.sum(e, axis=-1, keepdims=True)
    return jnp.dot(e.astype(BF16), vw, preferred_element_type=F32) / den


def _attn_prompt_kernel(q_ref, k_ref, v_ref, b_ref, gq_ref, gk_ref, o_ref, kn_ref, kwin_ref, vwin_ref,
                        *, hd, scale, reach, chunk_group):
    i = pl.program_id(2)
    tq, width = q_ref.shape
    band = reach + CHUNK
    pw = 2 * hd
    lanes = [slice(p * pw, (p + 1) * pw) for p in range(width // pw)]
    qn = [_pair_rms(q_ref[:, c], gq_ref[...], hd) * scale for c in lanes]
    knc = [_pair_rms(k_ref[:, c], gk_ref[...], hd) for c in lanes]

    @pl.when(i == pl.num_programs(2) - 1)
    def _():
        for c, x in zip(lanes, knc):
            kn_ref[:, c] = x

    @pl.when(i == 0)
    def _():
        kwin_ref[0:tq, :] = jnp.zeros((tq, width), BF16)
        vwin_ref[0:tq, :] = jnp.zeros((tq, width), BF16)

    @pl.when(i > 0)
    def _():
        kwin_ref[0:tq, :] = kwin_ref[tq:2 * tq, :]
        vwin_ref[0:tq, :] = vwin_ref[tq:2 * tq, :]

    for c, x in zip(lanes, knc):
        kwin_ref[tq:2 * tq, c] = x.astype(BF16)
    vwin_ref[tq:2 * tq, :] = v_ref[...].astype(BF16)
    col = lax.broadcasted_iota(jnp.int32, (1, band), 1)
    n_chunks = tq // CHUNK
    w0s = [tq - reach + jj * CHUNK for jj in range(n_chunks)]

    def attend(first_block, items):
        q2 = [_stack_heads(qn[p][jj * CHUNK:(jj + 1) * CHUNK], hd) for p, jj in items]
        s = [_mm_nt(x, kwin_ref[w0s[jj]:w0s[jj] + band, lanes[p]]) + b_ref[p] for x, (p, jj) in zip(q2, items)]
        if first_block:
            s = [jnp.where(col + w0s[jj] >= tq, x, -jnp.inf) for x, (p, jj) in zip(s, items)]
        mx = [jnp.max(x, axis=-1, keepdims=True) for x in s]
        e = [jnp.exp(x - m) for x, m in zip(s, mx)]
        den = [jnp.sum(x, axis=-1, keepdims=True) for x in e]
        o2 = [jnp.dot(x.astype(BF16), vwin_ref[w0s[jj]:w0s[jj] + band, lanes[p]], preferred_element_type=F32)
              for x, (p, jj) in zip(e, items)]
        for x, dn, (p, jj) in zip(o2, den, items):
            o_ref[jj * CHUNK:(jj + 1) * CHUNK, lanes[p]] = _unstack_heads(x / dn, hd)

    def attend_all(first_block):
        items = [(p, jj) for jj in range(n_chunks) for p in range(len(lanes))]
        for j0 in range(0, len(items), chunk_group):
            attend(first_block, items[j0:j0 + chunk_group])

    @pl.when(i == 0)
    def _():
        attend_all(True)

    @pl.when(i > 0)
    def _():
        attend_all(False)


def _attn_sample_kernel(q_ref, k_ref, v_ref, ck_ref, cv_ref, b_ref, gq_ref, gk_ref, oin_ref, o_ref, kn_ref,
                        *, hd, scale):
    del oin_ref
    pw = 2 * hd
    pairs = [slice(p * pw, (p + 1) * pw) for p in range(q_ref.shape[1] // pw)]
    qn = [_pair_rms(q_ref[:, c], gq_ref[...], hd) * scale for c in pairs]
    kn = [_pair_rms(k_ref[:, c], gk_ref[...], hd) for c in pairs]
    for c, x in zip(pairs, kn):
        kn_ref[:, c] = x
    kcat = [jnp.concatenate([ck_ref[:, c], x], axis=0).astype(BF16) for c, x in zip(pairs, kn)]
    vcat = [jnp.concatenate([cv_ref[:, c], v_ref[:, c]], axis=0).astype(BF16) for c in pairs]
    s = [_mm_nt(_stack_heads(x, hd), kk) + b_ref[p] for p, (x, kk) in enumerate(zip(qn, kcat))]
    mx = [jnp.max(x, axis=-1, keepdims=True) for x in s]
    e = [jnp.exp(x - m) for x, m in zip(s, mx)]
    den = [jnp.sum(x, axis=-1, keepdims=True) for x in e]
    o2 = [jnp.dot(x.astype(BF16), vv, preferred_element_type=F32) for x, vv in zip(e, vcat)]
    for c, x, dn in zip(pairs, o2, den):
        o_ref[:, c] = _unstack_heads(x / dn, hd)


def _rel_bias_pairs(table, q0, nq, nk):
    nh = table.shape[0]
    span = nq + nk - 1
    dmin, dmax = q0 - (nk - 1), q0 + nq - 1
    lo, hi = max(dmin, -A_REL_CLIP), min(dmax, A_REL_CLIP)
    line = jnp.concatenate([jnp.broadcast_to(table[:, :1], (nh, lo - dmin)),
                            table[:, lo + A_REL_CLIP:hi + A_REL_CLIP + 1],
                            jnp.broadcast_to(table[:, -1:], (nh, dmax - hi))], axis=1)
    flipped = line[:, ::-1]
    period = jnp.concatenate([flipped, jnp.zeros((nh, 1), table.dtype)], axis=1)
    shifted = jnp.tile(period, (1, nq))[:, :nq * span].reshape(nh, nq, span)
    b = shifted[:, :, nq - 1:nq - 1 + nk]
    return b.reshape(nh // 2, 2 * nq, nk)


def _attention(qkv, n_b, t, n_s, t_s, cache_k, cache_v, rel, g_q, g_k, n_heads):
    m_tot, d3 = qkv.shape
    d = d3 // 3
    hd = d // n_heads
    pw = 2 * hd
    n_pairs = d // pw
    m_p = n_b * t
    reach = A_PREV_CHUNKS * CHUNK
    tq = reach
    assert pw == LANES and t % tq == 0 and t_s % 16 == 0
    nqb = t // tq
    scale = hd ** -0.5
    gq2 = jnp.tile(g_q, 2).reshape(1, pw)
    gk2 = jnp.tile(g_k, 2).reshape(1, pw)

    bias_p = _rel_bias_pairs(rel, reach, CHUNK, reach + CHUNK)
    kcol, vcol = n_pairs, 2 * n_pairs

    def rows(b, p, i):
        return b * nqb + i

    ppb = 2 if n_pairs % 2 == 0 else 1
    npb = n_pairs // ppb
    blk = (tq, ppb * pw)
    o_all, kn_p = pl.pallas_call(
        functools.partial(_attn_prompt_kernel, hd=hd, scale=scale, reach=reach, chunk_group=4),
        grid=(n_b, npb, nqb),
        in_specs=[pl.BlockSpec(blk, lambda b, p, i: (rows(b, p, i), p)),
                  pl.BlockSpec(blk, lambda b, p, i: (rows(b, p, i), npb + p)),
                  pl.BlockSpec(blk, lambda b, p, i: (rows(b, p, i), 2 * npb + p)),
                  pl.BlockSpec((ppb, 2 * CHUNK, reach + CHUNK), lambda b, p, i: (p, 0, 0)),
                  pl.BlockSpec((1, pw), lambda b, p, i: (0, 0)),
                  pl.BlockSpec((1, pw), lambda b, p, i: (0, 0))],
        out_specs=[pl.BlockSpec(blk, lambda b, p, i: (rows(b, p, i), p)),
                   pl.BlockSpec(blk, lambda b, p, i: (b, p))],
        out_shape=[jax.ShapeDtypeStruct((m_tot, d), F32),
                   jax.ShapeDtypeStruct((n_b * tq, d), F32)],
        scratch_shapes=[pltpu.VMEM((2 * tq, ppb * pw), BF16), pltpu.VMEM((2 * tq, ppb * pw), BF16)],
        compiler_params=_cp("parallel", "parallel", "arbitrary"),
        name="attn_prompt",
    )(qkv, qkv, qkv, bias_p, gq2, gk2)

    w = cache_k.shape[1]
    bias_s = _rel_bias_pairs(rel, w, t_s, w + t_s)
    srow = m_p // t_s
    sblk = (t_s, d)
    o_all, kn_s = pl.pallas_call(
        functools.partial(_attn_sample_kernel, hd=hd, scale=scale),
        grid=(n_s,),
        in_specs=[pl.BlockSpec(sblk, lambda b: (srow + b, 0)),
                  pl.BlockSpec(sblk, lambda b: (srow + b, 1)),
                  pl.BlockSpec(sblk, lambda b: (srow + b, 2)),
                  pl.BlockSpec((None, w, d), lambda b: (b, 0, 0)),
                  pl.BlockSpec((None, w, d), lambda b: (b, 0, 0)),
                  pl.BlockSpec((n_pairs, 2 * t_s, w + t_s), lambda b: (0, 0, 0)),
                  pl.BlockSpec((1, pw), lambda b: (0, 0)),
                  pl.BlockSpec((1, pw), lambda b: (0, 0)),
                  pl.BlockSpec(memory_space=pl.ANY)],
        out_specs=[pl.BlockSpec(sblk, lambda b: (srow + b, 0)),
                   pl.BlockSpec(sblk, lambda b: (b, 0))],
        out_shape=[jax.ShapeDtypeStruct((m_tot, d), F32),
                   jax.ShapeDtypeStruct((n_s * t_s, d), F32)],
        input_output_aliases={8: 0},
        compiler_params=_cp("parallel"),
        name="attn_sample",
    )(qkv, qkv, qkv, cache_k, cache_v, bias_s, gq2, gk2, o_all)
    return o_all, kn_p, kn_s


def _gdn_act_kernel(xq_ref, xk_ref, xv_ref, pq_ref, pk_ref, pv_ref, hq_ref, hk_ref, hv_ref,
                    aux_ref, cw_ref, pvec_ref, o_ref, bg_ref, *, blocks_per_seq, n_heads, scale):
    i = pl.program_id(0)
    start = (i % blocks_per_seq) == 0
    d = xq_ref.shape[1]
    hd = d // n_heads
    parts = ((xq_ref, pq_ref, hq_ref), (xk_ref, pk_ref, hk_ref), (xv_ref, pv_ref, hv_ref))
    for c, (x_ref, p_ref, h_ref) in enumerate(parts):
        prev8 = jnp.where(start, h_ref[...], p_ref[...])
        a = _silu(_dwconv(x_ref[...], prev8, cw_ref[:, c * d:(c + 1) * d]))
        for hh in range(n_heads):
            ah = a[:, hh * hd:(hh + 1) * hd]
            if c < 2:
                ah = ah * lax.rsqrt(jnp.sum(ah * ah, axis=-1, keepdims=True) + EPS)
            if c == 0:
                ah = ah * scale
            o_ref[:, c * d + hh * hd:c * d + (hh + 1) * hd] = ah
    ax = aux_ref[...]
    lane = lax.broadcasted_iota(jnp.int32, ax.shape, 1)
    beta = _sigmoid(ax)
    gdec = -jnp.exp(pvec_ref[0:1, :]) * _softplus(ax + pvec_ref[1:2, :])
    bg_ref[...] = jnp.where(lane < n_heads, beta, jnp.where(lane < 2 * n_heads, gdec, 0.0))


def _gdn_act(proj, aux, row0, n_seq, t, tb, hist8, conv_w, pvec, n_heads):
    d = proj.shape[1] // 4
    hd = d // n_heads
    bps = t // tb
    r0 = row0 // tb
    p8 = tb // SUBLANES

    def xspec(c):
        return pl.BlockSpec((tb, d), lambda i, c=c: (r0 + i, c))

    def pspec(c):
        return pl.BlockSpec((SUBLANES, d), lambda i, c=c: (jnp.maximum((r0 + i) * p8 - 1, 0), c))

    def hspec(c):
        return pl.BlockSpec((None, SUBLANES, d), lambda i, c=c: (i // bps, 0, c))

    m = n_seq * t
    return pl.pallas_call(
        functools.partial(_gdn_act_kernel, blocks_per_seq=bps, n_heads=n_heads, scale=hd ** -0.5),
        grid=(m // tb,),
        in_specs=[xspec(0), xspec(1), xspec(2), pspec(0), pspec(1), pspec(2), hspec(0), hspec(1), hspec(2),
                  pl.BlockSpec((tb, LANES), lambda i: (r0 + i, 0)),
                  pl.BlockSpec((CONV_W, 3 * d), lambda i: (0, 0)),
                  pl.BlockSpec((2, LANES), lambda i: (0, 0))],
        out_specs=[pl.BlockSpec((tb, 3 * d), lambda i: (i, 0)),
                   pl.BlockSpec((tb, LANES), lambda i: (i, 0))],
        out_shape=[jax.ShapeDtypeStruct((m, 3 * d), F32), jax.ShapeDtypeStruct((m, LANES), F32)],
        compiler_params=_cp("parallel"),
        name="gdn_act",
    )(proj, proj, proj, proj, proj, proj, hist8, hist8, hist8, aux, conv_w, pvec)


def _seg_cumsum(x, rowmod, reverse):
    n = x.shape[0]
    d = 1
    while d < CHUNK:
        if reverse:
            sh = pltpu.roll(x, n - d, axis=0)
            x = x + jnp.where(rowmod < CHUNK - d, sh, 0.0)
        else:
            sh = pltpu.roll(x, d, axis=0)
            x = x + jnp.where(rowmod >= d, sh, 0.0)
        d *= 2
    return x


def _gdn_delta_kernel(act_ref, z_ref, bg_ref, s0_ref, go_ref, o_ref, sout_ref, s_ref, *, n_heads, chained, head_group):
    i = pl.program_id(1)
    tb, d = z_ref.shape
    hd = d // n_heads
    nch = tb // CHUNK
    dot = functools.partial(jnp.dot, preferred_element_type=F32)

    if chained:
        @pl.when(i == 0)
        def _():
            s_ref[...] = s0_ref[...]

    bg = bg_ref[...]
    lane = lax.broadcasted_iota(jnp.int32, (tb, hd), 1)
    rowmod = lax.broadcasted_iota(jnp.int32, (tb, hd), 0) & (CHUNK - 1)
    ri = lax.broadcasted_iota(jnp.int32, (tb, tb), 0)
    ci = lax.broadcasted_iota(jnp.int32, (tb, tb), 1)
    same = (ri >> 6) == (ci >> 6)
    low_incl = same & (ri >= ci)
    strict = same & (ri > ci)
    eye_c = (lax.broadcasted_iota(jnp.int32, (CHUNK, tb), 0)
             == (lax.broadcasted_iota(jnp.int32, (CHUNK, tb), 1) & (CHUNK - 1))).astype(F32)
    def compact(x):
        acc = x[0:CHUNK]
        for c in range(1, nch):
            acc = acc + x[c * CHUNK:(c + 1) * CHUNK]
        return acc

    def expand(xc):
        return jnp.where(same, jnp.concatenate([xc] * nch, axis=0), jnp.zeros((), xc.dtype))

    def each(fn, *lists):
        return [fn(*a) for a in zip(*lists)]

    cum_fwd = _seg_cumsum(bg, rowmod, reverse=False)
    cum_rev = _seg_cumsum(bg, rowmod, reverse=True) - bg

    def column(arr, idx):
        return jnp.broadcast_to(jnp.sum(jnp.where(lane == idx, arr, 0.0), axis=-1, keepdims=True), (tb, hd))

    def gc_diff(gc):
        gh, gm, gl = (piece.astype(F32) for piece in _split3(gc))
        lhs = jnp.where(lane == 0, gh, jnp.where(lane == 1, gm, jnp.where(lane == 2, gl,
                                                                          jnp.where(lane < 6, 1.0, 0.0))))
        rhs_t = jnp.where(lane < 3, 1.0, jnp.where(lane == 3, -gh, jnp.where(lane == 4, -gm,
                                                                             jnp.where(lane == 5, -gl, 0.0))))
        return _mm_nt(lhs, rhs_t)

    for h0 in range(0, n_heads, head_group):
        hs = list(range(h0, h0 + head_group))
        q = [act_ref[:, h * hd:(h + 1) * hd] for h in hs]
        k = [act_ref[:, d + h * hd:d + (h + 1) * hd] for h in hs]
        v = [act_ref[:, 2 * d + h * hd:2 * d + (h + 1) * hd] for h in hs]
        beta = [column(bg, h) for h in hs]
        gc = [column(cum_fwd, n_heads + h) for h in hs]
        grev = [column(cum_rev, n_heads + h) for h in hs]
        decay = each(lambda x: jnp.exp(jnp.where(low_incl, gc_diff(x), -jnp.inf)), gc)
        kb = each(lambda a, b: a * b, k, beta)
        lmat = each(lambda a, b, dc: jnp.where(strict, _mm_nt(a, b) * dc, 0.0), kb, k, decay)
        egc = each(jnp.exp, gc)
        rhs = each(lambda a, b, c_, e: jnp.concatenate([a * b, c_ * e], axis=1), v, beta, kb, egc)

        xc = each(compact, lmat)
        tinv = each(lambda x: eye_c - x, xc)
        xh = each(lambda x: x.astype(BF16), xc)
        xe = each(expand, xh)
        p = 1
        while 2 * p < CHUNK:
            xh = each(lambda a, b: dot(a, b).astype(BF16), xh, xe)
            xe = each(expand, xh)
            tinv = each(lambda t, b: t + dot(t.astype(BF16), b), tinv, xe)
            p *= 2
        sol = each(lambda t, r: dot(expand(t.astype(BF16)), r.astype(BF16)), tinv, rhs)

        qk = each(lambda a, b, dc: jnp.where(low_incl, _mm_nt(a, b) * dc, 0.0), q, k, decay)
        q_dec = each(lambda a, e: a * e, q, egc)
        k_dec = each(lambda a, r: a * jnp.exp(r), k, grev)
        g_last = each(lambda a, r: jnp.exp(a + r), gc, grev)

        v_news = [[] for _ in hs]
        o_inter = [[] for _ in hs]
        s = [s_ref[h] for h in hs] if chained else None
        for c in range(nch):
            rc = slice(c * CHUNK, (c + 1) * CHUNK)
            if not chained:
                s = [s0_ref[c, h] for h in hs]
            v_new = each(lambda so, st: so[rc, :hd] - _mm(so[rc, hd:], st), sol, s)
            o_c = each(lambda qd, st: _mm(qd[rc], st), q_dec, s)
            s = each(lambda st, gl, kd, vn: st * gl[c * CHUNK:c * CHUNK + 1, :] + _mm_tn(kd[rc], vn),
                     s, g_last, k_dec, v_new)
            for j, h in enumerate(hs):
                v_news[j].append(v_new[j])
                o_inter[j].append(o_c[j])
                if not chained:
                    sout_ref[c, h] = s[j]
        if chained:
            for j, h in enumerate(hs):
                s_ref[h] = s[j]

            @pl.when(i == pl.num_programs(1) - 1)
            def _(s=s, hs=hs):
                for j, h in enumerate(hs):
                    sout_ref[h] = s[j]

        o = each(lambda oi, m, vn: jnp.concatenate(oi, axis=0) + _mm(m, jnp.concatenate(vn, axis=0)),
                 o_inter, qk, v_news)
        for j, h in enumerate(hs):
            o_ref[:, h * hd:(h + 1) * hd] = _rms(o[j], go_ref[...]) * _silu(z_ref[:, h * hd:(h + 1) * hd])


def _gdn_delta(act, bg, zsrc, zcol, n_seq, t, s0, g_o, n_heads, out_rows=None):
    d = act.shape[1] // 3
    hd = d // n_heads
    tb = 4 * CHUNK
    chained = t > CHUNK
    if chained:
        assert t % tb == 0
        grid = (n_seq, t // tb)
        rows = lambda b, i: b * (t // tb) + i
        sspec = pl.BlockSpec((None, n_heads, hd, hd), lambda b, i: (b, 0, 0, 0))
    else:
        assert t == CHUNK and n_seq % (tb // CHUNK) == 0
        grid = (n_seq * t // tb, 1)
        rows = lambda b, i: b
        sspec = pl.BlockSpec((tb // CHUNK, n_heads, hd, hd), lambda b, i: (b, 0, 0, 0))
    return pl.pallas_call(
        functools.partial(_gdn_delta_kernel, n_heads=n_heads, chained=chained, head_group=8),
        grid=grid,
        in_specs=[pl.BlockSpec((tb, 3 * d), lambda b, i: (rows(b, i), 0)),
                  pl.BlockSpec((tb, d), lambda b, i: (rows(b, i), zcol)),
                  pl.BlockSpec((tb, LANES), lambda b, i: (rows(b, i), 0)),
                  sspec,
                  pl.BlockSpec((1, hd), lambda b, i: (0, 0))],
        out_specs=[pl.BlockSpec((tb, d), lambda b, i: (rows(b, i), 0)), sspec],
        out_shape=[jax.ShapeDtypeStruct((out_rows or n_seq * t, d), F32),
                   jax.ShapeDtypeStruct((n_seq, n_heads, hd, hd), F32)],
        scratch_shapes=[pltpu.VMEM((n_heads, hd, hd), F32)],
        compiler_params=_cp("parallel", "arbitrary"),
        name="gdn_delta",
    )(act, zsrc, bg, s0, g_o.reshape(1, hd))


def _rglru_kernel(h_ref, g_ref, win_ref, cw_ref, cb_ref, wgate_ref, bra_ref, bri_ref, lam_ref, wout_ref,
                  hist_ref, h0_ref, oin_ref, o_ref, hlast_ref, xtail_ref, cx_ref, ch_ref, *, n_blocks):
    del oin_ref
    i = pl.program_id(1)
    tb, d = h_ref.shape
    cw = lam_ref.shape[1]
    blk = cw // n_blocks

    @pl.when(i == 0)
    def _():
        cx_ref[...] = hist_ref[...]
        ch_ref[...] = h0_ref[...]

    h = h_ref[...]
    proj = jnp.dot(_rms(h, g_ref[...]).astype(BF16), win_ref[...], preferred_element_type=F32)
    y_br = proj[:, :cw]
    x_br = proj[:, cw:]
    xc = _dwconv(x_br, cx_ref[...], cw_ref[...]) + cb_ref[...]
    r_parts, i_parts = [], []
    for n in range(n_blocks):
        ri = jnp.dot(xc[:, n * blk:(n + 1) * blk].astype(BF16), wgate_ref[n], preferred_element_type=F32)
        r_parts.append(ri[:, :blk])
        i_parts.append(ri[:, blk:])
    r = _sigmoid(jnp.concatenate(r_parts, axis=1) + bra_ref[...])
    ig = _sigmoid(jnp.concatenate(i_parts, axis=1) + bri_ref[...])
    log_a = -LRU_C * r * _softplus(-lam_ref[...])
    a = jnp.exp(log_a)
    th = jnp.tanh(log_a)
    b = jnp.sqrt(-2.0 * th / (1.0 - th)) * (ig * xc)
    row8 = lax.broadcasted_iota(jnp.int32, (tb, cw), 0) & (SUBLANES - 1)
    step = 1
    while step < SUBLANES:
        a_s = jnp.where(row8 >= step, pltpu.roll(a, step, axis=0), 1.0)
        b_s = jnp.where(row8 >= step, pltpu.roll(b, step, axis=0), 0.0)
        b = a * b_s + b
        a = a * a_s
        step *= 2
    h_in = ch_ref[SUBLANES - 1:SUBLANES, :]
    groups = []
    for j in range(tb // SUBLANES):
        rows = slice(j * SUBLANES, (j + 1) * SUBLANES)
        hg = b[rows] + a[rows] * h_in
        groups.append(hg)
        h_in = hg[SUBLANES - 1:SUBLANES, :]
    hs = jnp.concatenate(groups, axis=0)
    cx_ref[...] = x_br[tb - SUBLANES:, :]
    ch_ref[...] = hs[tb - SUBLANES:, :]

    @pl.when(i == pl.num_programs(1) - 1)
    def _():
        hlast_ref[...] = hs[tb - SUBLANES:, :]
        xtail_ref[...] = x_br[tb - SUBLANES:, :]

    o_ref[...] = h + jnp.dot((hs * _gelu(y_br)).astype(BF16), wout_ref[...], preferred_element_type=F32)


def _rglru(h, h_out, row0, n_seq, t, tb, g, w_in, conv_w, conv_b, wgate, b_ra, b_ri, lam, w_out, hist8, h08):
    m, d = h.shape
    cw = lam.shape[0]
    nb = wgate.shape[0]
    nt = t // tb
    r0 = row0 // tb
    vec = lambda x: x.reshape(1, -1)
    cst = lambda shape: pl.BlockSpec(shape, lambda b, i: (0,) * len(shape))
    in_specs = [pl.BlockSpec((tb, d), lambda b, i: (r0 + b * nt + i, 0)),
                cst((1, d)), cst((d, 2 * cw)), cst((CONV_W, cw)), cst((1, cw)),
                cst(wgate.shape), cst((1, cw)), cst((1, cw)), cst((1, cw)), cst((cw, d)),
                pl.BlockSpec((None, SUBLANES, cw), lambda b, i: (b, 0, 0)),
                pl.BlockSpec((None, SUBLANES, cw), lambda b, i: (b, 0, 0))]
    args = [h, vec(g), w_in, conv_w, vec(conv_b), wgate, vec(b_ra), vec(b_ri), vec(lam), w_out, hist8, h08]
    aliases = {}
    if h_out is not None:
        in_specs.append(pl.BlockSpec(memory_space=pl.ANY))
        args.append(h_out)
        aliases = {len(args) - 1: 0}
    kern = functools.partial(_rglru_kernel, n_blocks=nb)
    if h_out is None:
        kern = lambda *refs: _rglru_kernel(*refs[:12], None, *refs[12:], n_blocks=nb)
    return pl.pallas_call(
        kern,
        grid=(n_seq, nt),
        in_specs=in_specs,
        out_specs=[pl.BlockSpec((tb, d), lambda b, i: (r0 + b * nt + i, 0)),
                   pl.BlockSpec((None, SUBLANES, cw), lambda b, i: (b, 0, 0)),
                   pl.BlockSpec((None, SUBLANES, cw), lambda b, i: (b, 0, 0))],
        out_shape=[jax.ShapeDtypeStruct((m, d), F32),
                   jax.ShapeDtypeStruct((n_seq, SUBLANES, cw), F32),
                   jax.ShapeDtypeStruct((n_seq, SUBLANES, cw), F32)],
        scratch_shapes=[pltpu.VMEM((SUBLANES, cw), F32), pltpu.VMEM((SUBLANES, cw), F32)],
        input_output_aliases=aliases,
        compiler_params=_cp("parallel", "arbitrary"),
        name="rglru",
    )(*args)


def _gmlp_kernel(h_ref, g_ref, win_ref, bin_ref, lng_ref, lnb_ref, ws_ref, bs_ref, wout_ref, o_ref, vn_ref,
                 *, n_groups, first_sample_tile, lc_prompt, lc_sample):
    i = pl.program_id(0)
    tm, d = h_ref.shape
    half = lng_ref.shape[1]
    gw = half // n_groups
    h = h_ref[...]
    act = _gelu(jnp.dot(_rms(h, g_ref[...]).astype(BF16), win_ref[...], preferred_element_type=F32) + bin_ref[...])
    u = act[:, :half]
    v = act[:, half:]
    vc = v - jnp.mean(v, axis=-1, keepdims=True)
    vn = vc * lax.rsqrt(jnp.mean(vc * vc, axis=-1, keepdims=True) + EPS) * lng_ref[...] + lnb_ref[...]
    vn_ref[...] = vn
    shift = jnp.where(i >= first_sample_tile, lc_sample.bit_length() - 1, lc_prompt.bit_length() - 1)
    ri = lax.broadcasted_iota(jnp.int32, (tm, tm), 0)
    ci = lax.broadcasted_iota(jnp.int32, (tm, tm), 1)
    keep = (lax.shift_right_logical(ri, shift) == lax.shift_right_logical(ci, shift)) & (ri >= ci)
    vnb = vn.astype(BF16)
    parts = []
    for gi in range(n_groups):
        ws = jnp.where(keep, ws_ref[0, gi], 0.0).astype(BF16)
        sg = jnp.dot(ws, vnb[:, gi * gw:(gi + 1) * gw], preferred_element_type=F32)
        parts.append(u[:, gi * gw:(gi + 1) * gw] * (sg + bs_ref[0, :, gi * gw:(gi + 1) * gw]))
    gated = jnp.concatenate(parts, axis=1).astype(BF16)
    o_ref[...] = h + jnp.dot(gated, wout_ref[...], preferred_element_type=F32)


def _gmlp(h, m_p, t_s, g, w_in, b_in, ln_g, ln_b, w_s, b_s, w_out):
    m, d = h.shape
    half = ln_g.shape[0]
    ng = w_s.shape[0]
    gw = half // ng
    tm = 256
    lc_p, lc_s = D_CHUNK, min(t_s, D_CHUNK)
    assert m_p % tm == 0 and (m - m_p) % tm == 0 and tm % lc_p == 0 and tm % lc_s == 0

    def tiled(lc):
        wt = jnp.tile(w_s[:, :lc, :lc], (1, tm // lc, tm // lc))
        bt = jnp.repeat(jnp.tile(b_s[:, :lc].T, (tm // lc, 1)), gw, axis=1)
        return wt, bt

    wt_p, bt_p = tiled(lc_p)
    wt_s, bt_s = tiled(lc_s)
    ws_all = jnp.stack([wt_p, wt_s])
    bs_all = jnp.stack([bt_p, bt_s])
    fst = m_p // tm
    vec = lambda x: x.reshape(1, -1)
    cst = lambda shape: pl.BlockSpec(shape, lambda i: (0,) * len(shape))
    return pl.pallas_call(
        functools.partial(_gmlp_kernel, n_groups=ng, first_sample_tile=fst, lc_prompt=lc_p, lc_sample=lc_s),
        grid=(m // tm,),
        in_specs=[pl.BlockSpec((tm, d), lambda i: (i, 0)),
                  cst((1, d)), cst((d, 2 * half)), cst((1, 2 * half)), cst((1, half)), cst((1, half)),
                  pl.BlockSpec((1, ng, tm, tm), lambda i: (i // fst, 0, 0, 0)),
                  pl.BlockSpec((1, tm, half), lambda i: (i // fst, 0, 0)),
                  cst((half, d))],
        out_specs=[pl.BlockSpec((tm, d), lambda i: (i, 0)),
                   pl.BlockSpec((tm, half), lambda i: (jnp.maximum(i - fst, 0), 0))],
        out_shape=[jax.ShapeDtypeStruct((m, d), F32),
                   jax.ShapeDtypeStruct((m - m_p, half), F32)],
        compiler_params=_cp("arbitrary"),
        name="gmlp",
    )(h, vec(g), w_in, vec(b_in), vec(ln_g), vec(ln_b), ws_all, bs_all, w_out)


def _pad_rows8(x3):
    n, r, c = x3.shape
    return jnp.concatenate([jnp.zeros((n, SUBLANES - r, c), x3.dtype), x3], axis=1)


def kernel(x_prompt, x_sample, p_prompt, p_sample, cache_k_a, cache_v_a, state_s_b, state_conv_b, state_h_c, state_conv_c, g_mix, g_ffn, g_ple, g_final, w_in_a, g_q_a, g_k_a, rel_a, w_out_a, w_in_b, conv_b, a_log_b, dt_bias_b, g_o_b, w_out_b, w_in_c, conv_c, conv_bias_c, w_ra_c, b_ra_c, w_ri_c, b_ri_c, lam_c, w_out_c, w_in_d, b_in_d, ln_g_d, ln_b_d, w_s_d, b_s_d, w_out_d, w_gate_ff, w_up_ff, w_down_ff, w_router, w_gate_e, w_up_e, w_down_e, w_ple, w_ple_gate):
    n_b, t, d = x_prompt.shape
    n_s, t_s, _ = x_sample.shape
    depth = p_prompt.shape[0]
    pd = p_prompt.shape[-1]
    m_p, m_s = n_b * t, n_s * t_s
    m = m_p + m_s
    a_heads = d // g_q_a.shape[-1]
    b_heads = a_log_b.shape[-1]
    n_exp = w_router.shape[-1]
    bf = lambda x: x.astype(BF16)

    h = jnp.concatenate([x_prompt.reshape(m_p, d), x_sample.reshape(m_s, d)], axis=0)
    pp_all = p_prompt.reshape(depth, m_p, pd)
    ps_all = p_sample.reshape(depth, m_s, pd)

    outs = {name: [] for name in ("k_a_p", "v_a_p", "k_a_s", "v_a_s", "s_b_p", "conv_b_p", "s_b_s", "conv_b_s",
                                  "h_c_p", "conv_c_p", "h_c_s", "conv_c_s", "v_d_s")}

    def tail_rows(x2, n_seq, tt, row0, cols):
        keep = CONV_W - 1
        if n_seq <= 4:
            return jnp.stack([x2[row0 + (b + 1) * tt - keep:row0 + (b + 1) * tt, :cols] for b in range(n_seq)])
        return x2[row0:row0 + n_seq * tt, :cols].reshape(n_seq, tt, cols)[:, tt - keep:]

    y = None
    for li in range(depth):
        kind, c = li % 4, li // 4
        if kind == 0:
            qkv = _proj(h, g_mix[li], bf(w_in_a[c]), 1024)
            w_keep = cache_k_a.shape[2]
            o, kn_p, kn_s = _attention(qkv, n_b, t, n_s, t_s,
                                       cache_k_a[c].reshape(n_s, w_keep, d), cache_v_a[c].reshape(n_s, w_keep, d),
                                       rel_a[c], g_q_a[c], g_k_a[c], a_heads)
            hd = d // a_heads
            keep = min(A_PREV_CHUNKS * CHUNK, t)
            v_p = jnp.stack([qkv[(b + 1) * t - keep:(b + 1) * t, 2 * d:] for b in range(n_b)])
            outs["k_a_p"].append(kn_p.reshape(n_b, keep, a_heads, hd))
            outs["v_a_p"].append(v_p.reshape(n_b, keep, a_heads, hd))
            outs["k_a_s"].append(kn_s.reshape(n_s, t_s, a_heads, hd))
            outs["v_a_s"].append(qkv[m_p:, 2 * d:].reshape(n_s, t_s, a_heads, hd))
            h = _outproj_res(h, o, bf(w_out_a[c]))
        elif kind == 1:
            w_in = w_in_b[c]
            nq = 3 * d
            w_main = bf(w_in[:, :nq + d])
            w_aux = bf(jnp.pad(w_in[:, nq + d:], ((0, 0), (0, LANES - 2 * b_heads))))
            proj, aux = _proj(h, g_mix[li], w_main, 1024, w_aux)
            pvec = jnp.zeros((2, LANES), F32)
            pvec = pvec.at[0, b_heads:2 * b_heads].set(a_log_b[c]).at[1, b_heads:2 * b_heads].set(dt_bias_b[c])
            hd = d // b_heads
            act_p, bg_p = _gdn_act(proj, aux, 0, n_b, t, 256, jnp.zeros((n_b, SUBLANES, nq), F32),
                                   conv_b[c], pvec, b_heads)
            o_p, s_p = _gdn_delta(act_p, bg_p, proj, 3, n_b, t,
                                  jnp.zeros((n_b, b_heads, hd, hd), F32), g_o_b[c], b_heads, out_rows=m)
            act_s, bg_s = _gdn_act(proj, aux, m_p, n_s, t_s, t_s, _pad_rows8(state_conv_b[c]),
                                   conv_b[c], pvec, b_heads)
            tp = -(-t_s // CHUNK) * CHUNK
            padt = lambda x2: jnp.pad(x2.reshape(n_s, t_s, -1), ((0, 0), (0, tp - t_s), (0, 0))).reshape(n_s * tp, -1)
            z_s = padt(proj[m_p:, nq:nq + d])
            o_s, s_s = _gdn_delta(padt(act_s), padt(bg_s), z_s, 0, n_s, tp, state_s_b[c], g_o_b[c], b_heads)
            o_s = o_s.reshape(n_s, tp, d)[:, :t_s].reshape(m_s, d)
            outs["s_b_p"].append(s_p)
            outs["s_b_s"].append(s_s)
            outs["conv_b_p"].append(tail_rows(proj, n_b, t, 0, nq))
            outs["conv_b_s"].append(tail_rows(proj, n_s, t_s, m_p, nq))
            h = _outproj_res(h, lax.dynamic_update_slice(o_p, o_s, (m_p, 0)), bf(w_out_b[c]))
        elif kind == 2:
            cw = lam_c.shape[-1]
            wgate = bf(jnp.concatenate([w_ra_c[c], w_ri_c[c]], axis=-1))
            common = (g_mix[li], bf(w_in_c[c]), conv_c[c], conv_bias_c[c], wgate, b_ra_c[c], b_ri_c[c],
                      lam_c[c], bf(w_out_c[c]))
            zeros8 = jnp.zeros((n_b, SUBLANES, cw), F32)
            h_new, hl_p, xt_p = _rglru(h, None, 0, n_b, t, 256, *common, zeros8, zeros8)
            h_new, hl_s, xt_s = _rglru(h, h_new, m_p, n_s, t_s, t_s, *common,
                                       _pad_rows8(state_conv_c[c]), _pad_rows8(state_h_c[c][:, None, :]))
            h = h_new
            outs["h_c_p"].append(hl_p[:, SUBLANES - 1])
            outs["h_c_s"].append(hl_s[:, SUBLANES - 1])
            outs["conv_c_p"].append(xt_p[:, SUBLANES - (CONV_W - 1):])
            outs["conv_c_s"].append(xt_s[:, SUBLANES - (CONV_W - 1):])
        else:
            h, vn_s = _gmlp(h, m_p, t_s, g_mix[li], bf(w_in_d[c]), b_in_d[c], ln_g_d[c], ln_b_d[c],
                            w_s_d[c], b_s_d[c], bf(w_out_d[c]))
            outs["v_d_s"].append(vn_s.reshape(n_s, t_s, -1))

        j = li // 2
        if li % 2 == 0:
            h = _ffn_dense(h, g_ffn[li], bf(w_gate_ff[j]), bf(w_up_ff[j]), bf(w_down_ff[j]))
        else:
            wr = jnp.pad(w_router[j], ((0, 0), (0, LANES - n_exp)))
            h = _moe(h, g_ffn[li], wr, bf(w_gate_e[j]), bf(w_up_e[j]), bf(w_down_e[j]))
        last = li == depth - 1
        res = _ple(h, pp_all, ps_all, li, g_ple[li], bf(w_ple[li]), bf(w_ple_gate[li]),
                   g_final if last else None)
        if last:
            y = res
        else:
            h = res

    st = {name: jnp.stack(rows) for name, rows in outs.items()}
    return (y[0].reshape(n_b, t, d), y[1].reshape(n_s, t_s, d),
            st["k_a_p"], st["v_a_p"], st["k_a_s"], st["v_a_s"],
            st["s_b_p"], st["conv_b_p"], st["s_b_s"], st["conv_b_s"],
            st["h_c_p"], st["conv_c_p"], st["h_c_s"], st["conv_c_s"],
            st["v_d_s"])
```

```python
import functools

import jax
import jax.numpy as jnp
from jax import lax
from jax.experimental import pallas as pl
from jax.experimental.pallas import tpu as pltpu

F32 = jnp.float32
BF16 = jnp.bfloat16
EPS = 1e-6
CHUNK = 64
CONV_W = 4
A_REL_CLIP = 128
A_PREV_CHUNKS = 8
LRU_C = 8.0
D_CHUNK = 128
TOP_K = 2
LANES = 128
SUBLANES = 8
VMEM_LIMIT = 48 * 1024 * 1024


def _cp(*sem, vmem=VMEM_LIMIT):
    return pltpu.CompilerParams(dimension_semantics=sem, vmem_limit_bytes=vmem)


def _tile_m(m):
    for t in (1024, 512, 256):
        if m % t == 0:
            return t
    raise ValueError(f"token count {m} must be a multiple of 256")


def _mm(a, b):
    return jnp.dot(a.astype(BF16), b.astype(BF16), preferred_element_type=F32)


def _mm_nt(a, b):
    return lax.dot_general(a.astype(BF16), b.astype(BF16), (((1,), (1,)), ((), ())),
                           preferred_element_type=F32)


def _mm_tn(a, b):
    return lax.dot_general(a.astype(BF16), b.astype(BF16), (((0,), (0,)), ((), ())),
                           preferred_element_type=F32)


def _split2(x):
    hi = x.astype(BF16)
    lo = (x - hi.astype(F32)).astype(BF16)
    return hi, lo


def _split3(x):
    hi = x.astype(BF16)
    r = x - hi.astype(F32)
    mid = r.astype(BF16)
    lo = (r - mid.astype(F32)).astype(BF16)
    return hi, mid, lo


def _mm3(a, b):
    ah, al = _split2(a)
    bh, bl = _split2(b)
    d = functools.partial(jnp.dot, preferred_element_type=F32)
    return d(ah, bh) + (d(ah, bl) + d(al, bh))


def _rms(x, g):
    return x * lax.rsqrt(jnp.mean(x * x, axis=-1, keepdims=True) + EPS) * g


def _sigmoid(x):
    return 0.5 * (1.0 + jnp.tanh(0.5 * x))


def _silu(x):
    return x * _sigmoid(x)


def _softplus(x):
    return jnp.maximum(x, 0.0) + jnp.log1p(jnp.exp(-jnp.abs(x)))


def _gelu(x):
    c = 0.7978845608028654
    return x * (0.5 * (1.0 + jnp.tanh(c * (x + 0.044715 * (x * x * x)))))


def _shift_rows(x, prev8, s):
    r = pltpu.roll(x, s, axis=0)
    pr = pltpu.roll(prev8, s, axis=0)
    row8 = lax.broadcasted_iota(jnp.int32, pr.shape, 0)
    top = jnp.where(row8 < s, pr, r[0:SUBLANES])
    return jnp.concatenate([top, r[SUBLANES:]], axis=0)


def _dwconv(x, prev8, w, stage_ref=None):
    n = x.shape[0]
    if stage_ref is not None:
        stage_ref[0:SUBLANES, :] = prev8
        stage_ref[SUBLANES:SUBLANES + n, :] = x
    y = None
    for k in range(CONV_W):
        s = CONV_W - 1 - k
        if s == 0:
            xs = x
        elif stage_ref is not None:
            xs = stage_ref[SUBLANES - s:SUBLANES - s + n, :]
        else:
            xs = _shift_rows(x, prev8, s)
        term = xs * w[k:k + 1]
        y = term if y is None else y + term
    return y


def _two_source_specs(tm, k, n_head_tiles, n_grid_axes):
    if n_grid_axes == 1:
        return [pl.BlockSpec((tm, k), lambda i: (jnp.minimum(i, n_head_tiles - 1), 0)),
                pl.BlockSpec((tm, k), lambda i: (jnp.maximum(i - n_head_tiles, 0), 0))]
    return [pl.BlockSpec((tm, k), lambda i, j: (jnp.minimum(i, n_head_tiles - 1), 0)),
            pl.BlockSpec((tm, k), lambda i, j: (jnp.maximum(i - n_head_tiles, 0), 0))]


def _proj_kernel(*refs, has_aux, n_head_tiles):
    if n_head_tiles is None:
        x_ref, refs = refs[0], refs[1:]
        load_x = lambda: x_ref[...]
    else:
        xa_ref, xb_ref, refs = refs[0], refs[1], refs[2:]
        load_x = lambda: jnp.where(pl.program_id(0) >= n_head_tiles, xb_ref[...], xa_ref[...])
    if has_aux:
        g_ref, w_ref, waux_ref, o_ref, oaux_ref, hn_ref = refs
    else:
        g_ref, w_ref, o_ref, hn_ref = refs

    @pl.when(pl.program_id(1) == 0)
    def _():
        hn = _rms(load_x(), g_ref[...]).astype(BF16)
        hn_ref[...] = hn
        if has_aux:
            oaux_ref[...] = jnp.dot(hn, waux_ref[...], preferred_element_type=F32)

    o_ref[...] = jnp.dot(hn_ref[...], w_ref[...], preferred_element_type=F32)


def _proj(h, g, w, tn, waux=None, tail=None):
    m, k = h.shape
    if tail is not None:
        m += tail.shape[0]
    n = w.shape[1]
    tm = _tile_m(m)
    nht = None
    if tail is None:
        in_specs = [pl.BlockSpec((tm, k), lambda i, j: (i, 0))]
        args = [h]
    else:
        assert h.shape[0] % tm == 0 and tail.shape[0] % tm == 0
        nht = h.shape[0] // tm
        in_specs = _two_source_specs(tm, k, nht, 2)
        args = [h, tail]
    in_specs += [pl.BlockSpec((1, k), lambda i, j: (0, 0)),
                 pl.BlockSpec((k, tn), lambda i, j: (0, j))]
    args += [g.reshape(1, k), w]
    out_shape = [jax.ShapeDtypeStruct((m, n), F32)]
    out_specs = [pl.BlockSpec((tm, tn), lambda i, j: (i, j))]
    if waux is not None:
        na = waux.shape[1]
        in_specs.append(pl.BlockSpec((k, na), lambda i, j: (0, 0)))
        out_shape.append(jax.ShapeDtypeStruct((m, na), F32))
        out_specs.append(pl.BlockSpec((tm, na), lambda i, j: (i, 0)))
        args.append(waux)
    res = pl.pallas_call(
        functools.partial(_proj_kernel, has_aux=waux is not None, n_head_tiles=nht),
        grid=(m // tm, n // tn),
        in_specs=in_specs, out_specs=out_specs, out_shape=out_shape,
        scratch_shapes=[pltpu.VMEM((tm, k), BF16)],
        compiler_params=_cp("parallel", "arbitrary"),
        name="proj",
    )(*args)
    return res if waux is not None else res[0]


def _outproj_kernel(*refs, n_head_tiles):
    if n_head_tiles is None:
        h_ref, o_ref, w_ref, out_ref = refs
        h = h_ref[...]
    else:
        ha_ref, hb_ref, o_ref, w_ref, out_ref = refs
        h = jnp.where(pl.program_id(0) >= n_head_tiles, hb_ref[...], ha_ref[...])
    out_ref[...] = h + jnp.dot(o_ref[...].astype(BF16), w_ref[...], preferred_element_type=F32)


def _outproj_res(h, o, w, tail=None):
    m, k = o.shape
    d = h.shape[1]
    tm = _tile_m(m)
    nht = None
    if tail is None:
        in_specs = [pl.BlockSpec((tm, d), lambda i: (i, 0))]
        args = [h]
    else:
        assert h.shape[0] % tm == 0 and tail.shape[0] % tm == 0
        nht = h.shape[0] // tm
        in_specs = _two_source_specs(tm, d, nht, 1)
        args = [h, tail]
    return pl.pallas_call(
        functools.partial(_outproj_kernel, n_head_tiles=nht),
        grid=(m // tm,),
        in_specs=in_specs + [pl.BlockSpec((tm, k), lambda i: (i, 0)),
                             pl.BlockSpec((k, d), lambda i: (0, 0))],
        out_specs=pl.BlockSpec((tm, d), lambda i: (i, 0)),
        out_shape=jax.ShapeDtypeStruct((m, d), F32),
        compiler_params=_cp("parallel"),
        name="outproj_res",
    )(*args, o, w)


def _ffn_kernel(x_ref, g_ref, wg_ref, wu_ref, wd_ref, o_ref, hn_ref):
    @pl.when(pl.program_id(1) == 0)
    def _():
        x = x_ref[...]
        hn_ref[...] = _rms(x, g_ref[...]).astype(BF16)
        o_ref[...] = x

    hn = hn_ref[...]
    gg = jnp.dot(hn, wg_ref[...], preferred_element_type=F32)
    uu = jnp.dot(hn, wu_ref[...], preferred_element_type=F32)
    hid = (_silu(gg) * uu).astype(BF16)
    o_ref[...] += jnp.dot(hid, wd_ref[...], preferred_element_type=F32)


def _ffn_dense(h, g, wg, wu, wd):
    m, d = h.shape
    f = wg.shape[1]
    tm = 512
    tf = f // 2 if (f // 2) % LANES == 0 else 256
    return pl.pallas_call(
        _ffn_kernel,
        grid=(m // tm, f // tf),
        in_specs=[pl.BlockSpec((tm, d), lambda i, j: (i, 0)),
                  pl.BlockSpec((1, d), lambda i, j: (0, 0)),
                  pl.BlockSpec((d, tf), lambda i, j: (0, j)),
                  pl.BlockSpec((d, tf), lambda i, j: (0, j)),
                  pl.BlockSpec((tf, d), lambda i, j: (j, 0))],
        out_specs=pl.BlockSpec((tm, d), lambda i, j: (i, 0)),
        out_shape=jax.ShapeDtypeStruct((m, d), F32),
        scratch_shapes=[pltpu.VMEM((tm, d), BF16)],
        compiler_params=_cp("parallel", "arbitrary"),
        name="ffn_dense",
    )(h, g.reshape(1, d), wg, wu, wd)


def _moe_kernel(x_ref, g_ref, wr_ref, tril_ref, wg_ref, wu_ref, wd_ref, o_ref, hn_ref, gates_ref, rank_ref, rankt_ref,
                *, n_experts, pass_rows):
    e = pl.program_id(1)
    n_sub, _, sub = rankt_ref.shape

    @pl.when(e == 0)
    def _():
        tril = tril_ref[...]
        lane = lax.broadcasted_iota(jnp.int32, (sub, LANES), 1)
        for s in range(n_sub):
            rs = slice(s * sub, (s + 1) * sub)
            x = x_ref[rs, :]
            hn = _rms(x, g_ref[...])
            hn_hi, hn_lo = _split2(hn)
            hn_ref[rs, :] = hn_hi
            o_ref[rs, :] = x
            both = jnp.dot(hn_hi, wr_ref[...], preferred_element_type=F32)
            logits = both[:, :LANES] + (both[:, LANES:] + jnp.dot(hn_lo, wr_ref[:, :LANES],
                                                                  preferred_element_type=F32))
            lg = jnp.where(lane < n_experts, logits, -jnp.inf)
            m1 = jnp.max(lg, axis=-1, keepdims=True)
            i1 = jnp.min(jnp.where(lg == m1, lane, LANES), axis=-1, keepdims=True)
            lg2 = jnp.where(lane == i1, -jnp.inf, lg)
            m2 = jnp.max(lg2, axis=-1, keepdims=True)
            i2 = jnp.min(jnp.where(lg2 == m2, lane, LANES), axis=-1, keepdims=True)
            e2 = jnp.exp(m2 - m1)
            w1 = 1.0 / (1.0 + e2)
            w2 = e2 / (1.0 + e2)
            gates_ref[rs, :] = jnp.where(lane == i1, w1, 0.0) + jnp.where(lane == i2, w2, 0.0)
            sel = jnp.where(lane == i1, 1.0, jnp.where(lane == i2, 1.0, 0.0))
            count = jnp.dot(tril, sel.astype(BF16), preferred_element_type=F32)
            rank = jnp.where(sel > 0.0, count - 1.0, -1.0)
            rank_ref[rs, :] = rank
            rankt_ref[s] = rank.T

    lane = lax.broadcasted_iota(jnp.int32, (sub, LANES), 1)
    for s in range(n_sub):
        rs = slice(s * sub, (s + 1) * sub)
        rank_row = rankt_ref[s, pl.ds(e, 1), :]
        n_routed = (jnp.max(rank_row) + 1.0).astype(jnp.int32)
        rank_col = jnp.sum(jnp.where(lane == e, rank_ref[rs, :], 0.0), axis=-1, keepdims=True)
        gate_col = jnp.sum(jnp.where(lane == e, gates_ref[rs, :], 0.0), axis=-1, keepdims=True)

        def one_pass(cap, base, rs=rs, rank_row=rank_row, rank_col=rank_col, gate_col=gate_col):
            row_id = lax.broadcasted_iota(jnp.int32, (cap, sub), 0).astype(F32) + base
            col_id = lax.broadcasted_iota(jnp.int32, (sub, cap), 1).astype(F32) + base
            pick = jnp.where(rank_row == row_id, 1.0, 0.0).astype(BF16)
            xe = jnp.dot(pick, hn_ref[rs, :], preferred_element_type=F32).astype(BF16)
            gg = jnp.dot(xe, wg_ref[0], preferred_element_type=F32)
            uu = jnp.dot(xe, wu_ref[0], preferred_element_type=F32)
            hid = (_silu(gg) * uu).astype(BF16)
            ye = jnp.dot(hid, wd_ref[0], preferred_element_type=F32).astype(BF16)
            put = jnp.where(rank_col == col_id, gate_col, 0.0).astype(BF16)
            o_ref[rs, :] += jnp.dot(put, ye, preferred_element_type=F32)

        *single, full = pass_rows
        below = 0
        for rows_ in single:
            @pl.when((n_routed > below) & (n_routed <= rows_))
            def _(one_pass=one_pass, rows_=rows_):
                one_pass(rows_, 0.0)
            below = rows_

        @pl.when(n_routed > below)
        def _(one_pass=one_pass, n_routed=n_routed):
            def body(cc, carry):
                one_pass(full, (cc * full).astype(F32))
                return carry
            lax.fori_loop(0, (n_routed + full - 1) // full, body, 0)


def _moe(h, g, wr, wg, wu, wd):
    m, d = h.shape
    ne, _, ef = wg.shape
    sub = 512
    tm = 2 * sub if m % (2 * sub) == 0 else sub
    mean_rows = sub * TOP_K // ne
    pass_rows = tuple(mean_rows * k // 4 for k in (4, 5, 6, 7, 8))
    assert m % tm == 0
    return pl.pallas_call(
        functools.partial(_moe_kernel, n_experts=ne, pass_rows=pass_rows),
        grid=(m // tm, ne),
        in_specs=[pl.BlockSpec((tm, d), lambda i, e: (i, 0)),
                  pl.BlockSpec((1, d), lambda i, e: (0, 0)),
                  pl.BlockSpec((d, 2 * LANES), lambda i, e: (0, 0)),
                  pl.BlockSpec((sub, sub), lambda i, e: (0, 0)),
                  pl.BlockSpec((1, d, ef), lambda i, e: (e, 0, 0)),
                  pl.BlockSpec((1, d, ef), lambda i, e: (e, 0, 0)),
                  pl.BlockSpec((1, ef, d), lambda i, e: (e, 0, 0))],
        out_specs=pl.BlockSpec((tm, d), lambda i, e: (i, 0)),
        out_shape=jax.ShapeDtypeStruct((m, d), F32),
        scratch_shapes=[pltpu.VMEM((tm, d), BF16), pltpu.VMEM((tm, LANES), F32),
                        pltpu.VMEM((tm, LANES), F32), pltpu.VMEM((tm // sub, LANES, sub), F32)],
        compiler_params=_cp("parallel", "arbitrary"),
        name="moe",
    )(h, g.reshape(1, d), jnp.concatenate(_split2(wr), axis=1), jnp.tril(jnp.ones((sub, sub), BF16)), wg, wu, wd)


def _ple_kernel(*refs, final, n_prompt_tiles):
    if final:
        x_ref, pp_ref, ps_ref, g_ref, wp_ref, wg_ref, gf_ref, yp_ref, ys_ref = refs
    else:
        x_ref, pp_ref, ps_ref, g_ref, wp_ref, wg_ref, o_ref = refs
    i = pl.program_id(0)
    is_sample = i >= n_prompt_tiles
    x = x_ref[...]
    hn = _rms(x, g_ref[...]).astype(BF16)
    gate = _sigmoid(jnp.dot(hn, wg_ref[...], preferred_element_type=F32))
    p = jnp.where(is_sample, ps_ref[...], pp_ref[...])
    pp = jnp.dot(p.astype(BF16), wp_ref[...], preferred_element_type=F32)
    out = x + pp * gate
    if final:
        y = _rms(out, gf_ref[...])

        @pl.when(jnp.logical_not(is_sample))
        def _():
            yp_ref[...] = y

        @pl.when(is_sample)
        def _():
            ys_ref[...] = y
    else:
        o_ref[...] = out


def _ple(h, p_prompt, p_sample, layer, g, wp, wgate, g_final=None):
    m, d = h.shape
    _, m_p, pd = p_prompt.shape
    m_s = p_sample.shape[1]
    tm = _tile_m(m)
    assert m_p % tm == 0 and m_s % tm == 0
    npt, nst = m_p // tm, m_s // tm
    final = g_final is not None
    in_specs = [pl.BlockSpec((tm, d), lambda i: (i, 0)),
                pl.BlockSpec((None, tm, pd), lambda i: (layer, jnp.minimum(i, npt - 1), 0)),
                pl.BlockSpec((None, tm, pd), lambda i: (layer, jnp.maximum(i - npt, 0), 0)),
                pl.BlockSpec((1, d), lambda i: (0, 0)),
                pl.BlockSpec((pd, d), lambda i: (0, 0)),
                pl.BlockSpec((d, d), lambda i: (0, 0))]
    args = [h, p_prompt, p_sample, g.reshape(1, d), wp, wgate]
    if final:
        in_specs.append(pl.BlockSpec((1, d), lambda i: (0, 0)))
        args.append(g_final.reshape(1, d))
        out_specs = [pl.BlockSpec((tm, d), lambda i: (jnp.minimum(i, npt - 1), 0)),
                     pl.BlockSpec((tm, d), lambda i: (jnp.maximum(i - npt, 0), 0))]
        out_shape = [jax.ShapeDtypeStruct((m_p, d), F32), jax.ShapeDtypeStruct((m_s, d), F32)]
    else:
        out_specs = pl.BlockSpec((tm, d), lambda i: (i, 0))
        out_shape = jax.ShapeDtypeStruct((m, d), F32)
    return pl.pallas_call(
        functools.partial(_ple_kernel, final=final, n_prompt_tiles=npt),
        grid=(npt + nst,),
        in_specs=in_specs, out_specs=out_specs, out_shape=out_shape,
        compiler_params=_cp("arbitrary"),
        name="ple",
    )(*args)


def _pair_rms(x, g, hd):
    lane = lax.broadcasted_iota(jnp.int32, x.shape, 1)
    lo = lane < hd
    x2 = x * x
    s0 = jnp.sum(jnp.where(lo, x2, 0.0), axis=-1, keepdims=True)
    s1 = jnp.sum(jnp.where(lo, 0.0, x2), axis=-1, keepdims=True)
    ms = jnp.where(lo, s0, s1) * (1.0 / hd)
    return x * lax.rsqrt(ms + EPS) * g


def _stack_heads(q, hd):
    lane = lax.broadcasted_iota(jnp.int32, q.shape, 1)
    lo = lane < hd
    return jnp.concatenate([jnp.where(lo, q, 0.0), jnp.where(lo, 0.0, q)], axis=0).astype(BF16)


def _unstack_heads(o2, hd):
    n = o2.shape[0] // 2
    lane = lax.broadcasted_iota(jnp.int32, (n, o2.shape[1]), 1)
    return jnp.where(lane < hd, o2[:n], o2[n:])


def _attend(q2, kw, vw, bias, valid):
    s = _mm_nt(q2, kw) + bias
    if valid is not None:
        s = jnp.where(valid, s, -jnp.inf)
    mx = jnp.max(s, axis=-1, keepdims=True)
    e = jnp.exp(s - mx)
    den = jnp.sum(e, axis=-1, keepdims=True)
    return jnp.dot(e.astype(BF16), vw, preferred_element_type=F32) / den


def _attn_prompt_kernel(q_ref, k_ref, v_ref, b_ref, gq_ref, gk_ref, o_ref, kn_ref, kwin_ref, vwin_ref,
                        *, hd, scale, reach, chunk_group):
    i = pl.program_id(2)
    tq, width = q_ref.shape
    band = reach + CHUNK
    pw = 2 * hd
    lanes = [slice(p * pw, (p + 1) * pw) for p in range(width // pw)]
    qn = [_pair_rms(q_ref[:, c], gq_ref[...], hd) * scale for c in lanes]
    knc = [_pair_rms(k_ref[:, c], gk_ref[...], hd) for c in lanes]

    @pl.when(i == pl.num_programs(2) - 1)
    def _():
        for c, x in zip(lanes, knc):
            kn_ref[:, c] = x

    @pl.when(i == 0)
    def _():
        kwin_ref[0:tq, :] = jnp.zeros((tq, width), BF16)
        vwin_ref[0:tq, :] = jnp.zeros((tq, width), BF16)

    @pl.when(i > 0)
    def _():
        kwin_ref[0:tq, :] = kwin_ref[tq:2 * tq, :]
        vwin_ref[0:tq, :] = vwin_ref[tq:2 * tq, :]

    for c, x in zip(lanes, knc):
        kwin_ref[tq:2 * tq, c] = x.astype(BF16)
    vwin_ref[tq:2 * tq, :] = v_ref[...].astype(BF16)
    col = lax.broadcasted_iota(jnp.int32, (1, band), 1)
    n_chunks = tq // CHUNK
    w0s = [tq - reach + jj * CHUNK for jj in range(n_chunks)]

    def attend(first_block, items):
        q2 = [_stack_heads(qn[p][jj * CHUNK:(jj + 1) * CHUNK], hd) for p, jj in items]
        s = [_mm_nt(x, kwin_ref[w0s[jj]:w0s[jj] + band, lanes[p]]) + b_ref[p] for x, (p, jj) in zip(q2, items)]
        if first_block:
            s = [jnp.where(col + w0s[jj] >= tq, x, -jnp.inf) for x, (p, jj) in zip(s, items)]
        mx = [jnp.max(x, axis=-1, keepdims=True) for x in s]
        e = [jnp.exp(x - m) for x, m in zip(s, mx)]
        den = [jnp.sum(x, axis=-1, keepdims=True) for x in e]
        o2 = [jnp.dot(x.astype(BF16), vwin_ref[w0s[jj]:w0s[jj] + band, lanes[p]], preferred_element_type=F32)
              for x, (p, jj) in zip(e, items)]
        for x, dn, (p, jj) in zip(o2, den, items):
            o_ref[jj * CHUNK:(jj + 1) * CHUNK, lanes[p]] = _unstack_heads(x / dn, hd)

    def attend_all(first_block):
        items = [(p, jj) for jj in range(n_chunks) for p in range(len(lanes))]
        for j0 in range(0, len(items), chunk_group):
            attend(first_block, items[j0:j0 + chunk_group])

    @pl.when(i == 0)
    def _():
        attend_all(True)

    @pl.when(i > 0)
    def _():
        attend_all(False)


def _attn_sample_kernel(q_ref, k_ref, v_ref, ck_ref, cv_ref, b_ref, gq_ref, gk_ref, oin_ref, o_ref, kn_ref,
                        *, hd, scale):
    del oin_ref
    pw = 2 * hd
    pairs = [slice(p * pw, (p + 1) * pw) for p in range(q_ref.shape[1] // pw)]
    qn = [_pair_rms(q_ref[:, c], gq_ref[...], hd) * scale for c in pairs]
    kn = [_pair_rms(k_ref[:, c], gk_ref[...], hd) for c in pairs]
    for c, x in zip(pairs, kn):
        kn_ref[:, c] = x
    kcat = [jnp.concatenate([ck_ref[:, c], x], axis=0).astype(BF16) for c, x in zip(pairs, kn)]
    vcat = [jnp.concatenate([cv_ref[:, c], v_ref[:, c]], axis=0).astype(BF16) for c in pairs]
    s = [_mm_nt(_stack_heads(x, hd), kk) + b_ref[p] for p, (x, kk) in enumerate(zip(qn, kcat))]
    mx = [jnp.max(x, axis=-1, keepdims=True) for x in s]
    e = [jnp.exp(x - m) for x, m in zip(s, mx)]
    den = [jnp.sum(x, axis=-1, keepdims=True) for x in e]
    o2 = [jnp.dot(x.astype(BF16), vv, preferred_element_type=F32) for x, vv in zip(e, vcat)]
    for c, x, dn in zip(pairs, o2, den):
        o_ref[:, c] = _unstack_heads(x / dn, hd)


def _rel_bias_pairs(table, q0, nq, nk):
    nh = table.shape[0]
    span = nq + nk - 1
    dmin, dmax = q0 - (nk - 1), q0 + nq - 1
    lo, hi = max(dmin, -A_REL_CLIP), min(dmax, A_REL_CLIP)
    line = jnp.concatenate([jnp.broadcast_to(table[:, :1], (nh, lo - dmin)),
                            table[:, lo + A_REL_CLIP:hi + A_REL_CLIP + 1],
                            jnp.broadcast_to(table[:, -1:], (nh, dmax - hi))], axis=1)
    flipped = line[:, ::-1]
    period = jnp.concatenate([flipped, jnp.zeros((nh, 1), table.dtype)], axis=1)
    shifted = jnp.tile(period, (1, nq))[:, :nq * span].reshape(nh, nq, span)
    b = shifted[:, :, nq - 1:nq - 1 + nk]
    return b.reshape(nh // 2, 2 * nq, nk)


def _attention(qkv, n_b, t, n_s, t_s, cache_k, cache_v, rel, g_q, g_k, n_heads):
    m_tot, d3 = qkv.shape
    d = d3 // 3
    hd = d // n_heads
    pw = 2 * hd
    n_pairs = d // pw
    m_p = n_b * t
    reach = A_PREV_CHUNKS * CHUNK
    tq = reach
    assert pw == LANES and t % tq == 0 and t_s % 16 == 0
    nqb = t // tq
    scale = hd ** -0.5
    gq2 = jnp.tile(g_q, 2).reshape(1, pw)
    gk2 = jnp.tile(g_k, 2).reshape(1, pw)

    bias_p = _rel_bias_pairs(rel, reach, CHUNK, reach + CHUNK)
    kcol, vcol = n_pairs, 2 * n_pairs

    def rows(b, p, i):
        return b * nqb + i

    ppb = 2 if n_pairs % 2 == 0 else 1
    npb = n_pairs // ppb
    blk = (tq, ppb * pw)
    o_all, kn_p = pl.pallas_call(
        functools.partial(_attn_prompt_kernel, hd=hd, scale=scale, reach=reach, chunk_group=4),
        grid=(n_b, npb, nqb),
        in_specs=[pl.BlockSpec(blk, lambda b, p, i: (rows(b, p, i), p)),
                  pl.BlockSpec(blk, lambda b, p, i: (rows(b, p, i), npb + p)),
                  pl.BlockSpec(blk, lambda b, p, i: (rows(b, p, i), 2 * npb + p)),
                  pl.BlockSpec((ppb, 2 * CHUNK, reach + CHUNK), lambda b, p, i: (p, 0, 0)),
                  pl.BlockSpec((1, pw), lambda b, p, i: (0, 0)),
                  pl.BlockSpec((1, pw), lambda b, p, i: (0, 0))],
        out_specs=[pl.BlockSpec(blk, lambda b, p, i: (rows(b, p, i), p)),
                   pl.BlockSpec(blk, lambda b, p, i: (b, p))],
        out_shape=[jax.ShapeDtypeStruct((m_tot, d), F32),
                   jax.ShapeDtypeStruct((n_b * tq, d), F32)],
        scratch_shapes=[pltpu.VMEM((2 * tq, ppb * pw), BF16), pltpu.VMEM((2 * tq, ppb * pw), BF16)],
        compiler_params=_cp("parallel", "parallel", "arbitrary"),
        name="attn_prompt",
    )(qkv, qkv, qkv, bias_p, gq2, gk2)

    w = cache_k.shape[1]
    bias_s = _rel_bias_pairs(rel, w, t_s, w + t_s)
    srow = m_p // t_s
    sblk = (t_s, d)
    o_all, kn_s = pl.pallas_call(
        functools.partial(_attn_sample_kernel, hd=hd, scale=scale),
        grid=(n_s,),
        in_specs=[pl.BlockSpec(sblk, lambda b: (srow + b, 0)),
                  pl.BlockSpec(sblk, lambda b: (srow + b, 1)),
                  pl.BlockSpec(sblk, lambda b: (srow + b, 2)),
                  pl.BlockSpec((None, w, d), lambda b: (b, 0, 0)),
                  pl.BlockSpec((None, w, d), lambda b: (b, 0, 0)),
                  pl.BlockSpec((n_pairs, 2 * t_s, w + t_s), lambda b: (0, 0, 0)),
                  pl.BlockSpec((1, pw), lambda b: (0, 0)),
                  pl.BlockSpec((1, pw), lambda b: (0, 0)),
                  pl.BlockSpec(memory_space=pl.ANY)],
        out_specs=[pl.BlockSpec(sblk, lambda b: (srow + b, 0)),
                   pl.BlockSpec(sblk, lambda b: (b, 0))],
        out_shape=[jax.ShapeDtypeStruct((m_tot, d), F32),
                   jax.ShapeDtypeStruct((n_s * t_s, d), F32)],
        input_output_aliases={8: 0},
        compiler_params=_cp("parallel"),
        name="attn_sample",
    )(qkv, qkv, qkv, cache_k, cache_v, bias_s, gq2, gk2, o_all)
    return o_all, kn_p, kn_s


def _gdn_act_kernel(xq_ref, xk_ref, xv_ref, pq_ref, pk_ref, pv_ref, hq_ref, hk_ref, hv_ref,
                    aux_ref, cw_ref, pvec_ref, o_ref, bg_ref, stage_ref, *, blocks_per_seq, n_heads, scale):
    i = pl.program_id(0)
    start = (i % blocks_per_seq) == 0
    d = xq_ref.shape[1]
    hd = d // n_heads
    parts = ((xq_ref, pq_ref, hq_ref), (xk_ref, pk_ref, hk_ref), (xv_ref, pv_ref, hv_ref))
    for c, (x_ref, p_ref, h_ref) in enumerate(parts):
        prev8 = jnp.where(start, h_ref[...], p_ref[...])
        a = _silu(_dwconv(x_ref[...], prev8, cw_ref[:, c * d:(c + 1) * d], stage_ref))
        for hh in range(n_heads):
            ah = a[:, hh * hd:(hh + 1) * hd]
            if c < 2:
                ah = ah * lax.rsqrt(jnp.sum(ah * ah, axis=-1, keepdims=True) + EPS)
            if c == 0:
                ah = ah * scale
            o_ref[:, c * d + hh * hd:c * d + (hh + 1) * hd] = ah
    ax = aux_ref[...]
    lane = lax.broadcasted_iota(jnp.int32, ax.shape, 1)
    beta = _sigmoid(ax)
    gdec = -jnp.exp(pvec_ref[0:1, :]) * _softplus(ax + pvec_ref[1:2, :])
    bg_ref[...] = jnp.where(lane < n_heads, beta, jnp.where(lane < 2 * n_heads, gdec, 0.0))


def _gdn_act(proj, aux, row0, n_seq, t, tb, hist8, conv_w, pvec, n_heads):
    d = proj.shape[1] // 4
    hd = d // n_heads
    bps = t // tb
    r0 = row0 // tb
    p8 = tb // SUBLANES

    def xspec(c):
        return pl.BlockSpec((tb, d), lambda i, c=c: (r0 + i, c))

    def pspec(c):
        return pl.BlockSpec((SUBLANES, d), lambda i, c=c: (jnp.maximum((r0 + i) * p8 - 1, 0), c))

    def hspec(c):
        return pl.BlockSpec((None, SUBLANES, d), lambda i, c=c: (i // bps, 0, c))

    m = n_seq * t
    return pl.pallas_call(
        functools.partial(_gdn_act_kernel, blocks_per_seq=bps, n_heads=n_heads, scale=hd ** -0.5),
        grid=(m // tb,),
        in_specs=[xspec(0), xspec(1), xspec(2), pspec(0), pspec(1), pspec(2), hspec(0), hspec(1), hspec(2),
                  pl.BlockSpec((tb, LANES), lambda i: (r0 + i, 0)),
                  pl.BlockSpec((CONV_W, 3 * d), lambda i: (0, 0)),
                  pl.BlockSpec((2, LANES), lambda i: (0, 0))],
        out_specs=[pl.BlockSpec((tb, 3 * d), lambda i: (i, 0)),
                   pl.BlockSpec((tb, LANES), lambda i: (i, 0))],
        out_shape=[jax.ShapeDtypeStruct((m, 3 * d), F32), jax.ShapeDtypeStruct((m, LANES), F32)],
        scratch_shapes=[pltpu.VMEM((SUBLANES + tb, d), F32)],
        compiler_params=_cp("parallel"),
        name="gdn_act",
    )(proj, proj, proj, proj, proj, proj, hist8, hist8, hist8, aux, conv_w, pvec)


def _seg_cumsum(x, rowmod, reverse):
    n = x.shape[0]
    d = 1
    while d < CHUNK:
        if reverse:
            sh = pltpu.roll(x, n - d, axis=0)
            x = x + jnp.where(rowmod < CHUNK - d, sh, 0.0)
        else:
            sh = pltpu.roll(x, d, axis=0)
            x = x + jnp.where(rowmod >= d, sh, 0.0)
        d *= 2
    return x


def _gdn_delta_kernel(act_ref, z_ref, bg_ref, s0_ref, go_ref, o_ref, sout_ref, s_ref, *, n_heads, chained, head_group):
    i = pl.program_id(1)
    tb, d = z_ref.shape
    hd = d // n_heads
    nch = tb // CHUNK
    dot = functools.partial(jnp.dot, preferred_element_type=F32)

    if chained:
        @pl.when(i == 0)
        def _():
            s_ref[...] = s0_ref[...]

    bg = bg_ref[...]
    lane = lax.broadcasted_iota(jnp.int32, (tb, hd), 1)
    rowmod = lax.broadcasted_iota(jnp.int32, (tb, hd), 0) & (CHUNK - 1)
    ri = lax.broadcasted_iota(jnp.int32, (tb, tb), 0)
    ci = lax.broadcasted_iota(jnp.int32, (tb, tb), 1)
    same = (ri >> 6) == (ci >> 6)
    low_incl = same & (ri >= ci)
    strict = same & (ri > ci)
    eye_c = (lax.broadcasted_iota(jnp.int32, (CHUNK, tb), 0)
             == (lax.broadcasted_iota(jnp.int32, (CHUNK, tb), 1) & (CHUNK - 1))).astype(F32)
    def compact(x):
        acc = x[0:CHUNK]
        for c in range(1, nch):
            acc = acc + x[c * CHUNK:(c + 1) * CHUNK]
        return acc

    def expand(xc):
        return jnp.where(same, jnp.concatenate([xc] * nch, axis=0), jnp.zeros((), xc.dtype))

    def each(fn, *lists):
        return [fn(*a) for a in zip(*lists)]

    cum_fwd = _seg_cumsum(bg, rowmod, reverse=False)
    cum_rev = _seg_cumsum(bg, rowmod, reverse=True) - bg

    def column(arr, idx):
        return jnp.broadcast_to(jnp.sum(jnp.where(lane == idx, arr, 0.0), axis=-1, keepdims=True), (tb, hd))

    def gc_diff(gc):
        gh, gm, gl = (piece.astype(F32) for piece in _split3(gc))
        lhs = jnp.where(lane == 0, gh, jnp.where(lane == 1, gm, jnp.where(lane == 2, gl,
                                                                          jnp.where(lane < 6, 1.0, 0.0))))
        rhs_t = jnp.where(lane < 3, 1.0, jnp.where(lane == 3, -gh, jnp.where(lane == 4, -gm,
                                                                             jnp.where(lane == 5, -gl, 0.0))))
        return _mm_nt(lhs, rhs_t)

    for h0 in range(0, n_heads, head_group):
        hs = list(range(h0, h0 + head_group))
        q = [act_ref[:, h * hd:(h + 1) * hd] for h in hs]
        k = [act_ref[:, d + h * hd:d + (h + 1) * hd] for h in hs]
        v = [act_ref[:, 2 * d + h * hd:2 * d + (h + 1) * hd] for h in hs]
        beta = [column(bg, h) for h in hs]
        gc = [column(cum_fwd, n_heads + h) for h in hs]
        grev = [column(cum_rev, n_heads + h) for h in hs]
        decay = each(lambda x: jnp.exp(jnp.where(low_incl, gc_diff(x), -jnp.inf)), gc)
        kb = each(lambda a, b: a * b, k, beta)
        lmat = each(lambda a, b, dc: jnp.where(strict, _mm_nt(a, b) * dc, 0.0), kb, k, decay)
        egc = each(jnp.exp, gc)
        rhs = each(lambda a, b, c_, e: jnp.concatenate([a * b, c_ * e], axis=1), v, beta, kb, egc)

        xc = each(compact, lmat)
        tinv = each(lambda x: eye_c - x, xc)
        xh = each(lambda x: x.astype(BF16), xc)
        xe = each(expand, xh)
        p = 1
        while 2 * p < CHUNK:
            xh = each(lambda a, b: dot(a, b).astype(BF16), xh, xe)
            xe = each(expand, xh)
            tinv = each(lambda t, b: t + dot(t.astype(BF16), b), tinv, xe)
            p *= 2
        sol = each(lambda t, r: dot(expand(t.astype(BF16)), r.astype(BF16)), tinv, rhs)

        qk = each(lambda a, b, dc: jnp.where(low_incl, _mm_nt(a, b) * dc, 0.0), q, k, decay)
        q_dec = each(lambda a, e: a * e, q, egc)
        k_dec = each(lambda a, r: a * jnp.exp(r), k, grev)
        g_last = each(lambda a, r: jnp.exp(a + r), gc, grev)

        v_news = [[] for _ in hs]
        o_inter = [[] for _ in hs]
        s = [s_ref[h] for h in hs] if chained else None
        for c in range(nch):
            rc = slice(c * CHUNK, (c + 1) * CHUNK)
            if not chained:
                s = [s0_ref[c, h] for h in hs]
            v_new = each(lambda so, st: so[rc, :hd] - _mm(so[rc, hd:], st), sol, s)
            o_c = each(lambda qd, st: _mm(qd[rc], st), q_dec, s)
            s = each(lambda st, gl, kd, vn: st * gl[c * CHUNK:c * CHUNK + 1, :] + _mm_tn(kd[rc], vn),
                     s, g_last, k_dec, v_new)
            for j, h in enumerate(hs):
                v_news[j].append(v_new[j])
                o_inter[j].append(o_c[j])
                if not chained:
                    sout_ref[c, h] = s[j]
        if chained:
            for j, h in enumerate(hs):
                s_ref[h] = s[j]

            @pl.when(i == pl.num_programs(1) - 1)
            def _(s=s, hs=hs):
                for j, h in enumerate(hs):
                    sout_ref[h] = s[j]

        o = each(lambda oi, m, vn: jnp.concatenate(oi, axis=0) + _mm(m, jnp.concatenate(vn, axis=0)),
                 o_inter, qk, v_news)
        for j, h in enumerate(hs):
            o_ref[:, h * hd:(h + 1) * hd] = _rms(o[j], go_ref[...]) * _silu(z_ref[:, h * hd:(h + 1) * hd])


def _gdn_delta(act, bg, zsrc, zcol, n_seq, t, s0, g_o, n_heads, out_rows=None):
    d = act.shape[1] // 3
    hd = d // n_heads
    tb = 4 * CHUNK
    chained = t > CHUNK
    if chained:
        assert t % tb == 0
        grid = (n_seq, t // tb)
        rows = lambda b, i: b * (t // tb) + i
        sspec = pl.BlockSpec((None, n_heads, hd, hd), lambda b, i: (b, 0, 0, 0))
    else:
        assert t == CHUNK and n_seq % (tb // CHUNK) == 0
        grid = (n_seq * t // tb, 1)
        rows = lambda b, i: b
        sspec = pl.BlockSpec((tb // CHUNK, n_heads, hd, hd), lambda b, i: (b, 0, 0, 0))
    return pl.pallas_call(
        functools.partial(_gdn_delta_kernel, n_heads=n_heads, chained=chained, head_group=8),
        grid=grid,
        in_specs=[pl.BlockSpec((tb, 3 * d), lambda b, i: (rows(b, i), 0)),
                  pl.BlockSpec((tb, d), lambda b, i: (rows(b, i), zcol)),
                  pl.BlockSpec((tb, LANES), lambda b, i: (rows(b, i), 0)),
                  sspec,
                  pl.BlockSpec((1, hd), lambda b, i: (0, 0))],
        out_specs=[pl.BlockSpec((tb, d), lambda b, i: (rows(b, i), 0)), sspec],
        out_shape=[jax.ShapeDtypeStruct((out_rows or n_seq * t, d), F32),
                   jax.ShapeDtypeStruct((n_seq, n_heads, hd, hd), F32)],
        scratch_shapes=[pltpu.VMEM((n_heads, hd, hd), F32)],
        compiler_params=_cp("parallel", "arbitrary"),
        name="gdn_delta",
    )(act, zsrc, bg, s0, g_o.reshape(1, hd))


def _rglru_kernel(h_ref, g_ref, win_ref, cw_ref, cb_ref, wgate_ref, bra_ref, bri_ref, lam_ref, wout_ref,
                  hist_ref, h0_ref, oin_ref, o_ref, hlast_ref, xtail_ref, cx_ref, ch_ref, *, n_blocks):
    del oin_ref
    i = pl.program_id(1)
    tb, d = h_ref.shape
    cw = lam_ref.shape[1]
    blk = cw // n_blocks

    @pl.when(i == 0)
    def _():
        cx_ref[...] = hist_ref[...]
        ch_ref[...] = h0_ref[...]

    h = h_ref[...]
    proj = jnp.dot(_rms(h, g_ref[...]).astype(BF16), win_ref[...], preferred_element_type=F32)
    y_br = proj[:, :cw]
    x_br = proj[:, cw:]
    xc = _dwconv(x_br, cx_ref[...], cw_ref[...]) + cb_ref[...]
    r_parts, i_parts = [], []
    for n in range(n_blocks):
        ri = jnp.dot(xc[:, n * blk:(n + 1) * blk].astype(BF16), wgate_ref[n], preferred_element_type=F32)
        r_parts.append(ri[:, :blk])
        i_parts.append(ri[:, blk:])
    r = _sigmoid(jnp.concatenate(r_parts, axis=1) + bra_ref[...])
    ig = _sigmoid(jnp.concatenate(i_parts, axis=1) + bri_ref[...])
    log_a = -LRU_C * r * _softplus(-lam_ref[...])
    a = jnp.exp(log_a)
    th = jnp.tanh(log_a)
    b = jnp.sqrt(-2.0 * th / (1.0 - th)) * (ig * xc)
    row8 = lax.broadcasted_iota(jnp.int32, (tb, cw), 0) & (SUBLANES - 1)
    step = 1
    while step < SUBLANES:
        a_s = jnp.where(row8 >= step, pltpu.roll(a, step, axis=0), 1.0)
        b_s = jnp.where(row8 >= step, pltpu.roll(b, step, axis=0), 0.0)
        b = a * b_s + b
        a = a * a_s
        step *= 2
    h_in = ch_ref[SUBLANES - 1:SUBLANES, :]
    groups = []
    for j in range(tb // SUBLANES):
        rows = slice(j * SUBLANES, (j + 1) * SUBLANES)
        hg = b[rows] + a[rows] * h_in
        groups.append(hg)
        h_in = hg[SUBLANES - 1:SUBLANES, :]
    hs = jnp.concatenate(groups, axis=0)
    cx_ref[...] = x_br[tb - SUBLANES:, :]
    ch_ref[...] = hs[tb - SUBLANES:, :]

    @pl.when(i == pl.num_programs(1) - 1)
    def _():
        hlast_ref[...] = hs[tb - SUBLANES:, :]
        xtail_ref[...] = x_br[tb - SUBLANES:, :]

    o_ref[...] = h + jnp.dot((hs * _gelu(y_br)).astype(BF16), wout_ref[...], preferred_element_type=F32)


def _rglru(h, h_out, row0, n_seq, t, tb, g, w_in, conv_w, conv_b, wgate, b_ra, b_ri, lam, w_out, hist8, h08):
    m, d = h.shape
    cw = lam.shape[0]
    nb = wgate.shape[0]
    nt = t // tb
    r0 = row0 // tb
    vec = lambda x: x.reshape(1, -1)
    cst = lambda shape: pl.BlockSpec(shape, lambda b, i: (0,) * len(shape))
    in_specs = [pl.BlockSpec((tb, d), lambda b, i: (r0 + b * nt + i, 0)),
                cst((1, d)), cst((d, 2 * cw)), cst((CONV_W, cw)), cst((1, cw)),
                cst(wgate.shape), cst((1, cw)), cst((1, cw)), cst((1, cw)), cst((cw, d)),
                pl.BlockSpec((None, SUBLANES, cw), lambda b, i: (b, 0, 0)),
                pl.BlockSpec((None, SUBLANES, cw), lambda b, i: (b, 0, 0))]
    args = [h, vec(g), w_in, conv_w, vec(conv_b), wgate, vec(b_ra), vec(b_ri), vec(lam), w_out, hist8, h08]
    aliases = {}
    if h_out is not None:
        in_specs.append(pl.BlockSpec(memory_space=pl.ANY))
        args.append(h_out)
        aliases = {len(args) - 1: 0}
    kern = functools.partial(_rglru_kernel, n_blocks=nb)
    if h_out is None:
        kern = lambda *refs: _rglru_kernel(*refs[:12], None, *refs[12:], n_blocks=nb)
    return pl.pallas_call(
        kern,
        grid=(n_seq, nt),
        in_specs=in_specs,
        out_specs=[pl.BlockSpec((tb, d), lambda b, i: (r0 + b * nt + i, 0)),
                   pl.BlockSpec((None, SUBLANES, cw), lambda b, i: (b, 0, 0)),
                   pl.BlockSpec((None, SUBLANES, cw), lambda b, i: (b, 0, 0))],
        out_shape=[jax.ShapeDtypeStruct((m, d), F32),
                   jax.ShapeDtypeStruct((n_seq, SUBLANES, cw), F32),
                   jax.ShapeDtypeStruct((n_seq, SUBLANES, cw), F32)],
        scratch_shapes=[pltpu.VMEM((SUBLANES, cw), F32), pltpu.VMEM((SUBLANES, cw), F32)],
        input_output_aliases=aliases,
        compiler_params=_cp("parallel", "arbitrary"),
        name="rglru",
    )(*args)


def _gmlp_kernel(h_ref, g_ref, win_ref, bin_ref, lng_ref, lnb_ref, ws_ref, bs_ref, wout_ref, o_ref, vn_ref,
                 *, n_groups, first_sample_tile, lc_prompt, lc_sample):
    i = pl.program_id(0)
    tm, d = h_ref.shape
    half = lng_ref.shape[1]
    gw = half // n_groups
    h = h_ref[...]
    act = _gelu(jnp.dot(_rms(h, g_ref[...]).astype(BF16), win_ref[...], preferred_element_type=F32) + bin_ref[...])
    u = act[:, :half]
    v = act[:, half:]
    vc = v - jnp.mean(v, axis=-1, keepdims=True)
    vn = vc * lax.rsqrt(jnp.mean(vc * vc, axis=-1, keepdims=True) + EPS) * lng_ref[...] + lnb_ref[...]
    vn_ref[...] = vn
    shift = jnp.where(i >= first_sample_tile, lc_sample.bit_length() - 1, lc_prompt.bit_length() - 1)
    ri = lax.broadcasted_iota(jnp.int32, (tm, tm), 0)
    ci = lax.broadcasted_iota(jnp.int32, (tm, tm), 1)
    keep = (lax.shift_right_logical(ri, shift) == lax.shift_right_logical(ci, shift)) & (ri >= ci)
    vnb = vn.astype(BF16)
    parts = []
    for gi in range(n_groups):
        ws = jnp.where(keep, ws_ref[0, gi], 0.0).astype(BF16)
        sg = jnp.dot(ws, vnb[:, gi * gw:(gi + 1) * gw], preferred_element_type=F32)
        parts.append(u[:, gi * gw:(gi + 1) * gw] * (sg + bs_ref[0, :, gi * gw:(gi + 1) * gw]))
    gated = jnp.concatenate(parts, axis=1).astype(BF16)
    o_ref[...] = h + jnp.dot(gated, wout_ref[...], preferred_element_type=F32)


def _gmlp(h, m_p, t_s, g, w_in, b_in, ln_g, ln_b, w_s, b_s, w_out):
    m, d = h.shape
    half = ln_g.shape[0]
    ng = w_s.shape[0]
    gw = half // ng
    tm = 256
    lc_p, lc_s = D_CHUNK, min(t_s, D_CHUNK)
    assert m_p % tm == 0 and (m - m_p) % tm == 0 and tm % lc_p == 0 and tm % lc_s == 0

    def tiled(lc):
        wt = jnp.tile(w_s[:, :lc, :lc], (1, tm // lc, tm // lc))
        bt = jnp.repeat(jnp.tile(b_s[:, :lc].T, (tm // lc, 1)), gw, axis=1)
        return wt, bt

    wt_p, bt_p = tiled(lc_p)
    wt_s, bt_s = tiled(lc_s)
    ws_all = jnp.stack([wt_p, wt_s])
    bs_all = jnp.stack([bt_p, bt_s])
    fst = m_p // tm
    vec = lambda x: x.reshape(1, -1)
    cst = lambda shape: pl.BlockSpec(shape, lambda i: (0,) * len(shape))
    return pl.pallas_call(
        functools.partial(_gmlp_kernel, n_groups=ng, first_sample_tile=fst, lc_prompt=lc_p, lc_sample=lc_s),
        grid=(m // tm,),
        in_specs=[pl.BlockSpec((tm, d), lambda i: (i, 0)),
                  cst((1, d)), cst((d, 2 * half)), cst((1, 2 * half)), cst((1, half)), cst((1, half)),
                  pl.BlockSpec((1, ng, tm, tm), lambda i: (i // fst, 0, 0, 0)),
                  pl.BlockSpec((1, tm, half), lambda i: (i // fst, 0, 0)),
                  cst((half, d))],
        out_specs=[pl.BlockSpec((tm, d), lambda i: (i, 0)),
                   pl.BlockSpec((tm, half), lambda i: (jnp.maximum(i - fst, 0), 0))],
        out_shape=[jax.ShapeDtypeStruct((m, d), F32),
                   jax.ShapeDtypeStruct((m - m_p, half), F32)],
        compiler_params=_cp("arbitrary"),
        name="gmlp",
    )(h, vec(g), w_in, vec(b_in), vec(ln_g), vec(ln_b), ws_all, bs_all, w_out)


def _pad_rows8(x3):
    n, r, c = x3.shape
    return jnp.concatenate([jnp.zeros((n, SUBLANES - r, c), x3.dtype), x3], axis=1)


def kernel(x_prompt, x_sample, p_prompt, p_sample, cache_k_a, cache_v_a, state_s_b, state_conv_b, state_h_c, state_conv_c, g_mix, g_ffn, g_ple, g_final, w_in_a, g_q_a, g_k_a, rel_a, w_out_a, w_in_b, conv_b, a_log_b, dt_bias_b, g_o_b, w_out_b, w_in_c, conv_c, conv_bias_c, w_ra_c, b_ra_c, w_ri_c, b_ri_c, lam_c, w_out_c, w_in_d, b_in_d, ln_g_d, ln_b_d, w_s_d, b_s_d, w_out_d, w_gate_ff, w_up_ff, w_down_ff, w_router, w_gate_e, w_up_e, w_down_e, w_ple, w_ple_gate):
    n_b, t, d = x_prompt.shape
    n_s, t_s, _ = x_sample.shape
    depth = p_prompt.shape[0]
    pd = p_prompt.shape[-1]
    m_p, m_s = n_b * t, n_s * t_s
    m = m_p + m_s
    a_heads = d // g_q_a.shape[-1]
    b_heads = a_log_b.shape[-1]
    n_exp = w_router.shape[-1]
    bf = lambda x: x.astype(BF16)

    h, h_tail = x_prompt.reshape(m_p, d), x_sample.reshape(m_s, d)
    pp_all = p_prompt.reshape(depth, m_p, pd)
    ps_all = p_sample.reshape(depth, m_s, pd)

    outs = {name: [] for name in ("k_a_p", "v_a_p", "k_a_s", "v_a_s", "s_b_p", "conv_b_p", "s_b_s", "conv_b_s",
                                  "h_c_p", "conv_c_p", "h_c_s", "conv_c_s", "v_d_s")}

    def tail_rows(x2, n_seq, tt, row0, cols):
        keep = CONV_W - 1
        if n_seq <= 4:
            return jnp.stack([x2[row0 + (b + 1) * tt - keep:row0 + (b + 1) * tt, :cols] for b in range(n_seq)])
        return x2[row0:row0 + n_seq * tt, :cols].reshape(n_seq, tt, cols)[:, tt - keep:]

    y = None
    for li in range(depth):
        kind, c = li % 4, li // 4
        if kind == 0:
            qkv = _proj(h, g_mix[li], bf(w_in_a[c]), 1024, tail=h_tail)
            w_keep = cache_k_a.shape[2]
            o, kn_p, kn_s = _attention(qkv, n_b, t, n_s, t_s,
                                       cache_k_a[c].reshape(n_s, w_keep, d), cache_v_a[c].reshape(n_s, w_keep, d),
                                       rel_a[c], g_q_a[c], g_k_a[c], a_heads)
            hd = d // a_heads
            keep = min(A_PREV_CHUNKS * CHUNK, t)
            v_p = jnp.stack([qkv[(b + 1) * t - keep:(b + 1) * t, 2 * d:] for b in range(n_b)])
            outs["k_a_p"].append(kn_p.reshape(n_b, keep, a_heads, hd))
            outs["v_a_p"].append(v_p.reshape(n_b, keep, a_heads, hd))
            outs["k_a_s"].append(kn_s.reshape(n_s, t_s, a_heads, hd))
            outs["v_a_s"].append(qkv[m_p:, 2 * d:].reshape(n_s, t_s, a_heads, hd))
            h = _outproj_res(h, o, bf(w_out_a[c]), tail=h_tail)
            h_tail = None
        elif kind == 1:
            w_in = w_in_b[c]
            nq = 3 * d
            w_main = bf(w_in[:, :nq + d])
            w_aux = bf(jnp.pad(w_in[:, nq + d:], ((0, 0), (0, LANES - 2 * b_heads))))
            proj, aux = _proj(h, g_mix[li], w_main, 1024, w_aux)
            pvec = jnp.zeros((2, LANES), F32)
            pvec = pvec.at[0, b_heads:2 * b_heads].set(a_log_b[c]).at[1, b_heads:2 * b_heads].set(dt_bias_b[c])
            hd = d // b_heads
            act_p, bg_p = _gdn_act(proj, aux, 0, n_b, t, 256, jnp.zeros((n_b, SUBLANES, nq), F32),
                                   conv_b[c], pvec, b_heads)
            o_p, s_p = _gdn_delta(act_p, bg_p, proj, 3, n_b, t,
                                  jnp.zeros((n_b, b_heads, hd, hd), F32), g_o_b[c], b_heads, out_rows=m)
            act_s, bg_s = _gdn_act(proj, aux, m_p, n_s, t_s, t_s, _pad_rows8(state_conv_b[c]),
                                   conv_b[c], pvec, b_heads)
            tp = -(-t_s // CHUNK) * CHUNK
            padt = lambda x2: jnp.pad(x2.reshape(n_s, t_s, -1), ((0, 0), (0, tp - t_s), (0, 0))).reshape(n_s * tp, -1)
            z_s = padt(proj[m_p:, nq:nq + d])
            o_s, s_s = _gdn_delta(padt(act_s), padt(bg_s), z_s, 0, n_s, tp, state_s_b[c], g_o_b[c], b_heads)
            o_s = o_s.reshape(n_s, tp, d)[:, :t_s].reshape(m_s, d)
            outs["s_b_p"].append(s_p)
            outs["s_b_s"].append(s_s)
            outs["conv_b_p"].append(tail_rows(proj, n_b, t, 0, nq))
            outs["conv_b_s"].append(tail_rows(proj, n_s, t_s, m_p, nq))
            h = _outproj_res(h, lax.dynamic_update_slice(o_p, o_s, (m_p, 0)), bf(w_out_b[c]))
        elif kind == 2:
            cw = lam_c.shape[-1]
            wgate = bf(jnp.concatenate([w_ra_c[c], w_ri_c[c]], axis=-1))
            common = (g_mix[li], bf(w_in_c[c]), conv_c[c], conv_bias_c[c], wgate, b_ra_c[c], b_ri_c[c],
                      lam_c[c], bf(w_out_c[c]))
            zeros8 = jnp.zeros((n_b, SUBLANES, cw), F32)
            h_new, hl_p, xt_p = _rglru(h, None, 0, n_b, t, 256, *common, zeros8, zeros8)
            h_new, hl_s, xt_s = _rglru(h, h_new, m_p, n_s, t_s, t_s, *common,
                                       _pad_rows8(state_conv_c[c]), _pad_rows8(state_h_c[c][:, None, :]))
            h = h_new
            outs["h_c_p"].append(hl_p[:, SUBLANES - 1])
            outs["h_c_s"].append(hl_s[:, SUBLANES - 1])
            outs["conv_c_p"].append(xt_p[:, SUBLANES - (CONV_W - 1):])
            outs["conv_c_s"].append(xt_s[:, SUBLANES - (CONV_W - 1):])
        else:
            h, vn_s = _gmlp(h, m_p, t_s, g_mix[li], bf(w_in_d[c]), b_in_d[c], ln_g_d[c], ln_b_d[c],
                            w_s_d[c], b_s_d[c], bf(w_out_d[c]))
            outs["v_d_s"].append(vn_s.reshape(n_s, t_s, -1))

        j = li // 2
        if li % 2 == 0:
            h = _ffn_dense(h, g_ffn[li], bf(w_gate_ff[j]), bf(w_up_ff[j]), bf(w_down_ff[j]))
        else:
            wr = jnp.pad(w_router[j], ((0, 0), (0, LANES - n_exp)))
            h = _moe(h, g_ffn[li], wr, bf(w_gate_e[j]), bf(w_up_e[j]), bf(w_down_e[j]))
        last = li == depth - 1
        res = _ple(h, pp_all, ps_all, li, g_ple[li], bf(w_ple[li]), bf(w_ple_gate[li]),
                   g_final if last else None)
        if last:
            y = res
        else:
            h = res

    st = {name: jnp.stack(rows) for name, rows in outs.items()}
    return (y[0].reshape(n_b, t, d), y[1].reshape(n_s, t_s, d),
            st["k_a_p"], st["v_a_p"], st["k_a_s"], st["v_a_s"],
            st["s_b_p"], st["conv_b_p"], st["s_b_s"], st["conv_b_s"],
            st["h_c_p"], st["conv_c_p"], st["h_c_s"], st["conv_c_s"],
            st["v_d_s"])
```

```python
import functools

import jax
import jax.numpy as jnp
from jax import lax
from jax.experimental import pallas as pl
from jax.experimental.pallas import tpu as pltpu

F32 = jnp.float32
BF16 = jnp.bfloat16
EPS = 1e-6
CHUNK = 64
CONV_W = 4
A_REL_CLIP = 128
A_PREV_CHUNKS = 8
LRU_C = 8.0
D_CHUNK = 128
TOP_K = 2
LANES = 128
SUBLANES = 8
VMEM_LIMIT = 48 * 1024 * 1024


def _cp(*sem, vmem=VMEM_LIMIT):
    return pltpu.CompilerParams(dimension_semantics=sem, vmem_limit_bytes=vmem)


def _tile_m(m):
    for t in (1024, 512, 256):
        if m % t == 0:
            return t
    raise ValueError(f"token count {m} must be a multiple of 256")


def _mm(a, b):
    return jnp.dot(a.astype(BF16), b.astype(BF16), preferred_element_type=F32)


def _mm_nt(a, b):
    return lax.dot_general(a.astype(BF16), b.astype(BF16), (((1,), (1,)), ((), ())),
                           preferred_element_type=F32)


def _mm_tn(a, b):
    return lax.dot_general(a.astype(BF16), b.astype(BF16), (((0,), (0,)), ((), ())),
                           preferred_element_type=F32)


def _split2(x):
    hi = x.astype(BF16)
    lo = (x - hi.astype(F32)).astype(BF16)
    return hi, lo


def _split3(x):
    hi = x.astype(BF16)
    r = x - hi.astype(F32)
    mid = r.astype(BF16)
    lo = (r - mid.astype(F32)).astype(BF16)
    return hi, mid, lo


def _mm3(a, b):
    ah, al = _split2(a)
    bh, bl = _split2(b)
    d = functools.partial(jnp.dot, preferred_element_type=F32)
    return d(ah, bh) + (d(ah, bl) + d(al, bh))


def _rms(x, g):
    return x * lax.rsqrt(jnp.mean(x * x, axis=-1, keepdims=True) + EPS) * g


def _sigmoid(x):
    return 0.5 * (1.0 + jnp.tanh(0.5 * x))


def _silu(x):
    return x * _sigmoid(x)


def _softplus(x):
    return jnp.maximum(x, 0.0) + jnp.log1p(jnp.exp(-jnp.abs(x)))


def _gelu(x):
    c = 0.7978845608028654
    return x * (0.5 * (1.0 + jnp.tanh(c * (x + 0.044715 * (x * x * x)))))


def _shift_rows(x, prev8, s):
    r = pltpu.roll(x, s, axis=0)
    pr = pltpu.roll(prev8, s, axis=0)
    row8 = lax.broadcasted_iota(jnp.int32, pr.shape, 0)
    top = jnp.where(row8 < s, pr, r[0:SUBLANES])
    return jnp.concatenate([top, r[SUBLANES:]], axis=0)


def _dwconv(x, prev8, w, stage_ref=None):
    n = x.shape[0]
    if stage_ref is not None:
        stage_ref[0:SUBLANES, :] = prev8
        stage_ref[SUBLANES:SUBLANES + n, :] = x
    y = None
    for k in range(CONV_W):
        s = CONV_W - 1 - k
        if s == 0:
            xs = x
        elif stage_ref is not None:
            xs = stage_ref[SUBLANES - s:SUBLANES - s + n, :]
        else:
            xs = _shift_rows(x, prev8, s)
        term = xs * w[k:k + 1]
        y = term if y is None else y + term
    return y


def _two_source_specs(tm, k, n_head_tiles, n_grid_axes):
    if n_grid_axes == 1:
        return [pl.BlockSpec((tm, k), lambda i: (jnp.minimum(i, n_head_tiles - 1), 0)),
                pl.BlockSpec((tm, k), lambda i: (jnp.maximum(i - n_head_tiles, 0), 0))]
    return [pl.BlockSpec((tm, k), lambda i, j: (jnp.minimum(i, n_head_tiles - 1), 0)),
            pl.BlockSpec((tm, k), lambda i, j: (jnp.maximum(i - n_head_tiles, 0), 0))]


def _proj_kernel(*refs, has_aux, n_head_tiles):
    if n_head_tiles is None:
        x_ref, refs = refs[0], refs[1:]
        load_x = lambda: x_ref[...]
    else:
        xa_ref, xb_ref, refs = refs[0], refs[1], refs[2:]
        load_x = lambda: jnp.where(pl.program_id(0) >= n_head_tiles, xb_ref[...], xa_ref[...])
    if has_aux:
        g_ref, w_ref, waux_ref, o_ref, oaux_ref, hn_ref = refs
    else:
        g_ref, w_ref, o_ref, hn_ref = refs

    @pl.when(pl.program_id(1) == 0)
    def _():
        hn = _rms(load_x(), g_ref[...]).astype(BF16)
        hn_ref[...] = hn
        if has_aux:
            oaux_ref[...] = jnp.dot(hn, waux_ref[...], preferred_element_type=F32)

    o_ref[...] = jnp.dot(hn_ref[...], w_ref[...], preferred_element_type=F32)


def _proj(h, g, w, tn, waux=None, tail=None):
    m, k = h.shape
    if tail is not None:
        m += tail.shape[0]
    n = w.shape[1]
    tm = _tile_m(m)
    nht = None
    if tail is None:
        in_specs = [pl.BlockSpec((tm, k), lambda i, j: (i, 0))]
        args = [h]
    else:
        assert h.shape[0] % tm == 0 and tail.shape[0] % tm == 0
        nht = h.shape[0] // tm
        in_specs = _two_source_specs(tm, k, nht, 2)
        args = [h, tail]
    in_specs += [pl.BlockSpec((1, k), lambda i, j: (0, 0)),
                 pl.BlockSpec((k, tn), lambda i, j: (0, j))]
    args += [g.reshape(1, k), w]
    out_shape = [jax.ShapeDtypeStruct((m, n), F32)]
    out_specs = [pl.BlockSpec((tm, tn), lambda i, j: (i, j))]
    if waux is not None:
        na = waux.shape[1]
        in_specs.append(pl.BlockSpec((k, na), lambda i, j: (0, 0)))
        out_shape.append(jax.ShapeDtypeStruct((m, na), F32))
        out_specs.append(pl.BlockSpec((tm, na), lambda i, j: (i, 0)))
        args.append(waux)
    res = pl.pallas_call(
        functools.partial(_proj_kernel, has_aux=waux is not None, n_head_tiles=nht),
        grid=(m // tm, n // tn),
        in_specs=in_specs, out_specs=out_specs, out_shape=out_shape,
        scratch_shapes=[pltpu.VMEM((tm, k), BF16)],
        compiler_params=_cp("parallel", "arbitrary"),
        name="proj",
    )(*args)
    return res if waux is not None else res[0]


def _outproj_kernel(*refs, n_head_tiles):
    if n_head_tiles is None:
        h_ref, o_ref, w_ref, out_ref = refs
        h = h_ref[...]
    else:
        ha_ref, hb_ref, o_ref, w_ref, out_ref = refs
        h = jnp.where(pl.program_id(0) >= n_head_tiles, hb_ref[...], ha_ref[...])
    out_ref[...] = h + jnp.dot(o_ref[...].astype(BF16), w_ref[...], preferred_element_type=F32)


def _outproj_res(h, o, w, tail=None):
    m, k = o.shape
    d = h.shape[1]
    tm = _tile_m(m)
    nht = None
    if tail is None:
        in_specs = [pl.BlockSpec((tm, d), lambda i: (i, 0))]
        args = [h]
    else:
        assert h.shape[0] % tm == 0 and tail.shape[0] % tm == 0
        nht = h.shape[0] // tm
        in_specs = _two_source_specs(tm, d, nht, 1)
        args = [h, tail]
    return pl.pallas_call(
        functools.partial(_outproj_kernel, n_head_tiles=nht),
        grid=(m // tm,),
        in_specs=in_specs + [pl.BlockSpec((tm, k), lambda i: (i, 0)),
                             pl.BlockSpec((k, d), lambda i: (0, 0))],
        out_specs=pl.BlockSpec((tm, d), lambda i: (i, 0)),
        out_shape=jax.ShapeDtypeStruct((m, d), F32),
        compiler_params=_cp("parallel"),
        name="outproj_res",
    )(*args, o, w)


def _ffn_kernel(x_ref, g_ref, wg_ref, wu_ref, wd_ref, o_ref, hn_ref):
    @pl.when(pl.program_id(1) == 0)
    def _():
        x = x_ref[...]
        hn_ref[...] = _rms(x, g_ref[...]).astype(BF16)
        o_ref[...] = x

    hn = hn_ref[...]
    gg = jnp.dot(hn, wg_ref[...], preferred_element_type=F32)
    uu = jnp.dot(hn, wu_ref[...], preferred_element_type=F32)
    hid = (_silu(gg) * uu).astype(BF16)
    o_ref[...] += jnp.dot(hid, wd_ref[...], preferred_element_type=F32)


def _ffn_dense(h, g, wg, wu, wd):
    m, d = h.shape
    f = wg.shape[1]
    tm = 512
    tf = f // 2 if (f // 2) % LANES == 0 else 256
    return pl.pallas_call(
        _ffn_kernel,
        grid=(m // tm, f // tf),
        in_specs=[pl.BlockSpec((tm, d), lambda i, j: (i, 0)),
                  pl.BlockSpec((1, d), lambda i, j: (0, 0)),
                  pl.BlockSpec((d, tf), lambda i, j: (0, j)),
                  pl.BlockSpec((d, tf), lambda i, j: (0, j)),
                  pl.BlockSpec((tf, d), lambda i, j: (j, 0))],
        out_specs=pl.BlockSpec((tm, d), lambda i, j: (i, 0)),
        out_shape=jax.ShapeDtypeStruct((m, d), F32),
        scratch_shapes=[pltpu.VMEM((tm, d), BF16)],
        compiler_params=_cp("parallel", "arbitrary"),
        name="ffn_dense",
    )(h, g.reshape(1, d), wg, wu, wd)


def _moe_kernel(x_ref, g_ref, wr_ref, tril_ref, wg_ref, wu_ref, wd_ref, o_ref, hn_ref, gates_ref, rank_ref, rankt_ref,
                *, n_experts, pass_rows):
    e = pl.program_id(1)
    n_sub, _, sub = rankt_ref.shape

    @pl.when(e == 0)
    def _():
        tril = tril_ref[...]
        lane = lax.broadcasted_iota(jnp.int32, (sub, LANES), 1)
        for s in range(n_sub):
            rs = slice(s * sub, (s + 1) * sub)
            x = x_ref[rs, :]
            hn = _rms(x, g_ref[...])
            hn_hi, hn_lo = _split2(hn)
            hn_ref[rs, :] = hn_hi
            o_ref[rs, :] = x
            both = jnp.dot(hn_hi, wr_ref[...], preferred_element_type=F32)
            logits = both[:, :LANES] + (both[:, LANES:] + jnp.dot(hn_lo, wr_ref[:, :LANES],
                                                                  preferred_element_type=F32))
            lg = jnp.where(lane < n_experts, logits, -jnp.inf)
            m1 = jnp.max(lg, axis=-1, keepdims=True)
            i1 = jnp.min(jnp.where(lg == m1, lane, LANES), axis=-1, keepdims=True)
            lg2 = jnp.where(lane == i1, -jnp.inf, lg)
            m2 = jnp.max(lg2, axis=-1, keepdims=True)
            i2 = jnp.min(jnp.where(lg2 == m2, lane, LANES), axis=-1, keepdims=True)
            e2 = jnp.exp(m2 - m1)
            w1 = 1.0 / (1.0 + e2)
            w2 = e2 / (1.0 + e2)
            gates_ref[rs, :] = jnp.where(lane == i1, w1, 0.0) + jnp.where(lane == i2, w2, 0.0)
            sel = jnp.where(lane == i1, 1.0, jnp.where(lane == i2, 1.0, 0.0))
            count = jnp.dot(tril, sel.astype(BF16), preferred_element_type=F32)
            rank = jnp.where(sel > 0.0, count - 1.0, -1.0)
            rank_ref[rs, :] = rank
            rankt_ref[s] = rank.T

    lane = lax.broadcasted_iota(jnp.int32, (sub, LANES), 1)
    for s in range(n_sub):
        rs = slice(s * sub, (s + 1) * sub)
        rank_row = rankt_ref[s, pl.ds(e, 1), :]
        n_routed = (jnp.max(rank_row) + 1.0).astype(jnp.int32)
        rank_col = jnp.sum(jnp.where(lane == e, rank_ref[rs, :], 0.0), axis=-1, keepdims=True)
        gate_col = jnp.sum(jnp.where(lane == e, gates_ref[rs, :], 0.0), axis=-1, keepdims=True)

        def one_pass(cap, base, rs=rs, rank_row=rank_row, rank_col=rank_col, gate_col=gate_col):
            row_id = lax.broadcasted_iota(jnp.int32, (cap, sub), 0).astype(F32) + base
            col_id = lax.broadcasted_iota(jnp.int32, (sub, cap), 1).astype(F32) + base
            pick = jnp.where(rank_row == row_id, 1.0, 0.0).astype(BF16)
            xe = jnp.dot(pick, hn_ref[rs, :], preferred_element_type=F32).astype(BF16)
            gg = jnp.dot(xe, wg_ref[0], preferred_element_type=F32)
            uu = jnp.dot(xe, wu_ref[0], preferred_element_type=F32)
            hid = (_silu(gg) * uu).astype(BF16)
            ye = jnp.dot(hid, wd_ref[0], preferred_element_type=F32).astype(BF16)
            put = jnp.where(rank_col == col_id, gate_col, 0.0).astype(BF16)
            o_ref[rs, :] += jnp.dot(put, ye, preferred_element_type=F32)

        *single, full = pass_rows
        below = 0
        for rows_ in single:
            @pl.when((n_routed > below) & (n_routed <= rows_))
            def _(one_pass=one_pass, rows_=rows_):
                one_pass(rows_, 0.0)
            below = rows_

        @pl.when(n_routed > below)
        def _(one_pass=one_pass, n_routed=n_routed):
            def body(cc, carry):
                one_pass(full, (cc * full).astype(F32))
                return carry
            lax.fori_loop(0, (n_routed + full - 1) // full, body, 0)


def _moe(h, g, wr, wg, wu, wd):
    m, d = h.shape
    ne, _, ef = wg.shape
    sub = 512
    tm = 2 * sub if m % (2 * sub) == 0 else sub
    mean_rows = sub * TOP_K // ne
    pass_rows = tuple(mean_rows * k // 4 for k in (4, 5, 6, 7, 8))
    assert m % tm == 0
    return pl.pallas_call(
        functools.partial(_moe_kernel, n_experts=ne, pass_rows=pass_rows),
        grid=(m // tm, ne),
        in_specs=[pl.BlockSpec((tm, d), lambda i, e: (i, 0)),
                  pl.BlockSpec((1, d), lambda i, e: (0, 0)),
                  pl.BlockSpec((d, 2 * LANES), lambda i, e: (0, 0)),
                  pl.BlockSpec((sub, sub), lambda i, e: (0, 0)),
                  pl.BlockSpec((1, d, ef), lambda i, e: (e, 0, 0)),
                  pl.BlockSpec((1, d, ef), lambda i, e: (e, 0, 0)),
                  pl.BlockSpec((1, ef, d), lambda i, e: (e, 0, 0))],
        out_specs=pl.BlockSpec((tm, d), lambda i, e: (i, 0)),
        out_shape=jax.ShapeDtypeStruct((m, d), F32),
        scratch_shapes=[pltpu.VMEM((tm, d), BF16), pltpu.VMEM((tm, LANES), F32),
                        pltpu.VMEM((tm, LANES), F32), pltpu.VMEM((tm // sub, LANES, sub), F32)],
        compiler_params=_cp("parallel", "arbitrary"),
        name="moe",
    )(h, g.reshape(1, d), jnp.concatenate(_split2(wr), axis=1), jnp.tril(jnp.ones((sub, sub), BF16)), wg, wu, wd)


def _ple_kernel(*refs, final, n_prompt_tiles):
    if final:
        x_ref, pp_ref, ps_ref, g_ref, wp_ref, wg_ref, gf_ref, yp_ref, ys_ref = refs
    else:
        x_ref, pp_ref, ps_ref, g_ref, wp_ref, wg_ref, o_ref = refs
    i = pl.program_id(0)
    is_sample = i >= n_prompt_tiles
    x = x_ref[...]
    hn = _rms(x, g_ref[...]).astype(BF16)
    gate = _sigmoid(jnp.dot(hn, wg_ref[...], preferred_element_type=F32))
    p = jnp.where(is_sample, ps_ref[...], pp_ref[...])
    pp = jnp.dot(p.astype(BF16), wp_ref[...], preferred_element_type=F32)
    out = x + pp * gate
    if final:
        y = _rms(out, gf_ref[...])

        @pl.when(jnp.logical_not(is_sample))
        def _():
            yp_ref[...] = y

        @pl.when(is_sample)
        def _():
            ys_ref[...] = y
    else:
        o_ref[...] = out


def _ple(h, p_prompt, p_sample, layer, g, wp, wgate, g_final=None):
    m, d = h.shape
    _, m_p, pd = p_prompt.shape
    m_s = p_sample.shape[1]
    tm = _tile_m(m)
    assert m_p % tm == 0 and m_s % tm == 0
    npt, nst = m_p // tm, m_s // tm
    final = g_final is not None
    in_specs = [pl.BlockSpec((tm, d), lambda i: (i, 0)),
                pl.BlockSpec((None, tm, pd), lambda i: (layer, jnp.minimum(i, npt - 1), 0)),
                pl.BlockSpec((None, tm, pd), lambda i: (layer, jnp.maximum(i - npt, 0), 0)),
                pl.BlockSpec((1, d), lambda i: (0, 0)),
                pl.BlockSpec((pd, d), lambda i: (0, 0)),
                pl.BlockSpec((d, d), lambda i: (0, 0))]
    args = [h, p_prompt, p_sample, g.reshape(1, d), wp, wgate]
    if final:
        in_specs.append(pl.BlockSpec((1, d), lambda i: (0, 0)))
        args.append(g_final.reshape(1, d))
        out_specs = [pl.BlockSpec((tm, d), lambda i: (jnp.minimum(i, npt - 1), 0)),
                     pl.BlockSpec((tm, d), lambda i: (jnp.maximum(i - npt, 0), 0))]
        out_shape = [jax.ShapeDtypeStruct((m_p, d), F32), jax.ShapeDtypeStruct((m_s, d), F32)]
    else:
        out_specs = pl.BlockSpec((tm, d), lambda i: (i, 0))
        out_shape = jax.ShapeDtypeStruct((m, d), F32)
    return pl.pallas_call(
        functools.partial(_ple_kernel, final=final, n_prompt_tiles=npt),
        grid=(npt + nst,),
        in_specs=in_specs, out_specs=out_specs, out_shape=out_shape,
        compiler_params=_cp("arbitrary"),
        name="ple",
    )(*args)


def _pair_rms(x, g, hd):
    lane = lax.broadcasted_iota(jnp.int32, x.shape, 1)
    lo = lane < hd
    x2 = x * x
    s0 = jnp.sum(jnp.where(lo, x2, 0.0), axis=-1, keepdims=True)
    s1 = jnp.sum(jnp.where(lo, 0.0, x2), axis=-1, keepdims=True)
    ms = jnp.where(lo, s0, s1) * (1.0 / hd)
    return x * lax.rsqrt(ms + EPS) * g


def _stack_heads(q, hd):
    lane = lax.broadcasted_iota(jnp.int32, q.shape, 1)
    lo = lane < hd
    return jnp.concatenate([jnp.where(lo, q, 0.0), jnp.where(lo, 0.0, q)], axis=0).astype(BF16)


def _unstack_heads(o2, hd):
    n = o2.shape[0] // 2
    lane = lax.broadcasted_iota(jnp.int32, (n, o2.shape[1]), 1)
    return jnp.where(lane < hd, o2[:n], o2[n:])


def _attend(q2, kw, vw, bias, valid):
    s = _mm_nt(q2, kw) + bias
    if valid is not None:
        s = jnp.where(valid, s, -jnp.inf)
    mx = jnp.max(s, axis=-1, keepdims=True)
    e = jnp.exp(s - mx)
    den = jnp.sum(e, axis=-1, keepdims=True)
    return jnp.dot(e.astype(BF16), vw, preferred_element_type=F32) / den


def _attn_prompt_kernel(q_ref, k_ref, v_ref, b_ref, gq_ref, gk_ref, o_ref, kn_ref, kwin_ref, vwin_ref,
                        *, hd, scale, reach, chunk_group):
    i = pl.program_id(2)
    tq, width = q_ref.shape
    band = reach + CHUNK
    pw = 2 * hd
    lanes = [slice(p * pw, (p + 1) * pw) for p in range(width // pw)]
    qn = [_pair_rms(q_ref[:, c], gq_ref[...], hd) * scale for c in lanes]
    knc = [_pair_rms(k_ref[:, c], gk_ref[...], hd) for c in lanes]

    @pl.when(i == pl.num_programs(2) - 1)
    def _():
        for c, x in zip(lanes, knc):
            kn_ref[:, c] = x

    @pl.when(i == 0)
    def _():
        kwin_ref[0:tq, :] = jnp.zeros((tq, width), BF16)
        vwin_ref[0:tq, :] = jnp.zeros((tq, width), BF16)

    @pl.when(i > 0)
    def _():
        kwin_ref[0:tq, :] = kwin_ref[tq:2 * tq, :]
        vwin_ref[0:tq, :] = vwin_ref[tq:2 * tq, :]

    for c, x in zip(lanes, knc):
        kwin_ref[tq:2 * tq, c] = x.astype(BF16)
    vwin_ref[tq:2 * tq, :] = v_ref[...].astype(BF16)
    col = lax.broadcasted_iota(jnp.int32, (1, band), 1)
    n_chunks = tq // CHUNK
    w0s = [tq - reach + jj * CHUNK for jj in range(n_chunks)]

    def attend(first_block, items):
        q2 = [_stack_heads(qn[p][jj * CHUNK:(jj + 1) * CHUNK], hd) for p, jj in items]
        s = [_mm_nt(x, kwin_ref[w0s[jj]:w0s[jj] + band, lanes[p]]) + b_ref[p] for x, (p, jj) in zip(q2, items)]
        if first_block:
            s = [jnp.where(col + w0s[jj] >= tq, x, -jnp.inf) for x, (p, jj) in zip(s, items)]
        mx = [jnp.max(x, axis=-1, keepdims=True) for x in s]
        e = [jnp.exp(x - m) for x, m in zip(s, mx)]
        den = [jnp.sum(x, axis=-1, keepdims=True) for x in e]
        o2 = [jnp.dot(x.astype(BF16), vwin_ref[w0s[jj]:w0s[jj] + band, lanes[p]], preferred_element_type=F32)
              for x, (p, jj) in zip(e, items)]
        for x, dn, (p, jj) in zip(o2, den, items):
            o_ref[jj * CHUNK:(jj + 1) * CHUNK, lanes[p]] = _unstack_heads(x / dn, hd)

    def attend_all(first_block):
        items = [(p, jj) for jj in range(n_chunks) for p in range(len(lanes))]
        for j0 in range(0, len(items), chunk_group):
            attend(first_block, items[j0:j0 + chunk_group])

    @pl.when(i == 0)
    def _():
        attend_all(True)

    @pl.when(i > 0)
    def _():
        attend_all(False)


def _attn_sample_kernel(q_ref, k_ref, v_ref, ck_ref, cv_ref, b_ref, gq_ref, gk_ref, oin_ref, o_ref, kn_ref,
                        *, hd, scale):
    del oin_ref
    pw = 2 * hd
    pairs = [slice(p * pw, (p + 1) * pw) for p in range(q_ref.shape[1] // pw)]
    qn = [_pair_rms(q_ref[:, c], gq_ref[...], hd) * scale for c in pairs]
    kn = [_pair_rms(k_ref[:, c], gk_ref[...], hd) for c in pairs]
    for c, x in zip(pairs, kn):
        kn_ref[:, c] = x
    kcat = [jnp.concatenate([ck_ref[:, c], x], axis=0).astype(BF16) for c, x in zip(pairs, kn)]
    vcat = [jnp.concatenate([cv_ref[:, c], v_ref[:, c]], axis=0).astype(BF16) for c in pairs]
    s = [_mm_nt(_stack_heads(x, hd), kk) + b_ref[p] for p, (x, kk) in enumerate(zip(qn, kcat))]
    mx = [jnp.max(x, axis=-1, keepdims=True) for x in s]
    e = [jnp.exp(x - m) for x, m in zip(s, mx)]
    den = [jnp.sum(x, axis=-1, keepdims=True) for x in e]
    o2 = [jnp.dot(x.astype(BF16), vv, preferred_element_type=F32) for x, vv in zip(e, vcat)]
    for c, x, dn in zip(pairs, o2, den):
        o_ref[:, c] = _unstack_heads(x / dn, hd)


def _rel_bias_pairs(table, q0, nq, nk):
    nh = table.shape[0]
    span = nq + nk - 1
    dmin, dmax = q0 - (nk - 1), q0 + nq - 1
    lo, hi = max(dmin, -A_REL_CLIP), min(dmax, A_REL_CLIP)
    line = jnp.concatenate([jnp.broadcast_to(table[:, :1], (nh, lo - dmin)),
                            table[:, lo + A_REL_CLIP:hi + A_REL_CLIP + 1],
                            jnp.broadcast_to(table[:, -1:], (nh, dmax - hi))], axis=1)
    flipped = line[:, ::-1]
    period = jnp.concatenate([flipped, jnp.zeros((nh, 1), table.dtype)], axis=1)
    shifted = jnp.tile(period, (1, nq))[:, :nq * span].reshape(nh, nq, span)
    b = shifted[:, :, nq - 1:nq - 1 + nk]
    return b.reshape(nh // 2, 2 * nq, nk)


def _attention(qkv, n_b, t, n_s, t_s, cache_k, cache_v, rel, g_q, g_k, n_heads):
    m_tot, d3 = qkv.shape
    d = d3 // 3
    hd = d // n_heads
    pw = 2 * hd
    n_pairs = d // pw
    m_p = n_b * t
    reach = A_PREV_CHUNKS * CHUNK
    tq = reach
    assert pw == LANES and t % tq == 0 and t_s % 16 == 0
    nqb = t // tq
    scale = hd ** -0.5
    gq2 = jnp.tile(g_q, 2).reshape(1, pw)
    gk2 = jnp.tile(g_k, 2).reshape(1, pw)

    bias_p = _rel_bias_pairs(rel, reach, CHUNK, reach + CHUNK)
    kcol, vcol = n_pairs, 2 * n_pairs

    def rows(b, p, i):
        return b * nqb + i

    ppb = 2 if n_pairs % 2 == 0 else 1
    npb = n_pairs // ppb
    blk = (tq, ppb * pw)
    o_all, kn_p = pl.pallas_call(
        functools.partial(_attn_prompt_kernel, hd=hd, scale=scale, reach=reach, chunk_group=4),
        grid=(n_b, npb, nqb),
        in_specs=[pl.BlockSpec(blk, lambda b, p, i: (rows(b, p, i), p)),
                  pl.BlockSpec(blk, lambda b, p, i: (rows(b, p, i), npb + p)),
                  pl.BlockSpec(blk, lambda b, p, i: (rows(b, p, i), 2 * npb + p)),
                  pl.BlockSpec((ppb, 2 * CHUNK, reach + CHUNK), lambda b, p, i: (p, 0, 0)),
                  pl.BlockSpec((1, pw), lambda b, p, i: (0, 0)),
                  pl.BlockSpec((1, pw), lambda b, p, i: (0, 0))],
        out_specs=[pl.BlockSpec(blk, lambda b, p, i: (rows(b, p, i), p)),
                   pl.BlockSpec(blk, lambda b, p, i: (b, p))],
        out_shape=[jax.ShapeDtypeStruct((m_tot, d), F32),
                   jax.ShapeDtypeStruct((n_b * tq, d), F32)],
        scratch_shapes=[pltpu.VMEM((2 * tq, ppb * pw), BF16), pltpu.VMEM((2 * tq, ppb * pw), BF16)],
        compiler_params=_cp("parallel", "parallel", "arbitrary"),
        name="attn_prompt",
    )(qkv, qkv, qkv, bias_p, gq2, gk2)

    w = cache_k.shape[1]
    bias_s = _rel_bias_pairs(rel, w, t_s, w + t_s)
    srow = m_p // t_s
    sblk = (t_s, d)
    o_all, kn_s = pl.pallas_call(
        functools.partial(_attn_sample_kernel, hd=hd, scale=scale),
        grid=(n_s,),
        in_specs=[pl.BlockSpec(sblk, lambda b: (srow + b, 0)),
                  pl.BlockSpec(sblk, lambda b: (srow + b, 1)),
                  pl.BlockSpec(sblk, lambda b: (srow + b, 2)),
                  pl.BlockSpec((None, w, d), lambda b: (b, 0, 0)),
                  pl.BlockSpec((None, w, d), lambda b: (b, 0, 0)),
                  pl.BlockSpec((n_pairs, 2 * t_s, w + t_s), lambda b: (0, 0, 0)),
                  pl.BlockSpec((1, pw), lambda b: (0, 0)),
                  pl.BlockSpec((1, pw), lambda b: (0, 0)),
                  pl.BlockSpec(memory_space=pl.ANY)],
        out_specs=[pl.BlockSpec(sblk, lambda b: (srow + b, 0)),
                   pl.BlockSpec(sblk, lambda b: (b, 0))],
        out_shape=[jax.ShapeDtypeStruct((m_tot, d), F32),
                   jax.ShapeDtypeStruct((n_s * t_s, d), F32)],
        input_output_aliases={8: 0},
        compiler_params=_cp("parallel"),
        name="attn_sample",
    )(qkv, qkv, qkv, cache_k, cache_v, bias_s, gq2, gk2, o_all)
    return o_all, kn_p, kn_s


def _gdn_act_kernel(xq_ref, xk_ref, xv_ref, pq_ref, pk_ref, pv_ref, hq_ref, hk_ref, hv_ref,
                    aux_ref, cw_ref, pvec_ref, o_ref, bg_ref, stage_ref, *, blocks_per_seq, n_heads, scale):
    i = pl.program_id(0)
    start = (i % blocks_per_seq) == 0
    d = xq_ref.shape[1]
    hd = d // n_heads
    parts = ((xq_ref, pq_ref, hq_ref), (xk_ref, pk_ref, hk_ref), (xv_ref, pv_ref, hv_ref))
    for c, (x_ref, p_ref, h_ref) in enumerate(parts):
        prev8 = jnp.where(start, h_ref[...], p_ref[...])
        a = _silu(_dwconv(x_ref[...], prev8, cw_ref[:, c * d:(c + 1) * d], stage_ref))
        for hh in range(n_heads):
            ah = a[:, hh * hd:(hh + 1) * hd]
            if c < 2:
                ah = ah * lax.rsqrt(jnp.sum(ah * ah, axis=-1, keepdims=True) + EPS)
            if c == 0:
                ah = ah * scale
            o_ref[:, c * d + hh * hd:c * d + (hh + 1) * hd] = ah
    ax = aux_ref[...]
    lane = lax.broadcasted_iota(jnp.int32, ax.shape, 1)
    beta = _sigmoid(ax)
    gdec = -jnp.exp(pvec_ref[0:1, :]) * _softplus(ax + pvec_ref[1:2, :])
    bg_ref[...] = jnp.where(lane < n_heads, beta, jnp.where(lane < 2 * n_heads, gdec, 0.0))


def _gdn_act(proj, aux, row0, n_seq, t, tb, hist8, conv_w, pvec, n_heads):
    d = proj.shape[1] // 4
    hd = d // n_heads
    bps = t // tb
    r0 = row0 // tb
    p8 = tb // SUBLANES

    def xspec(c):
        return pl.BlockSpec((tb, d), lambda i, c=c: (r0 + i, c))

    def pspec(c):
        return pl.BlockSpec((SUBLANES, d), lambda i, c=c: (jnp.maximum((r0 + i) * p8 - 1, 0), c))

    def hspec(c):
        return pl.BlockSpec((None, SUBLANES, d), lambda i, c=c: (i // bps, 0, c))

    m = n_seq * t
    return pl.pallas_call(
        functools.partial(_gdn_act_kernel, blocks_per_seq=bps, n_heads=n_heads, scale=hd ** -0.5),
        grid=(m // tb,),
        in_specs=[xspec(0), xspec(1), xspec(2), pspec(0), pspec(1), pspec(2), hspec(0), hspec(1), hspec(2),
                  pl.BlockSpec((tb, LANES), lambda i: (r0 + i, 0)),
                  pl.BlockSpec((CONV_W, 3 * d), lambda i: (0, 0)),
                  pl.BlockSpec((2, LANES), lambda i: (0, 0))],
        out_specs=[pl.BlockSpec((tb, 3 * d), lambda i: (i, 0)),
                   pl.BlockSpec((tb, LANES), lambda i: (i, 0))],
        out_shape=[jax.ShapeDtypeStruct((m, 3 * d), F32), jax.ShapeDtypeStruct((m, LANES), F32)],
        scratch_shapes=[pltpu.VMEM((SUBLANES + tb, d), F32)],
        compiler_params=_cp("parallel"),
        name="gdn_act",
    )(proj, proj, proj, proj, proj, proj, hist8, hist8, hist8, aux, conv_w, pvec)


def _seg_cumsum(x, rowmod, reverse):
    n = x.shape[0]
    d = 1
    while d < CHUNK:
        if reverse:
            sh = pltpu.roll(x, n - d, axis=0)
            x = x + jnp.where(rowmod < CHUNK - d, sh, 0.0)
        else:
            sh = pltpu.roll(x, d, axis=0)
            x = x + jnp.where(rowmod >= d, sh, 0.0)
        d *= 2
    return x


def _gdn_delta_kernel(act_ref, z_ref, bg_ref, s0_ref, go_ref, o_ref, sout_ref, s_ref, *, n_heads, chained, head_group):
    i = pl.program_id(1)
    tb, d = z_ref.shape
    hd = d // n_heads
    nch = tb // CHUNK
    dot = functools.partial(jnp.dot, preferred_element_type=F32)

    if chained:
        @pl.when(i == 0)
        def _():
            s_ref[...] = s0_ref[...]

    bg = bg_ref[...]
    lane = lax.broadcasted_iota(jnp.int32, (tb, hd), 1)
    rowmod = lax.broadcasted_iota(jnp.int32, (tb, hd), 0) & (CHUNK - 1)
    ri = lax.broadcasted_iota(jnp.int32, (tb, tb), 0)
    ci = lax.broadcasted_iota(jnp.int32, (tb, tb), 1)
    same = (ri >> 6) == (ci >> 6)
    low_incl = same & (ri >= ci)
    strict = same & (ri > ci)
    eye_c = (lax.broadcasted_iota(jnp.int32, (CHUNK, tb), 0)
             == (lax.broadcasted_iota(jnp.int32, (CHUNK, tb), 1) & (CHUNK - 1))).astype(F32)
    def compact(x):
        acc = x[0:CHUNK]
        for c in range(1, nch):
            acc = acc + x[c * CHUNK:(c + 1) * CHUNK]
        return acc

    def expand(xc):
        return jnp.where(same, jnp.concatenate([xc] * nch, axis=0), jnp.zeros((), xc.dtype))

    def each(fn, *lists):
        return [fn(*a) for a in zip(*lists)]

    cum_fwd = _seg_cumsum(bg, rowmod, reverse=False)
    cum_rev = _seg_cumsum(bg, rowmod, reverse=True) - bg

    def column(arr, idx):
        return jnp.broadcast_to(jnp.sum(jnp.where(lane == idx, arr, 0.0), axis=-1, keepdims=True), (tb, hd))

    def gc_diff(gc):
        gh, gm, gl = (piece.astype(F32) for piece in _split3(gc))
        lhs = jnp.where(lane == 0, gh, jnp.where(lane == 1, gm, jnp.where(lane == 2, gl,
                                                                          jnp.where(lane < 6, 1.0, 0.0))))
        rhs_t = jnp.where(lane < 3, 1.0, jnp.where(lane == 3, -gh, jnp.where(lane == 4, -gm,
                                                                             jnp.where(lane == 5, -gl, 0.0))))
        return _mm_nt(lhs, rhs_t)

    for h0 in range(0, n_heads, head_group):
        hs = list(range(h0, h0 + head_group))
        q = [act_ref[:, h * hd:(h + 1) * hd] for h in hs]
        k = [act_ref[:, d + h * hd:d + (h + 1) * hd] for h in hs]
        v = [act_ref[:, 2 * d + h * hd:2 * d + (h + 1) * hd] for h in hs]
        beta = [column(bg, h) for h in hs]
        gc = [column(cum_fwd, n_heads + h) for h in hs]
        grev = [column(cum_rev, n_heads + h) for h in hs]
        decay = each(lambda x: jnp.exp(jnp.where(low_incl, gc_diff(x), -jnp.inf)), gc)
        kb = each(lambda a, b: a * b, k, beta)
        lmat = each(lambda a, b, dc: jnp.where(strict, _mm_nt(a, b) * dc, 0.0), kb, k, decay)
        egc = each(jnp.exp, gc)
        rhs = each(lambda a, b, c_, e: jnp.concatenate([a * b, c_ * e], axis=1), v, beta, kb, egc)

        xc = each(compact, lmat)
        tinv = each(lambda x: eye_c - x, xc)
        xh = each(lambda x: x.astype(BF16), xc)
        xe = each(expand, xh)
        p = 1
        while 2 * p < CHUNK:
            xh = each(lambda a, b: dot(a, b).astype(BF16), xh, xe)
            xe = each(expand, xh)
            tinv = each(lambda t, b: t + dot(t.astype(BF16), b), tinv, xe)
            p *= 2
        sol = each(lambda t, r: dot(expand(t.astype(BF16)), r.astype(BF16)), tinv, rhs)

        qk = each(lambda a, b, dc: jnp.where(low_incl, _mm_nt(a, b) * dc, 0.0), q, k, decay)
        q_dec = each(lambda a, e: a * e, q, egc)
        k_dec = each(lambda a, r: a * jnp.exp(r), k, grev)
        g_last = each(lambda a, r: jnp.exp(a + r), gc, grev)

        v_news = [[] for _ in hs]
        o_inter = [[] for _ in hs]
        s = [s_ref[h] for h in hs] if chained else None
        for c in range(nch):
            rc = slice(c * CHUNK, (c + 1) * CHUNK)
            if not chained:
                s = [s0_ref[c, h] for h in hs]
            v_new = each(lambda so, st: so[rc, :hd] - _mm(so[rc, hd:], st), sol, s)
            o_c = each(lambda qd, st: _mm(qd[rc], st), q_dec, s)
            s = each(lambda st, gl, kd, vn: st * gl[c * CHUNK:c * CHUNK + 1, :] + _mm_tn(kd[rc], vn),
                     s, g_last, k_dec, v_new)
            for j, h in enumerate(hs):
                v_news[j].append(v_new[j])
                o_inter[j].append(o_c[j])
                if not chained:
                    sout_ref[c, h] = s[j]
        if chained:
            for j, h in enumerate(hs):
                s_ref[h] = s[j]

            @pl.when(i == pl.num_programs(1) - 1)
            def _(s=s, hs=hs):
                for j, h in enumerate(hs):
                    sout_ref[h] = s[j]

        o = each(lambda oi, m, vn: jnp.concatenate(oi, axis=0) + _mm(m, jnp.concatenate(vn, axis=0)),
                 o_inter, qk, v_news)
        for j, h in enumerate(hs):
            o_ref[:, h * hd:(h + 1) * hd] = _rms(o[j], go_ref[...]) * _silu(z_ref[:, h * hd:(h + 1) * hd])


def _gdn_delta(act, bg, zsrc, zcol, n_seq, t, s0, g_o, n_heads, out_rows=None):
    d = act.shape[1] // 3
    hd = d // n_heads
    tb = 4 * CHUNK
    chained = t > CHUNK
    if chained:
        assert t % tb == 0
        grid = (n_seq, t // tb)
        rows = lambda b, i: b * (t // tb) + i
        sspec = pl.BlockSpec((None, n_heads, hd, hd), lambda b, i: (b, 0, 0, 0))
    else:
        assert t == CHUNK and n_seq % (tb // CHUNK) == 0
        grid = (n_seq * t // tb, 1)
        rows = lambda b, i: b
        sspec = pl.BlockSpec((tb // CHUNK, n_heads, hd, hd), lambda b, i: (b, 0, 0, 0))
    return pl.pallas_call(
        functools.partial(_gdn_delta_kernel, n_heads=n_heads, chained=chained, head_group=8),
        grid=grid,
        in_specs=[pl.BlockSpec((tb, 3 * d), lambda b, i: (rows(b, i), 0)),
                  pl.BlockSpec((tb, d), lambda b, i: (rows(b, i), zcol)),
                  pl.BlockSpec((tb, LANES), lambda b, i: (rows(b, i), 0)),
                  sspec,
                  pl.BlockSpec((1, hd), lambda b, i: (0, 0))],
        out_specs=[pl.BlockSpec((tb, d), lambda b, i: (rows(b, i), 0)), sspec],
        out_shape=[jax.ShapeDtypeStruct((out_rows or n_seq * t, d), F32),
                   jax.ShapeDtypeStruct((n_seq, n_heads, hd, hd), F32)],
        scratch_shapes=[pltpu.VMEM((n_heads, hd, hd), F32)],
        compiler_params=_cp("parallel", "arbitrary"),
        name="gdn_delta",
    )(act, zsrc, bg, s0, g_o.reshape(1, hd))


def _rglru_kernel(h_ref, g_ref, win_ref, cw_ref, cb_ref, wgate_ref, bra_ref, bri_ref, lam_ref, wout_ref,
                  hist_ref, h0_ref, oin_ref, o_ref, hlast_ref, xtail_ref, cx_ref, ch_ref, *, n_blocks):
    del oin_ref
    i = pl.program_id(1)
    tb, d = h_ref.shape
    cw = lam_ref.shape[1]
    blk = cw // n_blocks

    @pl.when(i == 0)
    def _():
        cx_ref[...] = hist_ref[...]
        ch_ref[...] = h0_ref[...]

    h = h_ref[...]
    proj = jnp.dot(_rms(h, g_ref[...]).astype(BF16), win_ref[...], preferred_element_type=F32)
    y_br = proj[:, :cw]
    x_br = proj[:, cw:]
    xc = _dwconv(x_br, cx_ref[...], cw_ref[...]) + cb_ref[...]
    r_parts, i_parts = [], []
    for n in range(n_blocks):
        ri = jnp.dot(xc[:, n * blk:(n + 1) * blk].astype(BF16), wgate_ref[n], preferred_element_type=F32)
        r_parts.append(ri[:, :blk])
        i_parts.append(ri[:, blk:])
    r = _sigmoid(jnp.concatenate(r_parts, axis=1) + bra_ref[...])
    ig = _sigmoid(jnp.concatenate(i_parts, axis=1) + bri_ref[...])
    log_a = -LRU_C * r * _softplus(-lam_ref[...])
    a = jnp.exp(log_a)
    th = jnp.tanh(log_a)
    b = jnp.sqrt(-2.0 * th / (1.0 - th)) * (ig * xc)
    row8 = lax.broadcasted_iota(jnp.int32, (tb, cw), 0) & (SUBLANES - 1)
    step = 1
    while step < SUBLANES:
        a_s = jnp.where(row8 >= step, pltpu.roll(a, step, axis=0), 1.0)
        b_s = jnp.where(row8 >= step, pltpu.roll(b, step, axis=0), 0.0)
        b = a * b_s + b
        a = a * a_s
        step *= 2
    h_in = ch_ref[SUBLANES - 1:SUBLANES, :]
    groups = []
    for j in range(tb // SUBLANES):
        rows = slice(j * SUBLANES, (j + 1) * SUBLANES)
        hg = b[rows] + a[rows] * h_in
        groups.append(hg)
        h_in = hg[SUBLANES - 1:SUBLANES, :]
    hs = jnp.concatenate(groups, axis=0)
    cx_ref[...] = x_br[tb - SUBLANES:, :]
    ch_ref[...] = hs[tb - SUBLANES:, :]

    @pl.when(i == pl.num_programs(1) - 1)
    def _():
        hlast_ref[...] = hs[tb - SUBLANES:, :]
        xtail_ref[...] = x_br[tb - SUBLANES:, :]

    o_ref[...] = h + jnp.dot((hs * _gelu(y_br)).astype(BF16), wout_ref[...], preferred_element_type=F32)


def _rglru(h, h_out, row0, n_seq, t, tb, g, w_in, conv_w, conv_b, wgate, b_ra, b_ri, lam, w_out, hist8, h08):
    m, d = h.shape
    cw = lam.shape[0]
    nb = wgate.shape[0]
    nt = t // tb
    r0 = row0 // tb
    vec = lambda x: x.reshape(1, -1)
    cst = lambda shape: pl.BlockSpec(shape, lambda b, i: (0,) * len(shape))
    in_specs = [pl.BlockSpec((tb, d), lambda b, i: (r0 + b * nt + i, 0)),
                cst((1, d)), cst((d, 2 * cw)), cst((CONV_W, cw)), cst((1, cw)),
                cst(wgate.shape), cst((1, cw)), cst((1, cw)), cst((1, cw)), cst((cw, d)),
                pl.BlockSpec((None, SUBLANES, cw), lambda b, i: (b, 0, 0)),
                pl.BlockSpec((None, SUBLANES, cw), lambda b, i: (b, 0, 0))]
    args = [h, vec(g), w_in, conv_w, vec(conv_b), wgate, vec(b_ra), vec(b_ri), vec(lam), w_out, hist8, h08]
    aliases = {}
    if h_out is not None:
        in_specs.append(pl.BlockSpec(memory_space=pl.ANY))
        args.append(h_out)
        aliases = {len(args) - 1: 0}
    kern = functools.partial(_rglru_kernel, n_blocks=nb)
    if h_out is None:
        kern = lambda *refs: _rglru_kernel(*refs[:12], None, *refs[12:], n_blocks=nb)
    return pl.pallas_call(
        kern,
        grid=(n_seq, nt),
        in_specs=in_specs,
        out_specs=[pl.BlockSpec((tb, d), lambda b, i: (r0 + b * nt + i, 0)),
                   pl.BlockSpec((None, SUBLANES, cw), lambda b, i: (b, 0, 0)),
                   pl.BlockSpec((None, SUBLANES, cw), lambda b, i: (b, 0, 0))],
        out_shape=[jax.ShapeDtypeStruct((m, d), F32),
                   jax.ShapeDtypeStruct((n_seq, SUBLANES, cw), F32),
                   jax.ShapeDtypeStruct((n_seq, SUBLANES, cw), F32)],
        scratch_shapes=[pltpu.VMEM((SUBLANES, cw), F32), pltpu.VMEM((SUBLANES, cw), F32)],
        input_output_aliases=aliases,
        compiler_params=_cp("parallel", "arbitrary"),
        name="rglru",
    )(*args)


def _gmlp_kernel(h_ref, g_ref, win_ref, bin_ref, lng_ref, lnb_ref, ws_ref, bs_ref, wout_ref, o_ref, vn_ref,
                 *, n_groups, first_sample_tile, lc_prompt, lc_sample):
    i = pl.program_id(0)
    tm, d = h_ref.shape
    half = lng_ref.shape[1]
    gw = half // n_groups
    h = h_ref[...]
    act = _gelu(jnp.dot(_rms(h, g_ref[...]).astype(BF16), win_ref[...], preferred_element_type=F32) + bin_ref[...])
    u = act[:, :half]
    v = act[:, half:]
    vc = v - jnp.mean(v, axis=-1, keepdims=True)
    vn = vc * lax.rsqrt(jnp.mean(vc * vc, axis=-1, keepdims=True) + EPS) * lng_ref[...] + lnb_ref[...]
    vn_ref[...] = vn
    shift = jnp.where(i >= first_sample_tile, lc_sample.bit_length() - 1, lc_prompt.bit_length() - 1)
    ri = lax.broadcasted_iota(jnp.int32, (tm, tm), 0)
    ci = lax.broadcasted_iota(jnp.int32, (tm, tm), 1)
    keep = (lax.shift_right_logical(ri, shift) == lax.shift_right_logical(ci, shift)) & (ri >= ci)
    vnb = vn.astype(BF16)
    parts = []
    for gi in range(n_groups):
        ws = jnp.where(keep, ws_ref[0, gi], 0.0).astype(BF16)
        sg = jnp.dot(ws, vnb[:, gi * gw:(gi + 1) * gw], preferred_element_type=F32)
        parts.append(u[:, gi * gw:(gi + 1) * gw] * (sg + bs_ref[0, :, gi * gw:(gi + 1) * gw]))
    gated = jnp.concatenate(parts, axis=1).astype(BF16)
    o_ref[...] = h + jnp.dot(gated, wout_ref[...], preferred_element_type=F32)


def _gmlp(h, m_p, t_s, g, w_in, b_in, ln_g, ln_b, w_s, b_s, w_out):
    m, d = h.shape
    half = ln_g.shape[0]
    ng = w_s.shape[0]
    gw = half // ng
    tm = 256
    lc_p, lc_s = D_CHUNK, min(t_s, D_CHUNK)
    assert m_p % tm == 0 and (m - m_p) % tm == 0 and tm % lc_p == 0 and tm % lc_s == 0

    def tiled(lc):
        wt = jnp.tile(w_s[:, :lc, :lc], (1, tm // lc, tm // lc))
        bt = jnp.repeat(jnp.tile(b_s[:, :lc].T, (tm // lc, 1)), gw, axis=1)
        return wt, bt

    wt_p, bt_p = tiled(lc_p)
    wt_s, bt_s = tiled(lc_s)
    ws_all = jnp.stack([wt_p, wt_s])
    bs_all = jnp.stack([bt_p, bt_s])
    fst = m_p // tm
    vec = lambda x: x.reshape(1, -1)
    cst = lambda shape: pl.BlockSpec(shape, lambda i: (0,) * len(shape))
    return pl.pallas_call(
        functools.partial(_gmlp_kernel, n_groups=ng, first_sample_tile=fst, lc_prompt=lc_p, lc_sample=lc_s),
        grid=(m // tm,),
        in_specs=[pl.BlockSpec((tm, d), lambda i: (i, 0)),
                  cst((1, d)), cst((d, 2 * half)), cst((1, 2 * half)), cst((1, half)), cst((1, half)),
                  pl.BlockSpec((1, ng, tm, tm), lambda i: (i // fst, 0, 0, 0)),
                  pl.BlockSpec((1, tm, half), lambda i: (i // fst, 0, 0)),
                  cst((half, d))],
        out_specs=[pl.BlockSpec((tm, d), lambda i: (i, 0)),
                   pl.BlockSpec((tm, half), lambda i: (jnp.maximum(i - fst, 0), 0))],
        out_shape=[jax.ShapeDtypeStruct((m, d), F32),
                   jax.ShapeDtypeStruct((m - m_p, half), F32)],
        compiler_params=_cp("arbitrary"),
        name="gmlp",
    )(h, vec(g), w_in, vec(b_in), vec(ln_g), vec(ln_b), ws_all, bs_all, w_out)


def _pad_rows8(x3):
    n, r, c = x3.shape
    return jnp.concatenate([jnp.zeros((n, SUBLANES - r, c), x3.dtype), x3], axis=1)


def kernel(x_prompt, x_sample, p_prompt, p_sample, cache_k_a, cache_v_a, state_s_b, state_conv_b, state_h_c, state_conv_c, g_mix, g_ffn, g_ple, g_final, w_in_a, g_q_a, g_k_a, rel_a, w_out_a, w_in_b, conv_b, a_log_b, dt_bias_b, g_o_b, w_out_b, w_in_c, conv_c, conv_bias_c, w_ra_c, b_ra_c, w_ri_c, b_ri_c, lam_c, w_out_c, w_in_d, b_in_d, ln_g_d, ln_b_d, w_s_d, b_s_d, w_out_d, w_gate_ff, w_up_ff, w_down_ff, w_router, w_gate_e, w_up_e, w_down_e, w_ple, w_ple_gate):
    n_b, t, d = x_prompt.shape
    n_s, t_s, _ = x_sample.shape
    depth = p_prompt.shape[0]
    pd = p_prompt.shape[-1]
    m_p, m_s = n_b * t, n_s * t_s
    m = m_p + m_s
    a_heads = d // g_q_a.shape[-1]
    b_heads = a_log_b.shape[-1]
    n_exp = w_router.shape[-1]
    bf = lambda x: x.astype(BF16)

    h, h_tail = x_prompt.reshape(m_p, d), x_sample.reshape(m_s, d)
    pp_all = p_prompt.reshape(depth, m_p, pd)
    ps_all = p_sample.reshape(depth, m_s, pd)

    outs = {name: [] for name in ("k_a_p", "v_a_p", "k_a_s", "v_a_s", "s_b_p", "conv_b_p", "s_b_s", "conv_b_s",
                                  "h_c_p", "conv_c_p", "h_c_s", "conv_c_s", "v_d_s")}

    def tail_rows(x2, n_seq, tt, row0, cols):
        keep = CONV_W - 1
        if n_seq <= 4:
            return jnp.stack([x2[row0 + (b + 1) * tt - keep:row0 + (b + 1) * tt, :cols] for b in range(n_seq)])
        return x2[row0:row0 + n_seq * tt, :cols].reshape(n_seq, tt, cols)[:, tt - keep:]

    y = None
    for li in range(depth):
        kind, c = li % 4, li // 4
        if kind == 0:
            qkv = _proj(h, g_mix[li], bf(w_in_a[c]), 1024, tail=h_tail)
            w_keep = cache_k_a.shape[2]
            o, kn_p, kn_s = _attention(qkv, n_b, t, n_s, t_s,
                                       cache_k_a[c].reshape(n_s, w_keep, d), cache_v_a[c].reshape(n_s, w_keep, d),
                                       rel_a[c], g_q_a[c], g_k_a[c], a_heads)
            hd = d // a_heads
            keep = min(A_PREV_CHUNKS * CHUNK, t)
            v_p = jnp.stack([qkv[(b + 1) * t - keep:(b + 1) * t, 2 * d:] for b in range(n_b)])
            outs["k_a_p"].append(kn_p.reshape(n_b, keep, a_heads, hd))
            outs["v_a_p"].append(v_p.reshape(n_b, keep, a_heads, hd))
            outs["k_a_s"].append(kn_s.reshape(n_s, t_s, a_heads, hd))
            outs["v_a_s"].append(qkv[m_p:, 2 * d:].reshape(n_s, t_s, a_heads, hd))
            h = _outproj_res(h, o, bf(w_out_a[c]), tail=h_tail)
            h_tail = None
        elif kind == 1:
            w_in = w_in_b[c]
            nq = 3 * d
            w_main = bf(w_in[:, :nq + d])
            w_aux = bf(jnp.pad(w_in[:, nq + d:], ((0, 0), (0, LANES - 2 * b_heads))))
            proj, aux = _proj(h, g_mix[li], w_main, 1024, w_aux)
            pvec = jnp.zeros((2, LANES), F32)
            pvec = pvec.at[0, b_heads:2 * b_heads].set(a_log_b[c]).at[1, b_heads:2 * b_heads].set(dt_bias_b[c])
            hd = d // b_heads
            act_p, bg_p = _gdn_act(proj, aux, 0, n_b, t, 512, jnp.zeros((n_b, SUBLANES, nq), F32),
                                   conv_b[c], pvec, b_heads)
            o_p, s_p = _gdn_delta(act_p, bg_p, proj, 3, n_b, t,
                                  jnp.zeros((n_b, b_heads, hd, hd), F32), g_o_b[c], b_heads, out_rows=m)
            act_s, bg_s = _gdn_act(proj, aux, m_p, n_s, t_s, t_s, _pad_rows8(state_conv_b[c]),
                                   conv_b[c], pvec, b_heads)
            tp = -(-t_s // CHUNK) * CHUNK
            padt = lambda x2: jnp.pad(x2.reshape(n_s, t_s, -1), ((0, 0), (0, tp - t_s), (0, 0))).reshape(n_s * tp, -1)
            z_s = padt(proj[m_p:, nq:nq + d])
            o_s, s_s = _gdn_delta(padt(act_s), padt(bg_s), z_s, 0, n_s, tp, state_s_b[c], g_o_b[c], b_heads)
            o_s = o_s.reshape(n_s, tp, d)[:, :t_s].reshape(m_s, d)
            outs["s_b_p"].append(s_p)
            outs["s_b_s"].append(s_s)
            outs["conv_b_p"].append(tail_rows(proj, n_b, t, 0, nq))
            outs["conv_b_s"].append(tail_rows(proj, n_s, t_s, m_p, nq))
            h = _outproj_res(h, lax.dynamic_update_slice(o_p, o_s, (m_p, 0)), bf(w_out_b[c]))
        elif kind == 2:
            cw = lam_c.shape[-1]
            wgate = bf(jnp.concatenate([w_ra_c[c], w_ri_c[c]], axis=-1))
            common = (g_mix[li], bf(w_in_c[c]), conv_c[c], conv_bias_c[c], wgate, b_ra_c[c], b_ri_c[c],
                      lam_c[c], bf(w_out_c[c]))
            zeros8 = jnp.zeros((n_b, SUBLANES, cw), F32)
            h_new, hl_p, xt_p = _rglru(h, None, 0, n_b, t, 512, *common, zeros8, zeros8)
            h_new, hl_s, xt_s = _rglru(h, h_new, m_p, n_s, t_s, t_s, *common,
                                       _pad_rows8(state_conv_c[c]), _pad_rows8(state_h_c[c][:, None, :]))
            h = h_new
            outs["h_c_p"].append(hl_p[:, SUBLANES - 1])
            outs["h_c_s"].append(hl_s[:, SUBLANES - 1])
            outs["conv_c_p"].append(xt_p[:, SUBLANES - (CONV_W - 1):])
            outs["conv_c_s"].append(xt_s[:, SUBLANES - (CONV_W - 1):])
        else:
            h, vn_s = _gmlp(h, m_p, t_s, g_mix[li], bf(w_in_d[c]), b_in_d[c], ln_g_d[c], ln_b_d[c],
                            w_s_d[c], b_s_d[c], bf(w_out_d[c]))
            outs["v_d_s"].append(vn_s.reshape(n_s, t_s, -1))

        j = li // 2
        if li % 2 == 0:
            h = _ffn_dense(h, g_ffn[li], bf(w_gate_ff[j]), bf(w_up_ff[j]), bf(w_down_ff[j]))
        else:
            wr = jnp.pad(w_router[j], ((0, 0), (0, LANES - n_exp)))
            h = _moe(h, g_ffn[li], wr, bf(w_gate_e[j]), bf(w_up_e[j]), bf(w_down_e[j]))
        last = li == depth - 1
        res = _ple(h, pp_all, ps_all, li, g_ple[li], bf(w_ple[li]), bf(w_ple_gate[li]),
                   g_final if last else None)
        if last:
            y = res
        else:
            h = res

    st = {name: jnp.stack(rows) for name, rows in outs.items()}
    return (y[0].reshape(n_b, t, d), y[1].reshape(n_s, t_s, d),
            st["k_a_p"], st["v_a_p"], st["k_a_s"], st["v_a_s"],
            st["s_b_p"], st["conv_b_p"], st["s_b_s"], st["conv_b_s"],
            st["h_c_p"], st["conv_c_p"], st["h_c_s"], st["conv_c_s"],
            st["v_d_s"])
```

```python
import functools

import jax
import jax.numpy as jnp
from jax import lax
from jax.experimental import pallas as pl
from jax.experimental.pallas import tpu as pltpu

F32 = jnp.float32
BF16 = jnp.bfloat16
EPS = 1e-6
CHUNK = 64
CONV_W = 4
A_REL_CLIP = 128
A_PREV_CHUNKS = 8
LRU_C = 8.0
D_CHUNK = 128
TOP_K = 2
LANES = 128
SUBLANES = 8
VMEM_LIMIT = 48 * 1024 * 1024


def _cp(*sem, vmem=VMEM_LIMIT):
    return pltpu.CompilerParams(dimension_semantics=sem, vmem_limit_bytes=vmem)


def _tile_m(m):
    for t in (1024, 512, 256):
        if m % t == 0:
            return t
    raise ValueError(f"token count {m} must be a multiple of 256")


def _mm(a, b):
    return jnp.dot(a.astype(BF16), b.astype(BF16), preferred_element_type=F32)


def _mm_nt(a, b):
    return lax.dot_general(a.astype(BF16), b.astype(BF16), (((1,), (1,)), ((), ())),
                           preferred_element_type=F32)


def _mm_tn(a, b):
    return lax.dot_general(a.astype(BF16), b.astype(BF16), (((0,), (0,)), ((), ())),
                           preferred_element_type=F32)


def _split2(x):
    hi = x.astype(BF16)
    lo = (x - hi.astype(F32)).astype(BF16)
    return hi, lo


def _split3(x):
    hi = x.astype(BF16)
    r = x - hi.astype(F32)
    mid = r.astype(BF16)
    lo = (r - mid.astype(F32)).astype(BF16)
    return hi, mid, lo


def _rms(x, g):
    return x * lax.rsqrt(jnp.mean(x * x, axis=-1, keepdims=True) + EPS) * g


def _sigmoid(x):
    return 0.5 * (1.0 + jnp.tanh(0.5 * x))


def _silu(x):
    return x * _sigmoid(x)


def _softplus(x):
    return jnp.maximum(x, 0.0) + jnp.log1p(jnp.exp(-jnp.abs(x)))


def _gelu(x):
    c = 0.7978845608028654
    return x * (0.5 * (1.0 + jnp.tanh(c * (x + 0.044715 * (x * x * x)))))


def _shift_rows(x, prev8, s):
    r = pltpu.roll(x, s, axis=0)
    pr = pltpu.roll(prev8, s, axis=0)
    row8 = lax.broadcasted_iota(jnp.int32, pr.shape, 0)
    top = jnp.where(row8 < s, pr, r[0:SUBLANES])
    return jnp.concatenate([top, r[SUBLANES:]], axis=0)


def _dwconv(x, prev8, w, stage_ref=None):
    n = x.shape[0]
    if stage_ref is not None:
        stage_ref[0:SUBLANES, :] = prev8
        stage_ref[SUBLANES:SUBLANES + n, :] = x
    y = None
    for k in range(CONV_W):
        s = CONV_W - 1 - k
        if s == 0:
            xs = x
        elif stage_ref is not None:
            xs = stage_ref[SUBLANES - s:SUBLANES - s + n, :]
        else:
            xs = _shift_rows(x, prev8, s)
        term = xs * w[k:k + 1]
        y = term if y is None else y + term
    return y


def _two_source_specs(tm, k, n_head_tiles, n_grid_axes):
    if n_grid_axes == 1:
        return [pl.BlockSpec((tm, k), lambda i: (jnp.minimum(i, n_head_tiles - 1), 0)),
                pl.BlockSpec((tm, k), lambda i: (jnp.maximum(i - n_head_tiles, 0), 0))]
    return [pl.BlockSpec((tm, k), lambda i, j: (jnp.minimum(i, n_head_tiles - 1), 0)),
            pl.BlockSpec((tm, k), lambda i, j: (jnp.maximum(i - n_head_tiles, 0), 0))]


def _proj_kernel(*refs, has_aux, n_head_tiles):
    if n_head_tiles is None:
        x_ref, refs = refs[0], refs[1:]
        load_x = lambda: x_ref[...]
    else:
        xa_ref, xb_ref, refs = refs[0], refs[1], refs[2:]
        load_x = lambda: jnp.where(pl.program_id(0) >= n_head_tiles, xb_ref[...], xa_ref[...])
    if has_aux:
        g_ref, w_ref, waux_ref, o_ref, oaux_ref, hn_ref = refs
    else:
        g_ref, w_ref, o_ref, hn_ref = refs

    @pl.when(pl.program_id(1) == 0)
    def _():
        hn = _rms(load_x(), g_ref[...]).astype(BF16)
        hn_ref[...] = hn
        if has_aux:
            oaux_ref[...] = jnp.dot(hn, waux_ref[...], preferred_element_type=F32)

    o_ref[...] = jnp.dot(hn_ref[...], w_ref[...], preferred_element_type=F32)


def _proj(h, g, w, tn, waux=None, tail=None):
    m, k = h.shape
    if tail is not None:
        m += tail.shape[0]
    n = w.shape[1]
    tm = _tile_m(m)
    nht = None
    if tail is None:
        in_specs = [pl.BlockSpec((tm, k), lambda i, j: (i, 0))]
        args = [h]
    else:
        assert h.shape[0] % tm == 0 and tail.shape[0] % tm == 0
        nht = h.shape[0] // tm
        in_specs = _two_source_specs(tm, k, nht, 2)
        args = [h, tail]
    in_specs += [pl.BlockSpec((1, k), lambda i, j: (0, 0)),
                 pl.BlockSpec((k, tn), lambda i, j: (0, j))]
    args += [g.reshape(1, k), w]
    out_shape = [jax.ShapeDtypeStruct((m, n), F32)]
    out_specs = [pl.BlockSpec((tm, tn), lambda i, j: (i, j))]
    if waux is not None:
        na = waux.shape[1]
        in_specs.append(pl.BlockSpec((k, na), lambda i, j: (0, 0)))
        out_shape.append(jax.ShapeDtypeStruct((m, na), F32))
        out_specs.append(pl.BlockSpec((tm, na), lambda i, j: (i, 0)))
        args.append(waux)
    res = pl.pallas_call(
        functools.partial(_proj_kernel, has_aux=waux is not None, n_head_tiles=nht),
        grid=(m // tm, n // tn),
        in_specs=in_specs, out_specs=out_specs, out_shape=out_shape,
        scratch_shapes=[pltpu.VMEM((tm, k), BF16)],
        compiler_params=_cp("parallel", "arbitrary"),
        name="proj",
    )(*args)
    return res if waux is not None else res[0]


def _outproj_kernel(*refs, n_head_tiles):
    if n_head_tiles is None:
        h_ref, o_ref, w_ref, out_ref = refs
        h = h_ref[...]
    else:
        ha_ref, hb_ref, o_ref, w_ref, out_ref = refs
        h = jnp.where(pl.program_id(0) >= n_head_tiles, hb_ref[...], ha_ref[...])
    out_ref[...] = h + jnp.dot(o_ref[...].astype(BF16), w_ref[...], preferred_element_type=F32)


def _outproj_res(h, o, w, tail=None):
    m, k = o.shape
    d = h.shape[1]
    tm = _tile_m(m)
    nht = None
    if tail is None:
        in_specs = [pl.BlockSpec((tm, d), lambda i: (i, 0))]
        args = [h]
    else:
        assert h.shape[0] % tm == 0 and tail.shape[0] % tm == 0
        nht = h.shape[0] // tm
        in_specs = _two_source_specs(tm, d, nht, 1)
        args = [h, tail]
    return pl.pallas_call(
        functools.partial(_outproj_kernel, n_head_tiles=nht),
        grid=(m // tm,),
        in_specs=in_specs + [pl.BlockSpec((tm, k), lambda i: (i, 0)),
                             pl.BlockSpec((k, d), lambda i: (0, 0))],
        out_specs=pl.BlockSpec((tm, d), lambda i: (i, 0)),
        out_shape=jax.ShapeDtypeStruct((m, d), F32),
        compiler_params=_cp("parallel"),
        name="outproj_res",
    )(*args, o, w)


def _ffn_kernel(x_ref, g_ref, wg_ref, wu_ref, wd_ref, o_ref, hn_ref):
    @pl.when(pl.program_id(1) == 0)
    def _():
        x = x_ref[...]
        hn_ref[...] = _rms(x, g_ref[...]).astype(BF16)
        o_ref[...] = x

    hn = hn_ref[...]
    gg = jnp.dot(hn, wg_ref[...], preferred_element_type=F32)
    uu = jnp.dot(hn, wu_ref[...], preferred_element_type=F32)
    hid = (_silu(gg) * uu).astype(BF16)
    o_ref[...] += jnp.dot(hid, wd_ref[...], preferred_element_type=F32)


def _ffn_dense(h, g, wg, wu, wd):
    m, d = h.shape
    f = wg.shape[1]
    tm = 512
    tf = f // 2 if (f // 2) % LANES == 0 else 256
    return pl.pallas_call(
        _ffn_kernel,
        grid=(m // tm, f // tf),
        in_specs=[pl.BlockSpec((tm, d), lambda i, j: (i, 0)),
                  pl.BlockSpec((1, d), lambda i, j: (0, 0)),
                  pl.BlockSpec((d, tf), lambda i, j: (0, j)),
                  pl.BlockSpec((d, tf), lambda i, j: (0, j)),
                  pl.BlockSpec((tf, d), lambda i, j: (j, 0))],
        out_specs=pl.BlockSpec((tm, d), lambda i, j: (i, 0)),
        out_shape=jax.ShapeDtypeStruct((m, d), F32),
        scratch_shapes=[pltpu.VMEM((tm, d), BF16)],
        compiler_params=_cp("parallel", "arbitrary"),
        name="ffn_dense",
    )(h, g.reshape(1, d), wg, wu, wd)


def _moe_kernel(x_ref, g_ref, wr_ref, tril_ref, wg_ref, wu_ref, wd_ref, o_ref, hn_ref, gates_ref, rank_ref, rankt_ref,
                *, n_experts, pass_rows):
    e = pl.program_id(1)
    n_sub, _, sub = rankt_ref.shape

    @pl.when(e == 0)
    def _():
        tril = tril_ref[...]
        lane = lax.broadcasted_iota(jnp.int32, (sub, LANES), 1)
        for s in range(n_sub):
            rs = slice(s * sub, (s + 1) * sub)
            x = x_ref[rs, :]
            hn = _rms(x, g_ref[...])
            hn_hi, hn_lo = _split2(hn)
            hn_ref[rs, :] = hn_hi
            o_ref[rs, :] = x
            both = jnp.dot(hn_hi, wr_ref[...], preferred_element_type=F32)
            logits = both[:, :LANES] + (both[:, LANES:] + jnp.dot(hn_lo, wr_ref[:, :LANES],
                                                                  preferred_element_type=F32))
            lg = jnp.where(lane < n_experts, logits, -jnp.inf)
            m1 = jnp.max(lg, axis=-1, keepdims=True)
            i1 = jnp.min(jnp.where(lg == m1, lane, LANES), axis=-1, keepdims=True)
            lg2 = jnp.where(lane == i1, -jnp.inf, lg)
            m2 = jnp.max(lg2, axis=-1, keepdims=True)
            i2 = jnp.min(jnp.where(lg2 == m2, lane, LANES), axis=-1, keepdims=True)
            e2 = jnp.exp(m2 - m1)
            w1 = 1.0 / (1.0 + e2)
            w2 = e2 / (1.0 + e2)
            gates_ref[rs, :] = jnp.where(lane == i1, w1, 0.0) + jnp.where(lane == i2, w2, 0.0)
            sel = jnp.where(lane == i1, 1.0, jnp.where(lane == i2, 1.0, 0.0))
            count = jnp.dot(tril, sel.astype(BF16), preferred_element_type=F32)
            rank = jnp.where(sel > 0.0, count - 1.0, -1.0)
            rank_ref[rs, :] = rank
            rankt_ref[s] = rank.T

    lane = lax.broadcasted_iota(jnp.int32, (sub, LANES), 1)
    for s in range(n_sub):
        rs = slice(s * sub, (s + 1) * sub)
        rank_row = rankt_ref[s, pl.ds(e, 1), :]
        n_routed = (jnp.max(rank_row) + 1.0).astype(jnp.int32)
        rank_col = jnp.sum(jnp.where(lane == e, rank_ref[rs, :], 0.0), axis=-1, keepdims=True)
        gate_col = jnp.sum(jnp.where(lane == e, gates_ref[rs, :], 0.0), axis=-1, keepdims=True)

        def one_pass(cap, base, rs=rs, rank_row=rank_row, rank_col=rank_col, gate_col=gate_col):
            row_id = lax.broadcasted_iota(jnp.int32, (cap, sub), 0).astype(F32) + base
            col_id = lax.broadcasted_iota(jnp.int32, (sub, cap), 1).astype(F32) + base
            pick = jnp.where(rank_row == row_id, 1.0, 0.0).astype(BF16)
            xe = jnp.dot(pick, hn_ref[rs, :], preferred_element_type=F32).astype(BF16)
            gg = jnp.dot(xe, wg_ref[0], preferred_element_type=F32)
            uu = jnp.dot(xe, wu_ref[0], preferred_element_type=F32)
            hid = (_silu(gg) * uu).astype(BF16)
            ye = jnp.dot(hid, wd_ref[0], preferred_element_type=F32).astype(BF16)
            put = jnp.where(rank_col == col_id, gate_col, 0.0).astype(BF16)
            o_ref[rs, :] += jnp.dot(put, ye, preferred_element_type=F32)

        *single, full = pass_rows
        below = 0
        for rows_ in single:
            @pl.when((n_routed > below) & (n_routed <= rows_))
            def _(one_pass=one_pass, rows_=rows_):
                one_pass(rows_, 0.0)
            below = rows_

        @pl.when(n_routed > below)
        def _(one_pass=one_pass, n_routed=n_routed):
            def body(cc, carry):
                one_pass(full, (cc * full).astype(F32))
                return carry
            lax.fori_loop(0, (n_routed + full - 1) // full, body, 0)


def _moe(h, g, wr, wg, wu, wd):
    m, d = h.shape
    ne, _, ef = wg.shape
    sub = 512
    tm = 2 * sub if m % (2 * sub) == 0 else sub
    mean_rows = sub * TOP_K // ne
    pass_rows = tuple(mean_rows * k // 8 for k in (6, 7, 8, 9, 10, 11, 12, 14, 16))
    assert m % tm == 0
    return pl.pallas_call(
        functools.partial(_moe_kernel, n_experts=ne, pass_rows=pass_rows),
        grid=(m // tm, ne),
        in_specs=[pl.BlockSpec((tm, d), lambda i, e: (i, 0)),
                  pl.BlockSpec((1, d), lambda i, e: (0, 0)),
                  pl.BlockSpec((d, 2 * LANES), lambda i, e: (0, 0)),
                  pl.BlockSpec((sub, sub), lambda i, e: (0, 0)),
                  pl.BlockSpec((1, d, ef), lambda i, e: (e, 0, 0)),
                  pl.BlockSpec((1, d, ef), lambda i, e: (e, 0, 0)),
                  pl.BlockSpec((1, ef, d), lambda i, e: (e, 0, 0))],
        out_specs=pl.BlockSpec((tm, d), lambda i, e: (i, 0)),
        out_shape=jax.ShapeDtypeStruct((m, d), F32),
        scratch_shapes=[pltpu.VMEM((tm, d), BF16), pltpu.VMEM((tm, LANES), F32),
                        pltpu.VMEM((tm, LANES), F32), pltpu.VMEM((tm // sub, LANES, sub), F32)],
        compiler_params=_cp("parallel", "arbitrary"),
        name="moe",
    )(h, g.reshape(1, d), jnp.concatenate(_split2(wr), axis=1), jnp.tril(jnp.ones((sub, sub), BF16)), wg, wu, wd)


def _ple_kernel(*refs, final, n_prompt_tiles):
    if final:
        x_ref, pp_ref, ps_ref, g_ref, wp_ref, wg_ref, gf_ref, yp_ref, ys_ref = refs
    else:
        x_ref, pp_ref, ps_ref, g_ref, wp_ref, wg_ref, o_ref = refs
    i = pl.program_id(0)
    is_sample = i >= n_prompt_tiles
    x = x_ref[...]
    hn = _rms(x, g_ref[...]).astype(BF16)
    gate = _sigmoid(jnp.dot(hn, wg_ref[...], preferred_element_type=F32))
    p = jnp.where(is_sample, ps_ref[...], pp_ref[...])
    pp = jnp.dot(p.astype(BF16), wp_ref[...], preferred_element_type=F32)
    out = x + pp * gate
    if final:
        y = _rms(out, gf_ref[...])

        @pl.when(jnp.logical_not(is_sample))
        def _():
            yp_ref[...] = y

        @pl.when(is_sample)
        def _():
            ys_ref[...] = y
    else:
        o_ref[...] = out


def _ple(h, p_prompt, p_sample, layer, g, wp, wgate, g_final=None):
    m, d = h.shape
    _, m_p, pd = p_prompt.shape
    m_s = p_sample.shape[1]
    tm = _tile_m(m)
    assert m_p % tm == 0 and m_s % tm == 0
    npt, nst = m_p // tm, m_s // tm
    final = g_final is not None
    in_specs = [pl.BlockSpec((tm, d), lambda i: (i, 0)),
                pl.BlockSpec((None, tm, pd), lambda i: (layer, jnp.minimum(i, npt - 1), 0)),
                pl.BlockSpec((None, tm, pd), lambda i: (layer, jnp.maximum(i - npt, 0), 0)),
                pl.BlockSpec((1, d), lambda i: (0, 0)),
                pl.BlockSpec((pd, d), lambda i: (0, 0)),
                pl.BlockSpec((d, d), lambda i: (0, 0))]
    args = [h, p_prompt, p_sample, g.reshape(1, d), wp, wgate]
    if final:
        in_specs.append(pl.BlockSpec((1, d), lambda i: (0, 0)))
        args.append(g_final.reshape(1, d))
        out_specs = [pl.BlockSpec((tm, d), lambda i: (jnp.minimum(i, npt - 1), 0)),
                     pl.BlockSpec((tm, d), lambda i: (jnp.maximum(i - npt, 0), 0))]
        out_shape = [jax.ShapeDtypeStruct((m_p, d), F32), jax.ShapeDtypeStruct((m_s, d), F32)]
    else:
        out_specs = pl.BlockSpec((tm, d), lambda i: (i, 0))
        out_shape = jax.ShapeDtypeStruct((m, d), F32)
    return pl.pallas_call(
        functools.partial(_ple_kernel, final=final, n_prompt_tiles=npt),
        grid=(npt + nst,),
        in_specs=in_specs, out_specs=out_specs, out_shape=out_shape,
        compiler_params=_cp("arbitrary"),
        name="ple",
    )(*args)


def _pair_rms(x, g, hd):
    lane = lax.broadcasted_iota(jnp.int32, x.shape, 1)
    lo = lane < hd
    x2 = x * x
    s0 = jnp.sum(jnp.where(lo, x2, 0.0), axis=-1, keepdims=True)
    s1 = jnp.sum(jnp.where(lo, 0.0, x2), axis=-1, keepdims=True)
    ms = jnp.where(lo, s0, s1) * (1.0 / hd)
    return x * lax.rsqrt(ms + EPS) * g


def _stack_heads(q, hd):
    lane = lax.broadcasted_iota(jnp.int32, q.shape, 1)
    lo = lane < hd
    return jnp.concatenate([jnp.where(lo, q, 0.0), jnp.where(lo, 0.0, q)], axis=0).astype(BF16)


def _unstack_heads(o2, hd):
    n = o2.shape[0] // 2
    lane = lax.broadcasted_iota(jnp.int32, (n, o2.shape[1]), 1)
    return jnp.where(lane < hd, o2[:n], o2[n:])


def _attn_prompt_kernel(q_ref, k_ref, v_ref, b_ref, gq_ref, gk_ref, o_ref, kn_ref, kwin_ref, vwin_ref,
                        *, hd, scale, reach, chunk_group):
    i = pl.program_id(2)
    tq, width = q_ref.shape
    band = reach + CHUNK
    pw = 2 * hd
    lanes = [slice(p * pw, (p + 1) * pw) for p in range(width // pw)]
    qn = [_pair_rms(q_ref[:, c], gq_ref[...], hd) * scale for c in lanes]
    knc = [_pair_rms(k_ref[:, c], gk_ref[...], hd) for c in lanes]

    @pl.when(i == pl.num_programs(2) - 1)
    def _():
        for c, x in zip(lanes, knc):
            kn_ref[:, c] = x

    @pl.when(i == 0)
    def _():
        kwin_ref[0:tq, :] = jnp.zeros((tq, width), BF16)
        vwin_ref[0:tq, :] = jnp.zeros((tq, width), BF16)

    @pl.when(i > 0)
    def _():
        kwin_ref[0:tq, :] = kwin_ref[tq:2 * tq, :]
        vwin_ref[0:tq, :] = vwin_ref[tq:2 * tq, :]

    for c, x in zip(lanes, knc):
        kwin_ref[tq:2 * tq, c] = x.astype(BF16)
    vwin_ref[tq:2 * tq, :] = v_ref[...].astype(BF16)
    col = lax.broadcasted_iota(jnp.int32, (1, band), 1)
    n_chunks = tq // CHUNK
    w0s = [tq - reach + jj * CHUNK for jj in range(n_chunks)]

    def attend(first_block, items):
        q2 = [_stack_heads(qn[p][jj * CHUNK:(jj + 1) * CHUNK], hd) for p, jj in items]
        s = [_mm_nt(x, kwin_ref[w0s[jj]:w0s[jj] + band, lanes[p]]) + b_ref[p] for x, (p, jj) in zip(q2, items)]
        if first_block:
            s = [jnp.where(col + w0s[jj] >= tq, x, -jnp.inf) for x, (p, jj) in zip(s, items)]
        mx = [jnp.max(x, axis=-1, keepdims=True) for x in s]
        e = [jnp.exp(x - m) for x, m in zip(s, mx)]
        den = [jnp.sum(x, axis=-1, keepdims=True) for x in e]
        o2 = [jnp.dot(x.astype(BF16), vwin_ref[w0s[jj]:w0s[jj] + band, lanes[p]], preferred_element_type=F32)
              for x, (p, jj) in zip(e, items)]
        for x, dn, (p, jj) in zip(o2, den, items):
            o_ref[jj * CHUNK:(jj + 1) * CHUNK, lanes[p]] = _unstack_heads(x / dn, hd)

    def attend_all(first_block):
        items = [(p, jj) for jj in range(n_chunks) for p in range(len(lanes))]
        for j0 in range(0, len(items), chunk_group):
            attend(first_block, items[j0:j0 + chunk_group])

    @pl.when(i == 0)
    def _():
        attend_all(True)

    @pl.when(i > 0)
    def _():
        attend_all(False)


def _attn_sample_kernel(q_ref, k_ref, v_ref, ck_ref, cv_ref, b_ref, gq_ref, gk_ref, oin_ref, o_ref, kn_ref,
                        *, hd, scale):
    del oin_ref
    pw = 2 * hd
    pairs = [slice(p * pw, (p + 1) * pw) for p in range(q_ref.shape[1] // pw)]
    qn = [_pair_rms(q_ref[:, c], gq_ref[...], hd) * scale for c in pairs]
    kn = [_pair_rms(k_ref[:, c], gk_ref[...], hd) for c in pairs]
    for c, x in zip(pairs, kn):
        kn_ref[:, c] = x
    kcat = [jnp.concatenate([ck_ref[:, c], x], axis=0).astype(BF16) for c, x in zip(pairs, kn)]
    vcat = [jnp.concatenate([cv_ref[:, c], v_ref[:, c]], axis=0).astype(BF16) for c in pairs]
    s = [_mm_nt(_stack_heads(x, hd), kk) + b_ref[p] for p, (x, kk) in enumerate(zip(qn, kcat))]
    mx = [jnp.max(x, axis=-1, keepdims=True) for x in s]
    e = [jnp.exp(x - m) for x, m in zip(s, mx)]
    den = [jnp.sum(x, axis=-1, keepdims=True) for x in e]
    o2 = [jnp.dot(x.astype(BF16), vv, preferred_element_type=F32) for x, vv in zip(e, vcat)]
    for c, x, dn in zip(pairs, o2, den):
        o_ref[:, c] = _unstack_heads(x / dn, hd)


def _rel_bias_pairs(table, q0, nq, nk):
    nh = table.shape[0]
    span = nq + nk - 1
    dmin, dmax = q0 - (nk - 1), q0 + nq - 1
    lo, hi = max(dmin, -A_REL_CLIP), min(dmax, A_REL_CLIP)
    line = jnp.concatenate([jnp.broadcast_to(table[:, :1], (nh, lo - dmin)),
                            table[:, lo + A_REL_CLIP:hi + A_REL_CLIP + 1],
                            jnp.broadcast_to(table[:, -1:], (nh, dmax - hi))], axis=1)
    flipped = line[:, ::-1]
    period = jnp.concatenate([flipped, jnp.zeros((nh, 1), table.dtype)], axis=1)
    shifted = jnp.tile(period, (1, nq))[:, :nq * span].reshape(nh, nq, span)
    b = shifted[:, :, nq - 1:nq - 1 + nk]
    return b.reshape(nh // 2, 2 * nq, nk)


def _attention(qkv, n_b, t, n_s, t_s, cache_k, cache_v, rel, g_q, g_k, n_heads):
    m_tot, d3 = qkv.shape
    d = d3 // 3
    hd = d // n_heads
    pw = 2 * hd
    n_pairs = d // pw
    m_p = n_b * t
    reach = A_PREV_CHUNKS * CHUNK
    tq = reach
    assert pw == LANES and t % tq == 0 and t_s % 16 == 0
    nqb = t // tq
    scale = hd ** -0.5
    gq2 = jnp.tile(g_q, 2).reshape(1, pw)
    gk2 = jnp.tile(g_k, 2).reshape(1, pw)

    bias_p = _rel_bias_pairs(rel, reach, CHUNK, reach + CHUNK)

    def rows(b, p, i):
        return b * nqb + i

    ppb = 2 if n_pairs % 2 == 0 else 1
    npb = n_pairs // ppb
    blk = (tq, ppb * pw)
    o_all, kn_p = pl.pallas_call(
        functools.partial(_attn_prompt_kernel, hd=hd, scale=scale, reach=reach, chunk_group=4),
        grid=(n_b, npb, nqb),
        in_specs=[pl.BlockSpec(blk, lambda b, p, i: (rows(b, p, i), p)),
                  pl.BlockSpec(blk, lambda b, p, i: (rows(b, p, i), npb + p)),
                  pl.BlockSpec(blk, lambda b, p, i: (rows(b, p, i), 2 * npb + p)),
                  pl.BlockSpec((ppb, 2 * CHUNK, reach + CHUNK), lambda b, p, i: (p, 0, 0)),
                  pl.BlockSpec((1, pw), lambda b, p, i: (0, 0)),
                  pl.BlockSpec((1, pw), lambda b, p, i: (0, 0))],
        out_specs=[pl.BlockSpec(blk, lambda b, p, i: (rows(b, p, i), p)),
                   pl.BlockSpec(blk, lambda b, p, i: (b, p))],
        out_shape=[jax.ShapeDtypeStruct((m_tot, d), F32),
                   jax.ShapeDtypeStruct((n_b * tq, d), F32)],
        scratch_shapes=[pltpu.VMEM((2 * tq, ppb * pw), BF16), pltpu.VMEM((2 * tq, ppb * pw), BF16)],
        compiler_params=_cp("parallel", "parallel", "arbitrary"),
        name="attn_prompt",
    )(qkv, qkv, qkv, bias_p, gq2, gk2)

    w = cache_k.shape[1]
    bias_s = _rel_bias_pairs(rel, w, t_s, w + t_s)
    srow = m_p // t_s
    sblk = (t_s, d)
    o_all, kn_s = pl.pallas_call(
        functools.partial(_attn_sample_kernel, hd=hd, scale=scale),
        grid=(n_s,),
        in_specs=[pl.BlockSpec(sblk, lambda b: (srow + b, 0)),
                  pl.BlockSpec(sblk, lambda b: (srow + b, 1)),
                  pl.BlockSpec(sblk, lambda b: (srow + b, 2)),
                  pl.BlockSpec((None, w, d), lambda b: (b, 0, 0)),
                  pl.BlockSpec((None, w, d), lambda b: (b, 0, 0)),
                  pl.BlockSpec((n_pairs, 2 * t_s, w + t_s), lambda b: (0, 0, 0)),
                  pl.BlockSpec((1, pw), lambda b: (0, 0)),
                  pl.BlockSpec((1, pw), lambda b: (0, 0)),
                  pl.BlockSpec(memory_space=pl.ANY)],
        out_specs=[pl.BlockSpec(sblk, lambda b: (srow + b, 0)),
                   pl.BlockSpec(sblk, lambda b: (b, 0))],
        out_shape=[jax.ShapeDtypeStruct((m_tot, d), F32),
                   jax.ShapeDtypeStruct((n_s * t_s, d), F32)],
        input_output_aliases={8: 0},
        compiler_params=_cp("parallel"),
        name="attn_sample",
    )(qkv, qkv, qkv, cache_k, cache_v, bias_s, gq2, gk2, o_all)
    return o_all, kn_p, kn_s


def _gdn_act_kernel(xq_ref, xk_ref, xv_ref, pq_ref, pk_ref, pv_ref, hq_ref, hk_ref, hv_ref,
                    aux_ref, cw_ref, pvec_ref, o_ref, bg_ref, stage_ref, *, blocks_per_seq, n_heads, scale):
    i = pl.program_id(0)
    start = (i % blocks_per_seq) == 0
    d = xq_ref.shape[1]
    hd = d // n_heads
    parts = ((xq_ref, pq_ref, hq_ref), (xk_ref, pk_ref, hk_ref), (xv_ref, pv_ref, hv_ref))
    for c, (x_ref, p_ref, h_ref) in enumerate(parts):
        prev8 = jnp.where(start, h_ref[...], p_ref[...])
        a = _silu(_dwconv(x_ref[...], prev8, cw_ref[:, c * d:(c + 1) * d], stage_ref))
        for hh in range(n_heads):
            ah = a[:, hh * hd:(hh + 1) * hd]
            if c < 2:
                ah = ah * lax.rsqrt(jnp.sum(ah * ah, axis=-1, keepdims=True) + EPS)
            if c == 0:
                ah = ah * scale
            o_ref[:, c * d + hh * hd:c * d + (hh + 1) * hd] = ah
    ax = aux_ref[...]
    lane = lax.broadcasted_iota(jnp.int32, ax.shape, 1)
    beta = _sigmoid(ax)
    gdec = -jnp.exp(pvec_ref[0:1, :]) * _softplus(ax + pvec_ref[1:2, :])
    bg_ref[...] = jnp.where(lane < n_heads, beta, jnp.where(lane < 2 * n_heads, gdec, 0.0))


def _gdn_act(proj, aux, row0, n_seq, t, tb, hist8, conv_w, pvec, n_heads):
    d = proj.shape[1] // 4
    hd = d // n_heads
    bps = t // tb
    r0 = row0 // tb
    p8 = tb // SUBLANES

    def xspec(c):
        return pl.BlockSpec((tb, d), lambda i, c=c: (r0 + i, c))

    def pspec(c):
        return pl.BlockSpec((SUBLANES, d), lambda i, c=c: (jnp.maximum((r0 + i) * p8 - 1, 0), c))

    def hspec(c):
        return pl.BlockSpec((None, SUBLANES, d), lambda i, c=c: (i // bps, 0, c))

    m = n_seq * t
    return pl.pallas_call(
        functools.partial(_gdn_act_kernel, blocks_per_seq=bps, n_heads=n_heads, scale=hd ** -0.5),
        grid=(m // tb,),
        in_specs=[xspec(0), xspec(1), xspec(2), pspec(0), pspec(1), pspec(2), hspec(0), hspec(1), hspec(2),
                  pl.BlockSpec((tb, LANES), lambda i: (r0 + i, 0)),
                  pl.BlockSpec((CONV_W, 3 * d), lambda i: (0, 0)),
                  pl.BlockSpec((2, LANES), lambda i: (0, 0))],
        out_specs=[pl.BlockSpec((tb, 3 * d), lambda i: (i, 0)),
                   pl.BlockSpec((tb, LANES), lambda i: (i, 0))],
        out_shape=[jax.ShapeDtypeStruct((m, 3 * d), F32), jax.ShapeDtypeStruct((m, LANES), F32)],
        scratch_shapes=[pltpu.VMEM((SUBLANES + tb, d), F32)],
        compiler_params=_cp("parallel"),
        name="gdn_act",
    )(proj, proj, proj, proj, proj, proj, hist8, hist8, hist8, aux, conv_w, pvec)


def _seg_cumsum(x, rowmod, reverse):
    n = x.shape[0]
    d = 1
    while d < CHUNK:
        if reverse:
            sh = pltpu.roll(x, n - d, axis=0)
            x = x + jnp.where(rowmod < CHUNK - d, sh, 0.0)
        else:
            sh = pltpu.roll(x, d, axis=0)
            x = x + jnp.where(rowmod >= d, sh, 0.0)
        d *= 2
    return x


def _gdn_delta_kernel(act_ref, z_ref, bg_ref, s0_ref, go_ref, o_ref, sout_ref, s_ref, *, n_heads, chained, head_group):
    i = pl.program_id(1)
    tb, d = z_ref.shape
    hd = d // n_heads
    nch = tb // CHUNK
    dot = functools.partial(jnp.dot, preferred_element_type=F32)

    if chained:
        @pl.when(i == 0)
        def _():
            s_ref[...] = s0_ref[...]

    bg = bg_ref[...]
    lane = lax.broadcasted_iota(jnp.int32, (tb, hd), 1)
    rowmod = lax.broadcasted_iota(jnp.int32, (tb, hd), 0) & (CHUNK - 1)
    ri = lax.broadcasted_iota(jnp.int32, (tb, tb), 0)
    ci = lax.broadcasted_iota(jnp.int32, (tb, tb), 1)
    same = (ri >> 6) == (ci >> 6)
    low_incl = same & (ri >= ci)
    strict = same & (ri > ci)
    eye_c = (lax.broadcasted_iota(jnp.int32, (CHUNK, tb), 0)
             == (lax.broadcasted_iota(jnp.int32, (CHUNK, tb), 1) & (CHUNK - 1))).astype(F32)
    def compact(x):
        acc = x[0:CHUNK]
        for c in range(1, nch):
            acc = acc + x[c * CHUNK:(c + 1) * CHUNK]
        return acc

    def expand(xc):
        return jnp.where(same, jnp.concatenate([xc] * nch, axis=0), jnp.zeros((), xc.dtype))

    def each(fn, *lists):
        return [fn(*a) for a in zip(*lists)]

    cum_fwd = _seg_cumsum(bg, rowmod, reverse=False)
    cum_rev = _seg_cumsum(bg, rowmod, reverse=True) - bg

    def column(arr, idx):
        return jnp.broadcast_to(jnp.sum(jnp.where(lane == idx, arr, 0.0), axis=-1, keepdims=True), (tb, hd))

    def gc_diff(gc):
        gh, gm, gl = (piece.astype(F32) for piece in _split3(gc))
        lhs = jnp.where(lane == 0, gh, jnp.where(lane == 1, gm, jnp.where(lane == 2, gl,
                                                                          jnp.where(lane < 6, 1.0, 0.0))))
        rhs_t = jnp.where(lane < 3, 1.0, jnp.where(lane == 3, -gh, jnp.where(lane == 4, -gm,
                                                                             jnp.where(lane == 5, -gl, 0.0))))
        return _mm_nt(lhs, rhs_t)

    for h0 in range(0, n_heads, head_group):
        hs = list(range(h0, h0 + head_group))
        q = [act_ref[:, h * hd:(h + 1) * hd] for h in hs]
        k = [act_ref[:, d + h * hd:d + (h + 1) * hd] for h in hs]
        v = [act_ref[:, 2 * d + h * hd:2 * d + (h + 1) * hd] for h in hs]
        beta = [column(bg, h) for h in hs]
        gc = [column(cum_fwd, n_heads + h) for h in hs]
        grev = [column(cum_rev, n_heads + h) for h in hs]
        decay = each(lambda x: jnp.exp(jnp.where(low_incl, gc_diff(x), -jnp.inf)), gc)
        kb = each(lambda a, b: a * b, k, beta)
        lmat = each(lambda a, b, dc: jnp.where(strict, _mm_nt(a, b) * dc, 0.0), kb, k, decay)
        egc = each(jnp.exp, gc)
        rhs = each(lambda a, b, c_, e: jnp.concatenate([a * b, c_ * e], axis=1), v, beta, kb, egc)

        xc = each(compact, lmat)
        tinv = each(lambda x: eye_c - x, xc)
        xh = each(lambda x: x.astype(BF16), xc)
        xe = each(expand, xh)
        p = 1
        while 2 * p < CHUNK:
            xh = each(lambda a, b: dot(a, b).astype(BF16), xh, xe)
            xe = each(expand, xh)
            tinv = each(lambda t, b: t + dot(t.astype(BF16), b), tinv, xe)
            p *= 2
        sol = each(lambda t, r: dot(expand(t.astype(BF16)), r.astype(BF16)), tinv, rhs)

        qk = each(lambda a, b, dc: jnp.where(low_incl, _mm_nt(a, b) * dc, 0.0), q, k, decay)
        q_dec = each(lambda a, e: a * e, q, egc)
        k_dec = each(lambda a, r: a * jnp.exp(r), k, grev)
        g_last = each(lambda a, r: jnp.exp(a + r), gc, grev)

        v_news = [[] for _ in hs]
        o_inter = [[] for _ in hs]
        s = [s_ref[h] for h in hs] if chained else None
        for c in range(nch):
            rc = slice(c * CHUNK, (c + 1) * CHUNK)
            if not chained:
                s = [s0_ref[c, h] for h in hs]
            v_new = each(lambda so, st: so[rc, :hd] - _mm(so[rc, hd:], st), sol, s)
            o_c = each(lambda qd, st: _mm(qd[rc], st), q_dec, s)
            s = each(lambda st, gl, kd, vn: st * gl[c * CHUNK:c * CHUNK + 1, :] + _mm_tn(kd[rc], vn),
                     s, g_last, k_dec, v_new)
            for j, h in enumerate(hs):
                v_news[j].append(v_new[j])
                o_inter[j].append(o_c[j])
                if not chained:
                    sout_ref[c, h] = s[j]
        if chained:
            for j, h in enumerate(hs):
                s_ref[h] = s[j]

            @pl.when(i == pl.num_programs(1) - 1)
            def _(s=s, hs=hs):
                for j, h in enumerate(hs):
                    sout_ref[h] = s[j]

        o = each(lambda oi, m, vn: jnp.concatenate(oi, axis=0) + _mm(m, jnp.concatenate(vn, axis=0)),
                 o_inter, qk, v_news)
        for j, h in enumerate(hs):
            o_ref[:, h * hd:(h + 1) * hd] = _rms(o[j], go_ref[...]) * _silu(z_ref[:, h * hd:(h + 1) * hd])


def _gdn_delta(act, bg, zsrc, zcol, n_seq, t, s0, g_o, n_heads, out_rows=None):
    d = act.shape[1] // 3
    hd = d // n_heads
    tb = 4 * CHUNK
    chained = t > CHUNK
    if chained:
        assert t % tb == 0
        grid = (n_seq, t // tb)
        rows = lambda b, i: b * (t // tb) + i
        sspec = pl.BlockSpec((None, n_heads, hd, hd), lambda b, i: (b, 0, 0, 0))
    else:
        assert t == CHUNK and n_seq % (tb // CHUNK) == 0
        grid = (n_seq * t // tb, 1)
        rows = lambda b, i: b
        sspec = pl.BlockSpec((tb // CHUNK, n_heads, hd, hd), lambda b, i: (b, 0, 0, 0))
    return pl.pallas_call(
        functools.partial(_gdn_delta_kernel, n_heads=n_heads, chained=chained, head_group=8),
        grid=grid,
        in_specs=[pl.BlockSpec((tb, 3 * d), lambda b, i: (rows(b, i), 0)),
                  pl.BlockSpec((tb, d), lambda b, i: (rows(b, i), zcol)),
                  pl.BlockSpec((tb, LANES), lambda b, i: (rows(b, i), 0)),
                  sspec,
                  pl.BlockSpec((1, hd), lambda b, i: (0, 0))],
        out_specs=[pl.BlockSpec((tb, d), lambda b, i: (rows(b, i), 0)), sspec],
        out_shape=[jax.ShapeDtypeStruct((out_rows or n_seq * t, d), F32),
                   jax.ShapeDtypeStruct((n_seq, n_heads, hd, hd), F32)],
        scratch_shapes=[pltpu.VMEM((n_heads, hd, hd), F32)],
        compiler_params=_cp("parallel", "arbitrary"),
        name="gdn_delta",
    )(act, zsrc, bg, s0, g_o.reshape(1, hd))


def _rglru_kernel(h_ref, g_ref, win_ref, cw_ref, cb_ref, wgate_ref, bra_ref, bri_ref, lam_ref, wout_ref,
                  hist_ref, h0_ref, oin_ref, o_ref, hlast_ref, xtail_ref, cx_ref, ch_ref, *, n_blocks):
    del oin_ref
    i = pl.program_id(1)
    tb, d = h_ref.shape
    cw = lam_ref.shape[1]
    blk = cw // n_blocks

    @pl.when(i == 0)
    def _():
        cx_ref[...] = hist_ref[...]
        ch_ref[...] = h0_ref[...]

    h = h_ref[...]
    proj = jnp.dot(_rms(h, g_ref[...]).astype(BF16), win_ref[...], preferred_element_type=F32)
    y_br = proj[:, :cw]
    x_br = proj[:, cw:]
    xc = _dwconv(x_br, cx_ref[...], cw_ref[...]) + cb_ref[...]
    r_parts, i_parts = [], []
    for n in range(n_blocks):
        ri = jnp.dot(xc[:, n * blk:(n + 1) * blk].astype(BF16), wgate_ref[n], preferred_element_type=F32)
        r_parts.append(ri[:, :blk])
        i_parts.append(ri[:, blk:])
    r = _sigmoid(jnp.concatenate(r_parts, axis=1) + bra_ref[...])
    ig = _sigmoid(jnp.concatenate(i_parts, axis=1) + bri_ref[...])
    log_a = -LRU_C * r * _softplus(-lam_ref[...])
    a = jnp.exp(log_a)
    th = jnp.tanh(log_a)
    b = jnp.sqrt(-2.0 * th / (1.0 - th)) * (ig * xc)
    row8 = lax.broadcasted_iota(jnp.int32, (tb, cw), 0) & (SUBLANES - 1)
    step = 1
    while step < SUBLANES:
        a_s = jnp.where(row8 >= step, pltpu.roll(a, step, axis=0), 1.0)
        b_s = jnp.where(row8 >= step, pltpu.roll(b, step, axis=0), 0.0)
        b = a * b_s + b
        a = a * a_s
        step *= 2
    h_in = ch_ref[SUBLANES - 1:SUBLANES, :]
    groups = []
    for j in range(tb // SUBLANES):
        rows = slice(j * SUBLANES, (j + 1) * SUBLANES)
        hg = b[rows] + a[rows] * h_in
        groups.append(hg)
        h_in = hg[SUBLANES - 1:SUBLANES, :]
    hs = jnp.concatenate(groups, axis=0)
    cx_ref[...] = x_br[tb - SUBLANES:, :]
    ch_ref[...] = hs[tb - SUBLANES:, :]

    @pl.when(i == pl.num_programs(1) - 1)
    def _():
        hlast_ref[...] = hs[tb - SUBLANES:, :]
        xtail_ref[...] = x_br[tb - SUBLANES:, :]

    o_ref[...] = h + jnp.dot((hs * _gelu(y_br)).astype(BF16), wout_ref[...], preferred_element_type=F32)


def _rglru(h, h_out, row0, n_seq, t, tb, g, w_in, conv_w, conv_b, wgate, b_ra, b_ri, lam, w_out, hist8, h08):
    m, d = h.shape
    cw = lam.shape[0]
    nb = wgate.shape[0]
    nt = t // tb
    r0 = row0 // tb
    vec = lambda x: x.reshape(1, -1)
    cst = lambda shape: pl.BlockSpec(shape, lambda b, i: (0,) * len(shape))
    in_specs = [pl.BlockSpec((tb, d), lambda b, i: (r0 + b * nt + i, 0)),
                cst((1, d)), cst((d, 2 * cw)), cst((CONV_W, cw)), cst((1, cw)),
                cst(wgate.shape), cst((1, cw)), cst((1, cw)), cst((1, cw)), cst((cw, d)),
                pl.BlockSpec((None, SUBLANES, cw), lambda b, i: (b, 0, 0)),
                pl.BlockSpec((None, SUBLANES, cw), lambda b, i: (b, 0, 0))]
    args = [h, vec(g), w_in, conv_w, vec(conv_b), wgate, vec(b_ra), vec(b_ri), vec(lam), w_out, hist8, h08]
    aliases = {}
    if h_out is not None:
        in_specs.append(pl.BlockSpec(memory_space=pl.ANY))
        args.append(h_out)
        aliases = {len(args) - 1: 0}
    kern = functools.partial(_rglru_kernel, n_blocks=nb)
    if h_out is None:
        kern = lambda *refs: _rglru_kernel(*refs[:12], None, *refs[12:], n_blocks=nb)
    return pl.pallas_call(
        kern,
        grid=(n_seq, nt),
        in_specs=in_specs,
        out_specs=[pl.BlockSpec((tb, d), lambda b, i: (r0 + b * nt + i, 0)),
                   pl.BlockSpec((None, SUBLANES, cw), lambda b, i: (b, 0, 0)),
                   pl.BlockSpec((None, SUBLANES, cw), lambda b, i: (b, 0, 0))],
        out_shape=[jax.ShapeDtypeStruct((m, d), F32),
                   jax.ShapeDtypeStruct((n_seq, SUBLANES, cw), F32),
                   jax.ShapeDtypeStruct((n_seq, SUBLANES, cw), F32)],
        scratch_shapes=[pltpu.VMEM((SUBLANES, cw), F32), pltpu.VMEM((SUBLANES, cw), F32)],
        input_output_aliases=aliases,
        compiler_params=_cp("parallel", "arbitrary"),
        name="rglru",
    )(*args)


def _gmlp_kernel(h_ref, g_ref, win_ref, bin_ref, lng_ref, lnb_ref, ws_ref, bs_ref, wout_ref, o_ref, vn_ref,
                 *, n_groups, first_sample_tile, lc_prompt, lc_sample):
    i = pl.program_id(0)
    tm, d = h_ref.shape
    half = lng_ref.shape[1]
    gw = half // n_groups
    h = h_ref[...]
    act = _gelu(jnp.dot(_rms(h, g_ref[...]).astype(BF16), win_ref[...], preferred_element_type=F32) + bin_ref[...])
    u = act[:, :half]
    v = act[:, half:]
    vc = v - jnp.mean(v, axis=-1, keepdims=True)
    vn = vc * lax.rsqrt(jnp.mean(vc * vc, axis=-1, keepdims=True) + EPS) * lng_ref[...] + lnb_ref[...]
    vn_ref[...] = vn
    shift = jnp.where(i >= first_sample_tile, lc_sample.bit_length() - 1, lc_prompt.bit_length() - 1)
    ri = lax.broadcasted_iota(jnp.int32, (tm, tm), 0)
    ci = lax.broadcasted_iota(jnp.int32, (tm, tm), 1)
    keep = (lax.shift_right_logical(ri, shift) == lax.shift_right_logical(ci, shift)) & (ri >= ci)
    vnb = vn.astype(BF16)
    parts = []
    for gi in range(n_groups):
        ws = jnp.where(keep, ws_ref[0, gi], 0.0).astype(BF16)
        sg = jnp.dot(ws, vnb[:, gi * gw:(gi + 1) * gw], preferred_element_type=F32)
        parts.append(u[:, gi * gw:(gi + 1) * gw] * (sg + bs_ref[0, :, gi * gw:(gi + 1) * gw]))
    gated = jnp.concatenate(parts, axis=1).astype(BF16)
    o_ref[...] = h + jnp.dot(gated, wout_ref[...], preferred_element_type=F32)


def _gmlp(h, m_p, t_s, g, w_in, b_in, ln_g, ln_b, w_s, b_s, w_out):
    m, d = h.shape
    half = ln_g.shape[0]
    ng = w_s.shape[0]
    gw = half // ng
    tm = 256
    lc_p, lc_s = D_CHUNK, min(t_s, D_CHUNK)
    assert m_p % tm == 0 and (m - m_p) % tm == 0 and tm % lc_p == 0 and tm % lc_s == 0

    def tiled(lc):
        wt = jnp.tile(w_s[:, :lc, :lc], (1, tm // lc, tm // lc))
        bt = jnp.repeat(jnp.tile(b_s[:, :lc].T, (tm // lc, 1)), gw, axis=1)
        return wt, bt

    wt_p, bt_p = tiled(lc_p)
    wt_s, bt_s = tiled(lc_s)
    ws_all = jnp.stack([wt_p, wt_s])
    bs_all = jnp.stack([bt_p, bt_s])
    fst = m_p // tm
    vec = lambda x: x.reshape(1, -1)
    cst = lambda shape: pl.BlockSpec(shape, lambda i: (0,) * len(shape))
    return pl.pallas_call(
        functools.partial(_gmlp_kernel, n_groups=ng, first_sample_tile=fst, lc_prompt=lc_p, lc_sample=lc_s),
        grid=(m // tm,),
        in_specs=[pl.BlockSpec((tm, d), lambda i: (i, 0)),
                  cst((1, d)), cst((d, 2 * half)), cst((1, 2 * half)), cst((1, half)), cst((1, half)),
                  pl.BlockSpec((1, ng, tm, tm), lambda i: (i // fst, 0, 0, 0)),
                  pl.BlockSpec((1, tm, half), lambda i: (i // fst, 0, 0)),
                  cst((half, d))],
        out_specs=[pl.BlockSpec((tm, d), lambda i: (i, 0)),
                   pl.BlockSpec((tm, half), lambda i: (jnp.maximum(i - fst, 0), 0))],
        out_shape=[jax.ShapeDtypeStruct((m, d), F32),
                   jax.ShapeDtypeStruct((m - m_p, half), F32)],
        compiler_params=_cp("arbitrary"),
        name="gmlp",
    )(h, vec(g), w_in, vec(b_in), vec(ln_g), vec(ln_b), ws_all, bs_all, w_out)


def _pad_rows8(x3):
    n, r, c = x3.shape
    return jnp.concatenate([jnp.zeros((n, SUBLANES - r, c), x3.dtype), x3], axis=1)


def kernel(x_prompt, x_sample, p_prompt, p_sample, cache_k_a, cache_v_a, state_s_b, state_conv_b, state_h_c, state_conv_c, g_mix, g_ffn, g_ple, g_final, w_in_a, g_q_a, g_k_a, rel_a, w_out_a, w_in_b, conv_b, a_log_b, dt_bias_b, g_o_b, w_out_b, w_in_c, conv_c, conv_bias_c, w_ra_c, b_ra_c, w_ri_c, b_ri_c, lam_c, w_out_c, w_in_d, b_in_d, ln_g_d, ln_b_d, w_s_d, b_s_d, w_out_d, w_gate_ff, w_up_ff, w_down_ff, w_router, w_gate_e, w_up_e, w_down_e, w_ple, w_ple_gate):
    n_b, t, d = x_prompt.shape
    n_s, t_s, _ = x_sample.shape
    depth = p_prompt.shape[0]
    pd = p_prompt.shape[-1]
    m_p, m_s = n_b * t, n_s * t_s
    m = m_p + m_s
    a_heads = d // g_q_a.shape[-1]
    b_heads = a_log_b.shape[-1]
    n_exp = w_router.shape[-1]
    bf = lambda x: x.astype(BF16)

    h, h_tail = x_prompt.reshape(m_p, d), x_sample.reshape(m_s, d)
    pp_all = p_prompt.reshape(depth, m_p, pd)
    ps_all = p_sample.reshape(depth, m_s, pd)

    outs = {name: [] for name in ("k_a_p", "v_a_p", "k_a_s", "v_a_s", "s_b_p", "conv_b_p", "s_b_s", "conv_b_s",
                                  "h_c_p", "conv_c_p", "h_c_s", "conv_c_s", "v_d_s")}

    def tail_rows(x2, n_seq, tt, row0, cols):
        keep = CONV_W - 1
        if n_seq <= 4:
            return jnp.stack([x2[row0 + (b + 1) * tt - keep:row0 + (b + 1) * tt, :cols] for b in range(n_seq)])
        return x2[row0:row0 + n_seq * tt, :cols].reshape(n_seq, tt, cols)[:, tt - keep:]

    y = None
    for li in range(depth):
        kind, c = li % 4, li // 4
        if kind == 0:
            qkv = _proj(h, g_mix[li], bf(w_in_a[c]), 1024, tail=h_tail)
            w_keep = cache_k_a.shape[2]
            o, kn_p, kn_s = _attention(qkv, n_b, t, n_s, t_s,
                                       cache_k_a[c].reshape(n_s, w_keep, d), cache_v_a[c].reshape(n_s, w_keep, d),
                                       rel_a[c], g_q_a[c], g_k_a[c], a_heads)
            hd = d // a_heads
            keep = min(A_PREV_CHUNKS * CHUNK, t)
            v_p = jnp.stack([qkv[(b + 1) * t - keep:(b + 1) * t, 2 * d:] for b in range(n_b)])
            outs["k_a_p"].append(kn_p.reshape(n_b, keep, a_heads, hd))
            outs["v_a_p"].append(v_p.reshape(n_b, keep, a_heads, hd))
            outs["k_a_s"].append(kn_s.reshape(n_s, t_s, a_heads, hd))
            outs["v_a_s"].append(qkv[m_p:, 2 * d:].reshape(n_s, t_s, a_heads, hd))
            h = _outproj_res(h, o, bf(w_out_a[c]), tail=h_tail)
            h_tail = None
        elif kind == 1:
            w_in = w_in_b[c]
            nq = 3 * d
            w_main = bf(w_in[:, :nq + d])
            w_aux = bf(jnp.pad(w_in[:, nq + d:], ((0, 0), (0, LANES - 2 * b_heads))))
            proj, aux = _proj(h, g_mix[li], w_main, 1024, w_aux)
            pvec = jnp.zeros((2, LANES), F32)
            pvec = pvec.at[0, b_heads:2 * b_heads].set(a_log_b[c]).at[1, b_heads:2 * b_heads].set(dt_bias_b[c])
            hd = d // b_heads
            act_p, bg_p = _gdn_act(proj, aux, 0, n_b, t, 512, jnp.zeros((n_b, SUBLANES, nq), F32),
                                   conv_b[c], pvec, b_heads)
            o_p, s_p = _gdn_delta(act_p, bg_p, proj, 3, n_b, t,
                                  jnp.zeros((n_b, b_heads, hd, hd), F32), g_o_b[c], b_heads, out_rows=m)
            act_s, bg_s = _gdn_act(proj, aux, m_p, n_s, t_s, t_s, _pad_rows8(state_conv_b[c]),
                                   conv_b[c], pvec, b_heads)
            tp = -(-t_s // CHUNK) * CHUNK
            padt = lambda x2: jnp.pad(x2.reshape(n_s, t_s, -1), ((0, 0), (0, tp - t_s), (0, 0))).reshape(n_s * tp, -1)
            z_s = padt(proj[m_p:, nq:nq + d])
            o_s, s_s = _gdn_delta(padt(act_s), padt(bg_s), z_s, 0, n_s, tp, state_s_b[c], g_o_b[c], b_heads)
            o_s = o_s.reshape(n_s, tp, d)[:, :t_s].reshape(m_s, d)
            outs["s_b_p"].append(s_p)
            outs["s_b_s"].append(s_s)
            outs["conv_b_p"].append(tail_rows(proj, n_b, t, 0, nq))
            outs["conv_b_s"].append(tail_rows(proj, n_s, t_s, m_p, nq))
            h = _outproj_res(h, lax.dynamic_update_slice(o_p, o_s, (m_p, 0)), bf(w_out_b[c]))
        elif kind == 2:
            cw = lam_c.shape[-1]
            wgate = bf(jnp.concatenate([w_ra_c[c], w_ri_c[c]], axis=-1))
            common = (g_mix[li], bf(w_in_c[c]), conv_c[c], conv_bias_c[c], wgate, b_ra_c[c], b_ri_c[c],
                      lam_c[c], bf(w_out_c[c]))
            zeros8 = jnp.zeros((n_b, SUBLANES, cw), F32)
            h_new, hl_p, xt_p = _rglru(h, None, 0, n_b, t, 512, *common, zeros8, zeros8)
            h_new, hl_s, xt_s = _rglru(h, h_new, m_p, n_s, t_s, t_s, *common,
                                       _pad_rows8(state_conv_c[c]), _pad_rows8(state_h_c[c][:, None, :]))
            h = h_new
            outs["h_c_p"].append(hl_p[:, SUBLANES - 1])
            outs["h_c_s"].append(hl_s[:, SUBLANES - 1])
            outs["conv_c_p"].append(xt_p[:, SUBLANES - (CONV_W - 1):])
            outs["conv_c_s"].append(xt_s[:, SUBLANES - (CONV_W - 1):])
        else:
            h, vn_s = _gmlp(h, m_p, t_s, g_mix[li], bf(w_in_d[c]), b_in_d[c], ln_g_d[c], ln_b_d[c],
                            w_s_d[c], b_s_d[c], bf(w_out_d[c]))
            outs["v_d_s"].append(vn_s.reshape(n_s, t_s, -1))

        j = li // 2
        if li % 2 == 0:
            h = _ffn_dense(h, g_ffn[li], bf(w_gate_ff[j]), bf(w_up_ff[j]), bf(w_down_ff[j]))
        else:
            wr = jnp.pad(w_router[j], ((0, 0), (0, LANES - n_exp)))
            h = _moe(h, g_ffn[li], wr, bf(w_gate_e[j]), bf(w_up_e[j]), bf(w_down_e[j]))
        last = li == depth - 1
        res = _ple(h, pp_all, ps_all, li, g_ple[li], bf(w_ple[li]), bf(w_ple_gate[li]),
                   g_final if last else None)
        if last:
            y = res
        else:
            h = res

    st = {name: jnp.stack(rows) for name, rows in outs.items()}
    return (y[0].reshape(n_b, t, d), y[1].reshape(n_s, t_s, d),
            st["k_a_p"], st["v_a_p"], st["k_a_s"], st["v_a_s"],
            st["s_b_p"], st["conv_b_p"], st["s_b_s"], st["conv_b_s"],
            st["h_c_p"], st["conv_c_p"], st["h_c_s"], st["conv_c_s"],
            st["v_d_s"])
```
